```python
import jax, jax.numpy as jnp
from jax import lax
import numpy as np

D_MODEL = 2048
BATCH = 1
SEQ = 8192
DEPTH = 1
DEC_BATCH = 16
DEC_SEQ = 32
PAST_LEN = 2048

CHUNK = 64
N_HEADS = 16
HEAD_DIM = D_MODEL // N_HEADS
N_KV_HEADS = 4
ROT_DIM = HEAD_DIM // 4
ROPE_THETA = 500000.0
IDX_HEADS = 16
IDX_DIM = 128
IDX_ROT_DIM = IDX_DIM // 4
TOPK_MAX = 256
Q_BLOCK = 128
HG_HEADS = 16
HG_DK = D_MODEL // HG_HEADS
HG_DV = D_MODEL // HG_HEADS
D_FF = 4 * D_MODEL
EPS = 1e-6
ADA_SCALE = 0.2

SPLIT_SIZES = (N_HEADS * HEAD_DIM, N_KV_HEADS * HEAD_DIM, N_KV_HEADS * HEAD_DIM,
               IDX_HEADS * IDX_DIM, IDX_DIM, IDX_HEADS,
               HG_HEADS * HG_DK, HG_HEADS * HG_DK, HG_HEADS * HG_DV, HG_HEADS * HG_DV,
               D_MODEL, D_MODEL)
D_IN = (N_HEADS * HEAD_DIM + 2 * N_KV_HEADS * HEAD_DIM + IDX_HEADS * IDX_DIM + IDX_DIM + IDX_HEADS
        + 2 * HG_HEADS * HG_DK + 2 * HG_HEADS * HG_DV + 2 * D_MODEL)

kernel_name = 'hybrid_dsa_hgrn2_streaming_step'


def rmsnorm(x, g):
    xf = x.astype(jnp.float32)
    return (xf * lax.rsqrt(jnp.mean(xf * xf, axis=-1, keepdims=True) + EPS) * g.astype(jnp.float32)).astype(x.dtype)


def rope_partial(x, pos, rot_dim):
    half = rot_dim // 2
    inv_freq = ROPE_THETA ** (-(jnp.arange(half, dtype=jnp.float32) * (2.0 / rot_dim)))
    ang = pos.astype(jnp.float32)[:, None] * inv_freq[None, :]
    cos = jnp.cos(ang)[:, None, :]
    sin = jnp.sin(ang)[:, None, :]
    xf = x[..., :rot_dim].astype(jnp.float32)
    x1, x2 = xf[..., :half], xf[..., half:]
    rot = jnp.concatenate([x1 * cos - x2 * sin, x2 * cos + x1 * sin], axis=-1).astype(x.dtype)
    return jnp.concatenate([rot, x[..., rot_dim:]], axis=-1)


def dsa_attend(q, qi, wi, qpos, k, v, ki, kpos, topk):
    B, T = q.shape[:2]
    vis = (kpos[None, :] // CHUNK) <= (qpos[:, None] // CHUNK)
    logits = jnp.einsum('bthd,bld->bthl', qi, ki).astype(jnp.float32)
    score = jnp.einsum('bthl,bth->btl', jax.nn.relu(logits), wi.astype(jnp.float32))
    score = jnp.where(vis[None], score, -jnp.inf)
    top_val, top_idx = lax.top_k(score, topk)
    valid = jnp.isfinite(top_val)
    kg = jax.vmap(lambda kb, ib: kb[ib])(k, top_idx)
    vg = jax.vmap(lambda vb, ib: vb[ib])(v, top_idx)
    qg = q.reshape(B, T, N_KV_HEADS, N_HEADS // N_KV_HEADS, HEAD_DIM)
    s = jnp.einsum('btkgd,btjkd->btkgj', qg, kg).astype(jnp.float32) * (HEAD_DIM ** -0.5)
    s = jnp.where(valid[:, :, None, None, :], s, -jnp.inf)
    p = jax.nn.softmax(s, axis=-1).astype(v.dtype)
    o = jnp.einsum('btkgj,btjkd->btkgd', p, vg)
    return o.reshape(B, T, N_HEADS * HEAD_DIM)


def dsa_full(q, qi, wi, k, v, ki, pos, topk):
    B, T = q.shape[:2]
    nb = T // Q_BLOCK

    def to_blocks(a):
        return jnp.swapaxes(a.reshape((B, nb, Q_BLOCK) + a.shape[2:]), 0, 1)

    def one_block(args):
        qb, qib, wib, pb = args
        return dsa_attend(qb, qib, wib, pb, k, v, ki, pos, topk)

    out = lax.map(one_block, (to_blocks(q), to_blocks(qi), to_blocks(wi), pos.reshape(nb, Q_BLOCK)))
    return jnp.swapaxes(out, 0, 1).reshape(B, T, N_HEADS * HEAD_DIM)


def hgrn_chunk(S, q, k, v, logf):
    c = q.shape[2]
    G = jnp.cumsum(logf, axis=2)
    causal = jnp.tril(jnp.ones((c, c), dtype=bool))
    decay = jnp.where(causal[:, :, None],
                      jnp.exp(jnp.minimum(G[:, :, :, None, :] - G[:, :, None, :, :], 0.0)), 0.0)
    A = jnp.einsum('bhtd,bhsd,bhtsd->bhts', q, k, decay)
    o = jnp.einsum('bhts,bhsv->bhtv', A, v) + jnp.einsum('bhtd,bhdv->bhtv', q * jnp.exp(G), S)
    G_last = G[:, :, -1:, :]
    S_new = jnp.exp(G_last[:, :, 0, :])[..., None] * S + jnp.einsum('bhsd,bhsv->bhdv', k * jnp.exp(G_last - G), v)
    return S_new, o


def hgrn_full(S0, q, k, v, logf):
    B, H, T, _ = q.shape
    n = T // CHUNK

    def to_chunks(a):
        return jnp.moveaxis(a.reshape(B, H, n, CHUNK, a.shape[-1]), 2, 0)

    S, o = lax.scan(lambda s, xs: hgrn_chunk(s, xs[0], xs[1], xs[2], xs[3]), S0,
                    (to_chunks(q), to_chunks(k), to_chunks(v), to_chunks(logf)))
    return S, jnp.moveaxis(o, 0, 2).reshape(B, H, T, HG_DV)


def trunk_layer(x, c, pos, past, w_ada, b_ada, norm1_w, w_in, lb, hg_norm_w, w_out, norm2_w, w_up, w_down):
    B, T, _ = x.shape
    mod = (c @ w_ada + b_ada).reshape(B, 6, D_MODEL)[:, :, None, :]
    sh1, sc1, g1, sh2, sc2, g2 = (mod[:, 0], mod[:, 1], mod[:, 2], mod[:, 3], mod[:, 4], mod[:, 5])

    h = rmsnorm(x, norm1_w) * (1.0 + sc1) + sh1
    z = h @ w_in
    q, k, v, qi, ki, wi, hq, hf, hi, hg, ga, gb = jnp.split(z, list(np.cumsum(SPLIT_SIZES)[:-1]), axis=-1)
    q = rope_partial(q.reshape(B, T, N_HEADS, HEAD_DIM), pos, ROT_DIM)
    k = rope_partial(k.reshape(B, T, N_KV_HEADS, HEAD_DIM), pos, ROT_DIM)
    v = v.reshape(B, T, N_KV_HEADS, HEAD_DIM)
    qi = rope_partial(qi.reshape(B, T, IDX_HEADS, IDX_DIM), pos, IDX_ROT_DIM)
    ki = rope_partial(ki.reshape(B, T, 1, IDX_DIM), pos, IDX_ROT_DIM)[:, :, 0, :]
    wi = wi * (IDX_HEADS ** -0.5 * IDX_DIM ** -0.5)

    f = lb + (1.0 - lb) * jax.nn.sigmoid(hf.astype(jnp.float32))

    def to_heads(a, d):
        return a.reshape(B, T, HG_HEADS, d).transpose(0, 2, 1, 3).astype(jnp.float32)

    hq_h, hk_h, hv_h, lf_h = to_heads(hq, HG_DK), to_heads(1.0 - f, HG_DK), to_heads(hi, HG_DV), to_heads(jnp.log(f), HG_DK)

    if past is None:
        o_attn = dsa_full(q, qi, wi, k, v, ki, pos, min(TOPK_MAX, T // 4))
        S0 = jnp.zeros((B, HG_HEADS, HG_DK, HG_DV), jnp.float32)
        S_new, o_hg = hgrn_full(S0, hq_h, hk_h, hv_h, lf_h)
    else:
        ck, cv, cki, S0 = past
        L = ck.shape[1] + T
        kpos = jnp.arange(L, dtype=jnp.int32)
        o_attn = dsa_attend(q, qi, wi, pos, jnp.concatenate([ck, k], axis=1), jnp.concatenate([cv, v], axis=1),
                            jnp.concatenate([cki, ki], axis=1), kpos, min(TOPK_MAX, L // 4))
        S_new, o_hg = hgrn_chunk(S0.astype(jnp.float32), hq_h, hk_h, hv_h, lf_h)

    o_hg = o_hg.transpose(0, 2, 1, 3).astype(x.dtype)
    o_hg = rmsnorm(o_hg, hg_norm_w) * jax.nn.silu(hg.reshape(B, T, HG_HEADS, HG_DV))
    merged = jax.nn.sigmoid(ga) * o_attn + jax.nn.sigmoid(gb) * o_hg.reshape(B, T, D_MODEL)
    x = x + g1 * (merged @ w_out)

    h2 = rmsnorm(x, norm2_w) * (1.0 + sc2) + sh2
    x = x + g2 * (jnp.square(jax.nn.relu(h2 @ w_up)) @ w_down)
    return x, (k, v, ki, S_new)


def setup_inputs(seed: int = 0) -> dict:
    key = jax.random.key(seed)
    ks = jax.random.split(key, 20)

    def nrm(k, shape, s):
        return jax.random.normal(k, shape, jnp.float32) * s

    return {
        'x_prompt': nrm(ks[0], (BATCH, SEQ, D_MODEL), 1.0),
        'x_sample': nrm(ks[1], (DEC_BATCH, DEC_SEQ, D_MODEL), 1.0),
        'cache_k': nrm(ks[2], (DEPTH, DEC_BATCH, PAST_LEN, N_KV_HEADS, HEAD_DIM), 1.0),
        'cache_v': nrm(ks[3], (DEPTH, DEC_BATCH, PAST_LEN, N_KV_HEADS, HEAD_DIM), 1.0),
        'cache_ki': nrm(ks[4], (DEPTH, DEC_BATCH, PAST_LEN, IDX_DIM), 1.0),
        'state_hgrn': nrm(ks[5], (DEPTH, DEC_BATCH, HG_HEADS, HG_DK, HG_DV), 0.3),
        'c_prompt': nrm(ks[6], (BATCH, D_MODEL), 1.0),
        'c_sample': nrm(ks[7], (DEC_BATCH, D_MODEL), 1.0),
        'w_ada': nrm(ks[8], (DEPTH, D_MODEL, 6 * D_MODEL), ADA_SCALE * D_MODEL ** -0.5),
        'b_ada': nrm(ks[9], (DEPTH, 6 * D_MODEL), 0.02),
        'norm1_w': 1.0 + nrm(ks[10], (DEPTH, D_MODEL), 0.02),
        'w_in': nrm(ks[11], (DEPTH, D_MODEL, D_IN), D_MODEL ** -0.5),
        'hg_lb_logits': nrm(ks[12], (DEPTH + 1, HG_HEADS * HG_DK), 0.5),
        'hg_norm_w': 1.0 + nrm(ks[13], (DEPTH, HG_DV), 0.02),
        'w_out': nrm(ks[14], (DEPTH, D_MODEL, D_MODEL), D_MODEL ** -0.5),
        'norm2_w': 1.0 + nrm(ks[15], (DEPTH, D_MODEL), 0.02),
        'w_up': nrm(ks[16], (DEPTH, D_MODEL, D_FF), D_MODEL ** -0.5),
        'w_down': nrm(ks[17], (DEPTH, D_FF, D_MODEL), D_FF ** -0.5),
        'final_norm_w': 1.0 + nrm(ks[18], (D_MODEL,), 0.02),
    }


def reference(x_prompt, x_sample, cache_k, cache_v, cache_ki, state_hgrn, c_prompt, c_sample,
              w_ada, b_ada, norm1_w, w_in, hg_lb_logits, hg_norm_w, w_out, norm2_w, w_up, w_down, final_norm_w):
    lb_all = jnp.cumsum(jax.nn.softmax(hg_lb_logits.astype(jnp.float32), axis=0), axis=0)
    pos_p = jnp.arange(x_prompt.shape[1], dtype=jnp.int32)
    pos_s = cache_k.shape[2] + jnp.arange(x_sample.shape[1], dtype=jnp.int32)
    xp, xs = x_prompt, x_sample
    kp, vp, kip, sp, ks_, vs, kis, ss = [], [], [], [], [], [], [], []
    for l in range(DEPTH):
        prm = (w_ada[l], b_ada[l], norm1_w[l], w_in[l], lb_all[l], hg_norm_w[l], w_out[l], norm2_w[l], w_up[l], w_down[l])
        xp, (k1, v1, ki1, s1) = trunk_layer(xp, c_prompt, pos_p, None, *prm)
        xs, (k2, v2, ki2, s2) = trunk_layer(xs, c_sample, pos_s, (cache_k[l], cache_v[l], cache_ki[l], state_hgrn[l]), *prm)
        kp.append(k1); vp.append(v1); kip.append(ki1); sp.append(s1)
        ks_.append(k2); vs.append(v2); kis.append(ki2); ss.append(s2)
    y_prompt = rmsnorm(xp, final_norm_w)
    y_sample = rmsnorm(xs, final_norm_w)
    k_prompt, v_prompt, ki_prompt, hgrn_prompt = jnp.stack(kp), jnp.stack(vp), jnp.stack(kip), jnp.stack(sp)
    k_sample, v_sample, ki_sample, hgrn_sample = jnp.stack(ks_), jnp.stack(vs), jnp.stack(kis), jnp.stack(ss)
    return (y_prompt, y_sample, k_prompt, v_prompt, ki_prompt, hgrn_prompt, k_sample, v_sample, ki_sample, hgrn_sample)
```

```python
import functools

import jax
import jax.numpy as jnp
import numpy as np
from jax import lax
from jax.experimental import pallas as pl
from jax.experimental.pallas import tpu as pltpu

CHUNK = 64
N_HEADS = 16
HEAD_DIM = 128
N_KV_HEADS = 4
GROUP = N_HEADS // N_KV_HEADS
ROT_DIM = HEAD_DIM // 4
ROPE_THETA = 500000.0
IDX_HEADS = 16
IDX_DIM = 128
TOPK_MAX = 256
HG_HEADS = 16
HG_DK = 128
HG_DV = 128
EPS = 1e-6
LANES = 128
SUB_BLOCK = 16
VMEM_LIMIT = 56 * 1024 * 1024
NEG_INF = float("-inf")
POS_INF = float("inf")

F32 = jnp.float32
BF16 = jnp.bfloat16


def _cparams(sem):
    return pltpu.CompilerParams(dimension_semantics=sem, vmem_limit_bytes=VMEM_LIMIT)


def _dot_nt(a, b):
    return lax.dot_general(a, b, (((1,), (1,)), ((), ())), preferred_element_type=F32)


def _dot_tn(a, b):
    return lax.dot_general(a, b, (((0,), (0,)), ((), ())), preferred_element_type=F32)


def _dot(a, b):
    return jnp.dot(a, b, preferred_element_type=F32)


def _sigmoid(x):
    return 1.0 / (1.0 + jnp.exp(-x))


def _ada_kernel(c_ref, w_ref, b_ref, o_ref):
    o_ref[...] = _dot(c_ref[...].astype(BF16), w_ref[...].astype(BF16)) + b_ref[...]


def _ada(c_all, w_ada, b_ada, tn=1024):
    r, d = c_all.shape
    n = w_ada.shape[1]
    return pl.pallas_call(
        _ada_kernel,
        out_shape=jax.ShapeDtypeStruct((r, n), F32),
        grid=(n // tn,),
        in_specs=[pl.BlockSpec((r, d), lambda j: (0, 0)),
                  pl.BlockSpec((d, tn), lambda j: (0, j)),
                  pl.BlockSpec((1, tn), lambda j: (0, j))],
        out_specs=pl.BlockSpec((r, tn), lambda j: (0, j)),
        compiler_params=_cparams(("arbitrary",)),
        name="ada",
    )(c_all, w_ada, b_ada)


def _normmod_kernel(x_ref, sc_ref, sh_ref, g_ref, o_ref):
    x = x_ref[...]
    ms = jnp.mean(x * x, axis=-1, keepdims=True)
    xn = x * lax.rsqrt(ms + EPS) * g_ref[...]
    o_ref[...] = (xn * (1.0 + sc_ref[...]) + sh_ref[...]).astype(o_ref.dtype)


def _normmod(x, sc, sh, g, bb, tb):
    b, t, d = x.shape
    return pl.pallas_call(
        _normmod_kernel,
        out_shape=jax.ShapeDtypeStruct((b, t, d), BF16),
        grid=(b // bb, t // tb),
        in_specs=[pl.BlockSpec((bb, tb, d), lambda i, j: (i, j, 0)),
                  pl.BlockSpec((bb, 1, d), lambda i, j: (i, 0, 0)),
                  pl.BlockSpec((bb, 1, d), lambda i, j: (i, 0, 0)),
                  pl.BlockSpec((1, 1, d), lambda i, j: (0, 0, 0))],
        out_specs=pl.BlockSpec((bb, tb, d), lambda i, j: (i, j, 0)),
        compiler_params=_cparams(("arbitrary", "arbitrary")),
        name="normmod",
    )(x, sc, sh, g)


def _proj_plain_kernel(h_ref, w_ref, *o_refs, scale):
    z = _dot(h_ref[...], w_ref[...])
    if scale != 1.0:
        z = z * scale
    for o_ref in o_refs:
        o_ref[...] = z.astype(o_ref.dtype)


def _proj_rope_kernel(h_ref, w_ref, cos_ref, sup_ref, sdn_ref, *o_refs, scale):
    z = _dot(h_ref[...], w_ref[...])
    tn = z.shape[1]
    reps = tn // LANES

    def wide(ref):
        t = ref[...]
        return t if reps == 1 else jnp.concatenate([t] * reps, axis=1)

    up = pltpu.roll(z, tn - ROT_DIM // 2, 1)
    dn = pltpu.roll(z, ROT_DIM // 2, 1)
    r = z * wide(cos_ref) + up * wide(sup_ref) + dn * wide(sdn_ref)
    for o_ref in o_refs:
        if o_ref.dtype == BF16 and scale != 1.0:
            o_ref[...] = (r * scale).astype(BF16)
        else:
            o_ref[...] = r.astype(o_ref.dtype)


def _proj_forget_kernel(h_ref, w_ref, lbl_ref, lf_ref, kk_ref):
    z = _dot(h_ref[...], w_ref[...])
    lbl = lbl_ref[...]
    mx = jnp.max(lbl, axis=0, keepdims=True)
    e = jnp.exp(lbl - mx)
    lb = e[0:1, :] / jnp.sum(e, axis=0, keepdims=True)
    f = lb + (1.0 - lb) * _sigmoid(z)
    lf_ref[...] = jnp.log(f)
    kk_ref[...] = 1.0 - f


def _proj(kind, h, w, tm, tn, extra=(), out_dtypes=(F32,), scale=1.0):
    m, k = h.shape
    n = w.shape[1]
    in_specs = [pl.BlockSpec((tm, k), lambda i, j: (i, 0)),
                pl.BlockSpec((k, tn), lambda i, j: (0, j))]
    if kind == "rope":
        kern = functools.partial(_proj_rope_kernel, scale=scale)
        in_specs += [pl.BlockSpec((tm, LANES), lambda i, j: (i, 0))] * 3
    elif kind == "forget":
        kern = _proj_forget_kernel
        in_specs += [pl.BlockSpec((extra[0].shape[0], tn), lambda i, j: (0, j))]
    else:
        kern = functools.partial(_proj_plain_kernel, scale=scale)
    outs = tuple(jax.ShapeDtypeStruct((m, n), dt) for dt in out_dtypes)
    out_specs = tuple(pl.BlockSpec((tm, tn), lambda i, j: (i, j)) for _ in out_dtypes)
    res = pl.pallas_call(
        kern,
        out_shape=outs,
        grid=(m // tm, n // tn),
        in_specs=in_specs,
        out_specs=out_specs,
        compiler_params=_cparams(("arbitrary", "arbitrary")),
        name="proj_" + kind,
    )(h, w, *extra)
    return res


def _dsa_kernel(*refs, tq, tk, t_new, l_cache, pos0, topk, has_cache):
    if has_cache:
        (qi_ref, wi_ref, q_ref, kin_ref, kn_ref, vn_ref, cki_ref, ck_ref, cv_ref, o_ref,
         sc_ref, wb_ref, qg_ref, m_ref, l_ref, acc_ref, lo_ref, hi_ref, tau_ref, act_ref) = refs
    else:
        (qi_ref, wi_ref, q_ref, kin_ref, kn_ref, vn_ref, o_ref,
         sc_ref, wb_ref, qg_ref, m_ref, l_ref, acc_ref, lo_ref, hi_ref, tau_ref, act_ref) = refs
        cki_ref = ck_ref = cv_ref = None
    i = pl.program_id(1)
    ntc = l_cache // tk
    wn = min(tk, t_new)
    q0 = pos0 + i * tq
    rows = lax.broadcasted_iota(jnp.int32, (tq, 1), 0)
    qend = (((q0 + rows) >> 6) + 1) << 6
    last_end = (((q0 + tq - 1) >> 6) + 1) << 6
    nvis_new = jnp.minimum(last_end - pos0, t_new)
    ntn = (nvis_new + wn - 1) // wn
    ntiles = ntc + ntn

    wi = wi_ref[0]
    for h in range(IDX_HEADS):
        wb_ref[h] = jnp.broadcast_to(wi[:, h:h + 1], (tq, LANES))
    qblk = q_ref[0]
    for g in range(N_KV_HEADS):
        qg_ref[g] = jnp.concatenate(
            [qblk[:, (GROUP * g + a) * HEAD_DIM:(GROUP * g + a + 1) * HEAD_DIM] for a in range(GROUP)], axis=0)

    def score_tile(ki_tile, kpos0, w):
        acc = jnp.zeros((tq, w), F32)
        for h in range(IDX_HEADS):
            lg = _dot_nt(qi_ref[0, :, h * IDX_DIM:(h + 1) * IDX_DIM], ki_tile)
            wbh = wb_ref[h]
            wfull = wbh[:, :w] if w < LANES else (wbh if w == LANES else jnp.concatenate([wbh] * (w // LANES), axis=1))
            acc = acc + wfull * jnp.maximum(lg, 0.0)
        kpos = kpos0 + lax.broadcasted_iota(jnp.int32, (1, w), 1)
        vis = kpos < qend
        s = jnp.where(vis, acc, NEG_INF)
        smin = jnp.min(jnp.where(vis, acc, POS_INF), axis=1, keepdims=True)
        smax = jnp.max(s, axis=1, keepdims=True)
        return s, smax, smin

    def p1_cache(t, carry):
        mx, mn = carry
        r0 = pl.multiple_of(t * tk, tk)
        s, smax, smin = score_tile(cki_ref[0, pl.ds(r0, tk), :].astype(BF16), r0, tk)
        sc_ref[t] = s
        return jnp.maximum(mx, smax), jnp.minimum(mn, smin)

    def p1_new(j, carry):
        mx, mn = carry
        r0 = pl.multiple_of(j * wn, wn)
        s, smax, smin = score_tile(kin_ref[0, pl.ds(r0, wn), :], pos0 + r0, wn)
        if wn == tk:
            sc_ref[ntc + j] = s
        else:
            sc_ref[ntc + j] = jnp.full((tq, tk), NEG_INF, F32)
            sc_ref[ntc + j, :, 0:wn] = s
        return jnp.maximum(mx, smax), jnp.minimum(mn, smin)

    carry = (jnp.full((tq, 1), NEG_INF, F32), jnp.full((tq, 1), POS_INF, F32))
    if has_cache:
        carry = lax.fori_loop(0, ntc, p1_cache, carry)
    mx, mn = lax.fori_loop(0, ntn, p1_new, carry)

    nvis = jnp.clip(qend - pos0, 0, t_new)
    if has_cache:
        nvis = nvis + jnp.minimum(qend, l_cache)
    lo_ref[...] = mn
    hi_ref[...] = mx
    tau_ref[...] = mn
    act0 = jnp.where(nvis > topk, 1.0, 0.0)
    act_ref[...] = act0
    kf = float(topk)

    def bis_cond(carry):
        it, active = carry
        return jnp.logical_and(it < 200, active > 0)

    def bis_body(carry):
        it, _ = carry
        lo = lo_ref[...]
        hi = hi_ref[...]
        mid = lo * 0.5 + hi * 0.5
        midb = jnp.broadcast_to(mid, (tq, LANES))

        def cnt_body(t, cnt):
            s = sc_ref[t]
            for c in range(tk // LANES):
                cnt = cnt + jnp.where(s[:, c * LANES:(c + 1) * LANES] >= midb, 1.0, 0.0)
            return cnt

        cnt = lax.fori_loop(0, ntiles, cnt_body, jnp.zeros((tq, LANES), F32))
        c = jnp.sum(cnt, axis=1, keepdims=True)
        act = act_ref[...]
        found = jnp.where(c == kf, act, 0.0)
        stuck = jnp.where(jnp.logical_or(mid <= lo, mid >= hi), act, 0.0) * (1.0 - found)
        above = jnp.where(c > kf, act, 0.0)
        below = act * (1.0 - above) * (1.0 - found)
        tau = tau_ref[...]
        tau = jnp.where(found > 0, mid, tau)
        tau = jnp.where(stuck > 0, lo, tau)
        tau_ref[...] = tau
        lo_ref[...] = jnp.where(above > 0, mid, lo)
        hi_ref[...] = jnp.where(below > 0, mid, hi)
        nact = act * (1.0 - found) * (1.0 - stuck)
        act_ref[...] = nact
        return it + 1, (jnp.max(nact) > 0).astype(jnp.int32)

    lax.while_loop(bis_cond, bis_body, (jnp.int32(0), (jnp.max(act0) > 0).astype(jnp.int32)))
    tau = tau_ref[...]

    m_ref[...] = jnp.full(m_ref.shape, NEG_INF, F32)
    l_ref[...] = jnp.zeros(l_ref.shape, F32)
    acc_ref[...] = jnp.zeros(acc_ref.shape, F32)

    def attend(t, k_of_g, v_of_g, w):
        s_idx = sc_ref[t]
        if w < tk:
            s_idx = s_idx[:, 0:w]
        bias = jnp.where(s_idx >= tau, 0.0, NEG_INF)
        bias4 = jnp.concatenate([bias] * GROUP, axis=0)
        for g in range(N_KV_HEADS):
            s = _dot_nt(qg_ref[g], k_of_g(g)) + bias4
            m_old = m_ref[g]
            m_new = jnp.maximum(m_old, jnp.max(s, axis=1, keepdims=True))
            m_safe = jnp.where(m_new == NEG_INF, 0.0, m_new)
            alpha = jnp.exp(m_old - m_safe)
            p = jnp.exp(s - m_safe)
            l_ref[g] = alpha * l_ref[g] + jnp.sum(p, axis=1, keepdims=True)
            acc_ref[g] = alpha * acc_ref[g] + _dot(p.astype(BF16), v_of_g(g))
            m_ref[g] = m_new

    def p3_cache(t, carry):
        r0 = pl.multiple_of(t * tk, tk)
        attend(t,
               lambda g: ck_ref[0, pl.ds(r0, tk), g * HEAD_DIM:(g + 1) * HEAD_DIM].astype(BF16),
               lambda g: cv_ref[0, pl.ds(r0, tk), g * HEAD_DIM:(g + 1) * HEAD_DIM].astype(BF16), tk)
        return carry

    def p3_new(j, carry):
        r0 = pl.multiple_of(j * wn, wn)
        attend(ntc + j,
               lambda g: kn_ref[0, pl.ds(r0, wn), g * HEAD_DIM:(g + 1) * HEAD_DIM],
               lambda g: vn_ref[0, pl.ds(r0, wn), g * HEAD_DIM:(g + 1) * HEAD_DIM], wn)
        return carry

    if has_cache:
        lax.fori_loop(0, ntc, p3_cache, 0)
    lax.fori_loop(0, ntn, p3_new, 0)

    for g in range(N_KV_HEADS):
        og = acc_ref[g] / l_ref[g]
        for a in range(GROUP):
            hh = GROUP * g + a
            o_ref[0, :, hh * HEAD_DIM:(hh + 1) * HEAD_DIM] = og[a * tq:(a + 1) * tq, :]


def _dsa(qi, wi, q, kin, kn, vn, cache, pos0, tq, tk):
    b, t, _ = q.shape
    has_cache = cache is not None
    l_cache = cache[0].shape[1] if has_cache else 0
    topk = min(TOPK_MAX, (l_cache + t) // 4)
    wn = min(tk, t)
    nt = l_cache // tk + (t + wn - 1) // wn
    kvd = N_KV_HEADS * HEAD_DIM
    res = pl.Buffered(1) if b == 1 else None
    in_specs = [pl.BlockSpec((1, tq, IDX_HEADS * IDX_DIM), lambda bb, i: (bb, i, 0)),
                pl.BlockSpec((1, tq, LANES), lambda bb, i: (bb, i, 0)),
                pl.BlockSpec((1, tq, N_HEADS * HEAD_DIM), lambda bb, i: (bb, i, 0)),
                pl.BlockSpec((1, t, IDX_DIM), lambda bb, i: (bb, 0, 0), pipeline_mode=res),
                pl.BlockSpec((1, t, kvd), lambda bb, i: (bb, 0, 0), pipeline_mode=res),
                pl.BlockSpec((1, t, kvd), lambda bb, i: (bb, 0, 0), pipeline_mode=res)]
    args = [qi, wi, q, kin, kn, vn]
    if has_cache:
        in_specs += [pl.BlockSpec((1, l_cache, IDX_DIM), lambda bb, i: (bb, 0, 0)),
                     pl.BlockSpec((1, l_cache, kvd), lambda bb, i: (bb, 0, 0)),
                     pl.BlockSpec((1, l_cache, kvd), lambda bb, i: (bb, 0, 0))]
        args += list(cache)
    kern = functools.partial(_dsa_kernel, tq=tq, tk=tk, t_new=t, l_cache=l_cache, pos0=pos0, topk=topk,
                             has_cache=has_cache)
    return pl.pallas_call(
        kern,
        out_shape=jax.ShapeDtypeStruct((b, t, N_HEADS * HEAD_DIM), F32),
        grid=(b, t // tq),
        in_specs=in_specs,
        out_specs=pl.BlockSpec((1, tq, N_HEADS * HEAD_DIM), lambda bb, i: (bb, i, 0)),
        scratch_shapes=[pltpu.VMEM((nt, tq, tk), F32),
                        pltpu.VMEM((IDX_HEADS, tq, LANES), F32),
                        pltpu.VMEM((N_KV_HEADS, GROUP * tq, HEAD_DIM), BF16),
                        pltpu.VMEM((N_KV_HEADS, GROUP * tq, 1), F32),
                        pltpu.VMEM((N_KV_HEADS, GROUP * tq, 1), F32),
                        pltpu.VMEM((N_KV_HEADS, GROUP * tq, HEAD_DIM), F32),
                        pltpu.VMEM((tq, 1), F32), pltpu.VMEM((tq, 1), F32),
                        pltpu.VMEM((tq, 1), F32), pltpu.VMEM((tq, 1), F32)],
        compiler_params=_cparams(("arbitrary", "arbitrary")),
        name="dsa",
    )(*args)


def _hgrn_kernel(*refs, c, nchunk, has_state):
    if has_state:
        q_ref, lf_ref, kk_ref, v_ref, hg_ref, nw_ref, s0_ref, o_ref, sout_ref, st_ref = refs
    else:
        q_ref, lf_ref, kk_ref, v_ref, hg_ref, nw_ref, o_ref, sout_ref, st_ref = refs
        s0_ref = None
    r = pl.program_id(2)

    @pl.when(r == 0)
    def _():
        if has_state:
            st_ref[...] = s0_ref[0, 0].T
        else:
            st_ref[...] = jnp.zeros(st_ref.shape, F32)

    ri = lax.broadcasted_iota(jnp.int32, (c, c), 0)
    ci = lax.broadcasted_iota(jnp.int32, (c, c), 1)
    tri = jnp.where(ri >= ci, 1.0, 0.0).astype(BF16)
    rowi = lax.broadcasted_iota(jnp.int32, (SUB_BLOCK, 1), 0)
    nsub = c // SUB_BLOCK

    def chunk(n, carry):
        base = pl.multiple_of(n * c, c)
        q = q_ref[pl.ds(base, c), :]
        lf = lf_ref[pl.ds(base, c), :]
        kk = kk_ref[pl.ds(base, c), :]
        v = v_ref[pl.ds(base, c), :]
        l1 = lf.astype(BF16)
        r1 = lf - l1.astype(F32)
        l2 = r1.astype(BF16)
        l3 = (r1 - l2.astype(F32)).astype(BF16)
        G = _dot(tri, l1) + _dot(tri, l2) + _dot(tri, l3)
        st = st_ref[...]
        vb = v.astype(BF16)
        o_inter = _dot_nt((q * jnp.exp(G)).astype(BF16), st.astype(BF16))
        parts = []
        for i in range(nsub):
            lo_, hi_ = i * SUB_BLOCK, (i + 1) * SUB_BLOCK
            qi_ = q[lo_:hi_]
            Gi = G[lo_:hi_]
            oi = o_inter[lo_:hi_]
            if i > 0:
                Gb = G[lo_ - 1:lo_]
                qt = qi_ * jnp.exp(Gi - Gb)
                kt = kk[:lo_] * jnp.exp(Gb - G[:lo_])
                A = _dot_nt(qt.astype(BF16), kt.astype(BF16))
                oi = oi + _dot(A.astype(BF16), vb[:lo_])
            for sg in range(SUB_BLOCK):
                s = lo_ + sg
                D = qi_ * kk[s:s + 1] * jnp.exp(jnp.minimum(Gi - G[s:s + 1], 0.0))
                a = jnp.sum(D, axis=1, keepdims=True)
                a = jnp.where(rowi >= sg, a, 0.0)
                oi = oi + a * v[s:s + 1]
            parts.append(oi)
        o = jnp.concatenate(parts, axis=0) if nsub > 1 else parts[0]
        Gl = G[c - 1:c]
        kdec = kk * jnp.exp(Gl - G)
        st_ref[...] = st * jnp.exp(Gl) + _dot_tn(vb, kdec.astype(BF16))
        on = o * lax.rsqrt(jnp.mean(o * o, axis=1, keepdims=True) + EPS) * nw_ref[...]
        hg = hg_ref[pl.ds(base, c), :]
        o_ref[pl.ds(base, c), :] = on * (hg * _sigmoid(hg))
        return carry

    lax.fori_loop(0, nchunk, chunk, 0)

    @pl.when(r == pl.num_programs(2) - 1)
    def _():
        sout_ref[0, 0] = st_ref[...].T


def _hgrn(hq, lf, kk, hv, hg, nw, s0, b, t, rb):
    c = min(CHUNK, t)
    nchunk = rb // c
    nr = t // rb
    has_state = s0 is not None
    blk = pl.BlockSpec((rb, HG_DK), lambda bb, h, r: (bb * nr + r, h))
    in_specs = [blk, blk, blk, blk, blk, pl.BlockSpec((1, HG_DV), lambda bb, h, r: (0, 0))]
    args = [hq, lf, kk, hv, hg, nw]
    if has_state:
        in_specs.append(pl.BlockSpec((1, 1, HG_DK, HG_DV), lambda bb, h, r: (bb, h, 0, 0)))
        args.append(s0)
    kern = functools.partial(_hgrn_kernel, c=c, nchunk=nchunk, has_state=has_state)
    return pl.pallas_call(
        kern,
        out_shape=(jax.ShapeDtypeStruct((b * t, HG_HEADS * HG_DV), F32),
                   jax.ShapeDtypeStruct((b, HG_HEADS, HG_DK, HG_DV), F32)),
        grid=(b, HG_HEADS, nr),
        in_specs=in_specs,
        out_specs=(blk, pl.BlockSpec((1, 1, HG_DK, HG_DV), lambda bb, h, r: (bb, h, 0, 0))),
        scratch_shapes=[pltpu.VMEM((HG_DV, HG_DK), F32)],
        compiler_params=_cparams(("arbitrary", "arbitrary", "arbitrary")),
        name="hgrn",
    )(*args)


def _merge_out_kernel(x_ref, oa_ref, oh_ref, ga_ref, gb_ref, w_ref, g1_ref, sc_ref, sh_ref, nw_ref,
                      x1_ref, h2_ref):
    bb, tb, d = x_ref.shape
    merged = _sigmoid(ga_ref[...]) * oa_ref[...] + _sigmoid(gb_ref[...]) * oh_ref[...]
    y = _dot(merged.astype(BF16), w_ref[...]).reshape(bb, tb, d)
    x1 = x_ref[...] + g1_ref[...] * y
    x1_ref[...] = x1
    ms = jnp.mean(x1 * x1, axis=-1, keepdims=True)
    xn = x1 * lax.rsqrt(ms + EPS) * nw_ref[...]
    h2 = xn * (1.0 + sc_ref[...]) + sh_ref[...]
    h2_ref[...] = h2.reshape(bb * tb, d).astype(BF16)


def _merge_out(x, oa, oh, ga, gb, w_out, g1, sc2, sh2, nw2, bb, tb):
    b, t, d = x.shape
    tm = bb * tb
    nt = t // tb

    def row(i, j):
        return (i * nt + j, 0)

    def mod(i, j):
        return (i, 0, 0)

    x3 = pl.BlockSpec((bb, tb, d), lambda i, j: (i, j, 0))
    r2 = pl.BlockSpec((tm, d), row)
    return pl.pallas_call(
        _merge_out_kernel,
        out_shape=(jax.ShapeDtypeStruct((b, t, d), F32), jax.ShapeDtypeStruct((b * t, d), BF16)),
        grid=(b // bb, nt),
        in_specs=[x3, r2, r2, r2, r2,
                  pl.BlockSpec((d, d), lambda i, j: (0, 0)),
                  pl.BlockSpec((bb, 1, d), mod), pl.BlockSpec((bb, 1, d), mod), pl.BlockSpec((bb, 1, d), mod),
                  pl.BlockSpec((1, 1, d), lambda i, j: (0, 0, 0))],
        out_specs=(x3, r2),
        compiler_params=_cparams(("arbitrary", "arbitrary")),
        name="merge_out",
    )(x, oa, oh, ga, gb, w_out, g1, sc2, sh2, nw2)


def _mlp_kernel(h_ref, wu_ref, wd_ref, x1_ref, g2_ref, fw_ref, y_ref, acc_ref):
    f = pl.program_id(2)

    @pl.when(f == 0)
    def _():
        acc_ref[...] = jnp.zeros(acc_ref.shape, F32)

    u = jnp.maximum(_dot(h_ref[...], wu_ref[...]), 0.0)
    acc_ref[...] += _dot((u * u).astype(BF16), wd_ref[...])

    @pl.when(f == pl.num_programs(2) - 1)
    def _():
        bb, tb, d = x1_ref.shape
        x2 = x1_ref[...] + g2_ref[...] * acc_ref[...].reshape(bb, tb, d)
        ms = jnp.mean(x2 * x2, axis=-1, keepdims=True)
        y_ref[...] = x2 * lax.rsqrt(ms + EPS) * fw_ref[...]


def _mlp(h2, w_up, w_down, x1, g2, fw, bb, tb, tf):
    b, t, d = x1.shape
    dff = w_up.shape[1]
    tm = bb * tb
    nt = t // tb
    x3 = pl.BlockSpec((bb, tb, d), lambda i, j, f: (i, j, 0))
    return pl.pallas_call(
        _mlp_kernel,
        out_shape=jax.ShapeDtypeStruct((b, t, d), F32),
        grid=(b // bb, nt, dff // tf),
        in_specs=[pl.BlockSpec((tm, d), lambda i, j, f: (i * nt + j, 0)),
                  pl.BlockSpec((d, tf), lambda i, j, f: (0, f)),
                  pl.BlockSpec((tf, d), lambda i, j, f: (f, 0)),
                  x3,
                  pl.BlockSpec((bb, 1, d), lambda i, j, f: (i, 0, 0)),
                  pl.BlockSpec((1, 1, d), lambda i, j, f: (0, 0, 0))],
        out_specs=x3,
        scratch_shapes=[pltpu.VMEM((tm, d), F32)],
        compiler_params=_cparams(("arbitrary", "arbitrary", "arbitrary")),
        name="mlp",
    )(h2, w_up, w_down, x1, g2, fw)


def _rope_tables(pos):
    half = ROT_DIM // 2
    inv_freq = ROPE_THETA ** (-(jnp.arange(half, dtype=F32) * (2.0 / ROT_DIM)))
    ang = pos.astype(F32)[:, None] * inv_freq[None, :]
    cos, sin = jnp.cos(ang), jnp.sin(ang)
    n = pos.shape[0]
    ones = jnp.ones((n, LANES - ROT_DIM), F32)
    zeros = jnp.zeros((n, LANES - ROT_DIM), F32)
    zh = jnp.zeros((n, half), F32)
    c_t = jnp.concatenate([cos, cos, ones], axis=1)
    s_up = jnp.concatenate([-sin, zh, zeros], axis=1)
    s_dn = jnp.concatenate([zh, sin, zeros], axis=1)
    return c_t, s_up, s_dn


def _trunk(x, mod, pos0, past, wts, blocks):
    (norm1_w, w_parts, lb_logits, hg_norm_w, w_out, norm2_w, w_up, w_down, final_w) = wts
    b, t, d = x.shape
    bb, tb, tm_proj, tq, tk, rb, mlp_bb, mlp_tb = blocks
    m = [mod[:, i:i + 1, :] for i in range(6)]
    sh1, sc1, g1, sh2, sc2, g2 = m
    h = _normmod(x, sc1, sh1, norm1_w.reshape(1, 1, d), bb, tb).reshape(b * t, d)

    pos = pos0 + jnp.arange(t, dtype=jnp.int32)
    tabs = tuple(jnp.tile(tb_, (b, 1)) for tb_ in _rope_tables(pos))
    wq, wk, wv, wqi, wki, wwi, whq, whf, whi, whg, wga, wgb = w_parts
    tm = tm_proj
    (q_bf,) = _proj("rope", h, wq, tm, 512, tabs, (BF16,), scale=HEAD_DIM ** -0.5)
    k_f, k_bf = _proj("rope", h, wk, tm, 512, tabs, (F32, BF16))
    v_f, v_bf = _proj("plain", h, wv, tm, 512, (), (F32, BF16))
    (qi_bf,) = _proj("rope", h, wqi, tm, 512, tabs, (BF16,))
    ki_f, ki_bf = _proj("rope", h, wki, tm, LANES, tabs, (F32, BF16))
    (wi_f,) = _proj("plain", h, wwi, tm, LANES, (), (F32,), scale=IDX_HEADS ** -0.5 * IDX_DIM ** -0.5)
    (hq,) = _proj("plain", h, whq, tm, 512)
    lf, kk = _proj("forget", h, whf, tm, 512, (lb_logits,), (F32, F32))
    (hi,) = _proj("plain", h, whi, tm, 512)
    (hg,) = _proj("plain", h, whg, tm, 512)
    (ga,) = _proj("plain", h, wga, tm, 512)
    (gb,) = _proj("plain", h, wgb, tm, 512)

    kvd = N_KV_HEADS * HEAD_DIM
    r3 = lambda a: a.reshape(b, t, a.shape[-1])
    if past is None:
        cache, s0 = None, None
    else:
        ck, cv, cki, s0 = past
        lc = ck.shape[1]
        cache = (cki, ck.reshape(b, lc, kvd), cv.reshape(b, lc, kvd))
    o_attn = _dsa(r3(qi_bf), r3(wi_f), r3(q_bf), r3(ki_bf), r3(k_bf), r3(v_bf), cache, pos0, tq, tk)
    o_hg, s_new = _hgrn(hq, lf, kk, hi, hg, hg_norm_w.reshape(1, HG_DV), s0, b, t, rb)

    x1, h2 = _merge_out(x, o_attn.reshape(b * t, d), o_hg, ga, gb, w_out, g1, sc2, sh2,
                        norm2_w.reshape(1, 1, d), bb, tb)
    y = _mlp(h2, w_up, w_down, x1, g2, final_w.reshape(1, 1, d), mlp_bb, mlp_tb, 512)
    return (y, k_f.reshape(b, t, N_KV_HEADS, HEAD_DIM), v_f.reshape(b, t, N_KV_HEADS, HEAD_DIM),
            ki_f.reshape(b, t, IDX_DIM), s_new)


def kernel(x_prompt, x_sample, cache_k, cache_v, cache_ki, state_hgrn, c_prompt, c_sample, w_ada, b_ada, norm1_w,
           w_in, hg_lb_logits, hg_norm_w, w_out, norm2_w, w_up, w_down, final_norm_w):
    depth = w_in.shape[0]
    assert depth == 1
    d = x_prompt.shape[-1]
    bp, tp, _ = x_prompt.shape
    bs, ts, _ = x_sample.shape
    past_len = cache_k.shape[2]

    c_all = jnp.concatenate([c_prompt, c_sample], axis=0)
    nrow = c_all.shape[0]
    pad = (-nrow) % 8
    c_all = jnp.pad(c_all, ((0, pad), (0, 0)))
    mod = _ada(c_all, w_ada[0], b_ada[0].reshape(1, -1)).reshape(nrow + pad, 6, d)

    sizes = (N_HEADS * HEAD_DIM, N_KV_HEADS * HEAD_DIM, N_KV_HEADS * HEAD_DIM, IDX_HEADS * IDX_DIM, IDX_DIM,
             IDX_HEADS, HG_HEADS * HG_DK, HG_HEADS * HG_DK, HG_HEADS * HG_DV, HG_HEADS * HG_DV, d, d)
    offs = np.concatenate([[0], np.cumsum(sizes)])
    w0 = w_in[0]
    w_parts = []
    for i, sz in enumerate(sizes):
        wp = w0[:, int(offs[i]):int(offs[i + 1])].astype(BF16)
        if sz % LANES:
            wp = jnp.pad(wp, ((0, 0), (0, LANES - sz % LANES)))
        w_parts.append(wp)
    n_lb = hg_lb_logits.shape[1]
    wts = (norm1_w[0], tuple(w_parts), hg_lb_logits.astype(F32), hg_norm_w[0], w_out[0].astype(BF16), norm2_w[0],
           w_up[0].astype(BF16), w_down[0].astype(BF16), final_norm_w)
    del n_lb

    yp, kp, vp, kip, sp = _trunk(x_prompt, mod[:bp], 0, None, wts, (1, 256, 1024, 128, 512, 512, 1, 512))
    past = (cache_k[0], cache_v[0], cache_ki[0], state_hgrn[0])
    ys, ks, vs, kis, ss = _trunk(x_sample, mod[bp:bp + bs], past_len, past, wts,
                                 (bs // 2, ts, bs * ts, ts, 512, ts, bs, ts))
    return (yp, ys, kp[None], vp[None], kip[None], sp[None], ks[None], vs[None], kis[None], ss[None])
```

```python
import functools
import math

import jax
import jax.numpy as jnp
import numpy as np
from jax import lax
from jax.experimental import pallas as pl
from jax.experimental.pallas import tpu as pltpu

CHUNK = 64
N_HEADS = 16
HEAD_DIM = 128
N_KV_HEADS = 4
GROUP = N_HEADS // N_KV_HEADS
ROT_DIM = HEAD_DIM // 4
ROPE_THETA = 500000.0
IDX_HEADS = 16
IDX_DIM = 128
TOPK_MAX = 256
HG_HEADS = 16
HG_DK = 128
HG_DV = 128
EPS = 1e-6
LANES = 128
SUBLANES = 8
BF16_ROWS = 16
SUB_BLOCK = 16
CNT_ROWS = 64
VT_ROWS = HEAD_DIM + BF16_ROWS
VMEM_LIMIT = 56 * 1024 * 1024
NEG_INF = float("-inf")
POS_INF = float("inf")
LOG2E = math.log2(math.e)

F32 = jnp.float32
BF16 = jnp.bfloat16


def _cparams(sem):
    return pltpu.CompilerParams(dimension_semantics=sem, vmem_limit_bytes=VMEM_LIMIT)


def _dot_nt(a, b):
    return lax.dot_general(a, b, (((1,), (1,)), ((), ())), preferred_element_type=F32)


def _dot_tn(a, b):
    return lax.dot_general(a, b, (((0,), (0,)), ((), ())), preferred_element_type=F32)


def _dot(a, b):
    return jnp.dot(a, b, preferred_element_type=F32)


def _sigmoid(x):
    return 1.0 / (1.0 + jnp.exp(-x))


def _col_reduce(op, x):
    rows, n = x.shape
    if rows > CNT_ROWS and rows % CNT_ROWS == 0:
        x = op(x.reshape(rows // CNT_ROWS, CNT_ROWS, n), axis=0)
    return op(x, axis=0, keepdims=True)


def _ada_kernel(c_ref, w_ref, b_ref, o_ref):
    o_ref[...] = _dot(c_ref[...].astype(BF16), w_ref[...].astype(BF16)) + b_ref[...]


def _ada(c_all, w_ada, b_ada, tn=1024):
    r, d = c_all.shape
    n = w_ada.shape[1]
    return pl.pallas_call(
        _ada_kernel,
        out_shape=jax.ShapeDtypeStruct((r, n), F32),
        grid=(n // tn,),
        in_specs=[pl.BlockSpec((r, d), lambda j: (0, 0)),
                  pl.BlockSpec((d, tn), lambda j: (0, j)),
                  pl.BlockSpec((1, tn), lambda j: (0, j))],
        out_specs=pl.BlockSpec((r, tn), lambda j: (0, j)),
        compiler_params=_cparams(("arbitrary",)),
        name="ada",
    )(c_all, w_ada, b_ada)


def _normmod_kernel(x_ref, sc_ref, sh_ref, g_ref, o_ref):
    x = x_ref[...]
    ms = jnp.mean(x * x, axis=-1, keepdims=True)
    xn = x * lax.rsqrt(ms + EPS) * g_ref[...]
    o_ref[...] = (xn * (1.0 + sc_ref[...]) + sh_ref[...]).astype(o_ref.dtype)


def _normmod(x, sc, sh, g, bb, tb):
    b, t, d = x.shape
    return pl.pallas_call(
        _normmod_kernel,
        out_shape=jax.ShapeDtypeStruct((b, t, d), BF16),
        grid=(b // bb, t // tb),
        in_specs=[pl.BlockSpec((bb, tb, d), lambda i, j: (i, j, 0)),
                  pl.BlockSpec((bb, 1, d), lambda i, j: (i, 0, 0)),
                  pl.BlockSpec((bb, 1, d), lambda i, j: (i, 0, 0)),
                  pl.BlockSpec((1, 1, d), lambda i, j: (0, 0, 0))],
        out_specs=pl.BlockSpec((bb, tb, d), lambda i, j: (i, j, 0)),
        compiler_params=_cparams(("arbitrary", "arbitrary")),
        name="normmod",
    )(x, sc, sh, g)


def _proj_plain_kernel(h_ref, w_ref, *o_refs, scale):
    z = _dot(h_ref[...], w_ref[...])
    if scale != 1.0:
        z = z * scale
    for o_ref in o_refs:
        o_ref[...] = z.astype(o_ref.dtype)


def _proj_rope_kernel(h_ref, w_ref, cos_ref, sup_ref, sdn_ref, *o_refs, scale):
    z = _dot(h_ref[...], w_ref[...])
    tn = z.shape[1]
    reps = tn // LANES

    def wide(ref):
        t = ref[...]
        return t if reps == 1 else jnp.concatenate([t] * reps, axis=1)

    up = pltpu.roll(z, tn - ROT_DIM // 2, 1)
    dn = pltpu.roll(z, ROT_DIM // 2, 1)
    r = z * wide(cos_ref) + up * wide(sup_ref) + dn * wide(sdn_ref)
    for o_ref in o_refs:
        if o_ref.dtype == BF16 and scale != 1.0:
            o_ref[...] = (r * scale).astype(BF16)
        else:
            o_ref[...] = r.astype(o_ref.dtype)


def _proj_forget_kernel(h_ref, w_ref, lbl_ref, lf_ref, kk_ref):
    z = _dot(h_ref[...], w_ref[...])
    lbl = lbl_ref[...]
    mx = jnp.max(lbl, axis=0, keepdims=True)
    e = jnp.exp(lbl - mx)
    lb = e[0:1, :] / jnp.sum(e, axis=0, keepdims=True)
    f = lb + (1.0 - lb) * _sigmoid(z)
    lf_ref[...] = jnp.log(f)
    kk_ref[...] = 1.0 - f


def _proj(kind, h, w, tm, tn, extra=(), out_dtypes=(F32,), scale=1.0):
    m, k = h.shape
    n = w.shape[1]
    in_specs = [pl.BlockSpec((tm, k), lambda i, j: (i, 0)),
                pl.BlockSpec((k, tn), lambda i, j: (0, j))]
    if kind == "rope":
        kern = functools.partial(_proj_rope_kernel, scale=scale)
        in_specs += [pl.BlockSpec((tm, LANES), lambda i, j: (i, 0))] * 3
    elif kind == "forget":
        kern = _proj_forget_kernel
        in_specs += [pl.BlockSpec((extra[0].shape[0], tn), lambda i, j: (0, j))]
    else:
        kern = functools.partial(_proj_plain_kernel, scale=scale)
    outs = tuple(jax.ShapeDtypeStruct((m, n), dt) for dt in out_dtypes)
    out_specs = tuple(pl.BlockSpec((tm, tn), lambda i, j: (i, j)) for _ in out_dtypes)
    res = pl.pallas_call(
        kern,
        out_shape=outs,
        grid=(m // tm, n // tn),
        in_specs=in_specs,
        out_specs=out_specs,
        compiler_params=_cparams(("arbitrary", "arbitrary")),
        name="proj_" + kind,
    )(h, w, *extra)
    return res


def _proj_v_kernel(h_ref, w_ref, vf_ref, vt_ref, *, wn):
    z = _dot(h_ref[...], w_ref[...])
    vf_ref[...] = z
    zt = z.T
    ones = jnp.ones((BF16_ROWS, wn), BF16)
    for u in range(z.shape[0] // wn):
        blk = zt[:, u * wn:(u + 1) * wn].astype(BF16)
        for g in range(N_KV_HEADS):
            vt_ref[u, g * VT_ROWS:g * VT_ROWS + HEAD_DIM, :] = blk[g * HEAD_DIM:(g + 1) * HEAD_DIM]
            vt_ref[u, g * VT_ROWS + HEAD_DIM:(g + 1) * VT_ROWS, :] = ones


def _proj_v(h, w, tm, wn):
    m, k = h.shape
    n = w.shape[1]
    return pl.pallas_call(
        functools.partial(_proj_v_kernel, wn=wn),
        out_shape=(jax.ShapeDtypeStruct((m, n), F32),
                   jax.ShapeDtypeStruct((m // wn, N_KV_HEADS * VT_ROWS, wn), BF16)),
        grid=(m // tm,),
        in_specs=[pl.BlockSpec((tm, k), lambda i: (i, 0)),
                  pl.BlockSpec((k, n), lambda i: (0, 0))],
        out_specs=(pl.BlockSpec((tm, n), lambda i: (i, 0)),
                   pl.BlockSpec((tm // wn, N_KV_HEADS * VT_ROWS, wn), lambda i: (i, 0, 0))),
        compiler_params=_cparams(("arbitrary",)),
        name="proj_v",
    )(h, w)


def _dsa_kernel(*refs, tq, tk, t_new, l_cache, pos0, topk, has_cache):
    if has_cache:
        (qi_ref, wi_ref, q_ref, kin_ref, kn_ref, vtn_ref, cki_ref, ck_ref, cv_ref, o_ref,
         sc_ref, qit_ref, wt_ref, qt_ref, acc_ref) = refs
    else:
        (qi_ref, wi_ref, q_ref, kin_ref, kn_ref, vtn_ref, o_ref,
         sc_ref, qit_ref, wt_ref, qt_ref, acc_ref) = refs
        cki_ref = ck_ref = cv_ref = None
    i = pl.program_id(1)
    ntc = l_cache // tk
    wn = min(tk, t_new)
    nq = GROUP * tq
    q0 = pos0 + i * tq
    lane_q = lax.broadcasted_iota(jnp.int32, (1, tq), 1)
    qend = (((q0 + lane_q) >> 6) + 1) << 6
    last_end = (((q0 + tq - 1) >> 6) + 1) << 6
    nvis_new = jnp.minimum(last_end - pos0, t_new)
    ntn = (nvis_new + wn - 1) // wn
    ntiles = ntc + ntn

    def tpose(x):
        xf = x.astype(F32)
        if tq < LANES:
            xf = jnp.concatenate([xf, jnp.zeros((LANES - tq, LANES), F32)], axis=0)
        xt = xf.T
        return xt[:, :tq] if tq < LANES else xt

    qi_blk = qi_ref[0]
    for h in range(IDX_HEADS):
        qit_ref[:, h * tq:(h + 1) * tq] = tpose(qi_blk[:, h * IDX_DIM:(h + 1) * IDX_DIM]).astype(BF16)
    wt_ref[...] = tpose(wi_ref[0])
    q_blk = q_ref[0]
    for g in range(N_KV_HEADS):
        for a in range(GROUP):
            hh = GROUP * g + a
            qt_ref[g, :, a * tq:(a + 1) * tq] = tpose(q_blk[:, hh * HEAD_DIM:(hh + 1) * HEAD_DIM]).astype(BF16)

    def score_tile(ki_tile, kpos0, w):
        acc = jnp.zeros((w, tq), F32)
        for hp in range(IDX_HEADS // 2):
            lg = _dot(ki_tile, qit_ref[:, 2 * hp * tq:(2 * hp + 2) * tq])
            for e in range(2):
                h = 2 * hp + e
                acc = acc + wt_ref[h:h + 1, :] * jnp.maximum(lg[:, e * tq:(e + 1) * tq], 0.0)
        kpos = kpos0 + lax.broadcasted_iota(jnp.int32, (w, 1), 0)
        vis = kpos < qend
        s = jnp.where(vis, acc, NEG_INF)
        smin = _col_reduce(jnp.min, jnp.where(vis, acc, POS_INF))
        smax = _col_reduce(jnp.max, s)
        return s, smax, smin

    def p1_cache(t, carry):
        mx, mn = carry
        r0 = pl.multiple_of(t * tk, tk)
        s, smax, smin = score_tile(cki_ref[0, pl.ds(r0, tk), :].astype(BF16), r0, tk)
        sc_ref[t] = s
        return jnp.maximum(mx, smax), jnp.minimum(mn, smin)

    def p1_new(j, carry):
        mx, mn = carry
        r0 = pl.multiple_of(j * wn, wn)
        s, smax, smin = score_tile(kin_ref[0, pl.ds(r0, wn), :], pos0 + r0, wn)
        if wn == tk:
            sc_ref[ntc + j] = s
        else:
            sc_ref[ntc + j] = jnp.full((tk, tq), NEG_INF, F32)
            sc_ref[ntc + j, 0:wn, :] = s
        return jnp.maximum(mx, smax), jnp.minimum(mn, smin)

    carry = (jnp.full((1, tq), NEG_INF, F32), jnp.full((1, tq), POS_INF, F32))
    if has_cache:
        carry = lax.fori_loop(0, ntc, p1_cache, carry)
    mx, mn = lax.fori_loop(0, ntn, p1_new, carry)

    nvis = jnp.clip(qend - pos0, 0, t_new)
    if has_cache:
        nvis = nvis + jnp.minimum(qend, l_cache)
    act0 = jnp.where(nvis > topk, 1.0, 0.0)
    kf = float(topk)

    def bis_cond(carry):
        return jnp.logical_and(carry[0] < 200, carry[1] > 0)

    def bis_body(carry):
        it, _, lo, hi, tau, act = carry
        mid = lo * 0.5 + hi * 0.5

        def cnt_body(t, cnt):
            ind = jnp.where(sc_ref[t] >= mid, 1.0, 0.0)
            return cnt + jnp.sum(ind.reshape(tk // CNT_ROWS, CNT_ROWS, tq), axis=0)

        cnt = lax.fori_loop(0, ntiles, cnt_body, jnp.zeros((CNT_ROWS, tq), F32))
        c = jnp.sum(cnt, axis=0, keepdims=True)
        found = jnp.where(c == kf, act, 0.0)
        stuck = jnp.where(jnp.logical_or(mid <= lo, mid >= hi), act, 0.0) * (1.0 - found)
        above = jnp.where(c > kf, act, 0.0)
        below = act * (1.0 - above) * (1.0 - found)
        tau = jnp.where(found > 0, mid, tau)
        tau = jnp.where(stuck > 0, lo, tau)
        lo = jnp.where(above > 0, mid, lo)
        hi = jnp.where(below > 0, mid, hi)
        nact = act * (1.0 - found) * (1.0 - stuck)
        return it + 1, (jnp.max(nact) > 0).astype(jnp.int32), lo, hi, tau, nact

    init = (jnp.int32(0), (jnp.max(act0) > 0).astype(jnp.int32), mn, mx, mn, act0)
    tau = lax.while_loop(bis_cond, bis_body, init)[4]

    acc_ref[...] = jnp.zeros(acc_ref.shape, F32)

    def attend(t, k_of_g, vt_of_g, w, ms):
        s_idx = sc_ref[t]
        if w < tk:
            s_idx = s_idx[0:w]
        bias = jnp.where(s_idx >= tau, 0.0, NEG_INF)
        bias4 = jnp.concatenate([bias] * GROUP, axis=1)
        out = []
        for g in range(N_KV_HEADS):
            s = _dot(k_of_g(g), qt_ref[g]) + bias4
            m_old = ms[g]
            m_new = jnp.maximum(m_old, _col_reduce(jnp.max, s))
            m_safe = jnp.where(m_new == NEG_INF, 0.0, m_new)
            alpha = jnp.exp2(m_old - m_safe)
            p = jnp.exp2(s - m_safe).astype(BF16)
            acc_ref[g] = acc_ref[g] * alpha + _dot(vt_of_g(g), p)
            out.append(m_new)
        return tuple(out)

    def p3_cache(t, ms):
        r0 = pl.multiple_of(t * tk, tk)

        def vt(g):
            vg = cv_ref[0, pl.ds(r0, tk), g * HEAD_DIM:(g + 1) * HEAD_DIM]
            return jnp.concatenate([vg.T, jnp.ones((BF16_ROWS, tk), F32)], axis=0).astype(BF16)

        return attend(t, lambda g: ck_ref[0, pl.ds(r0, tk), g * HEAD_DIM:(g + 1) * HEAD_DIM].astype(BF16), vt, tk, ms)

    def p3_new(j, ms):
        r0 = pl.multiple_of(j * wn, wn)
        return attend(ntc + j,
                      lambda g: kn_ref[0, pl.ds(r0, wn), g * HEAD_DIM:(g + 1) * HEAD_DIM],
                      lambda g: vtn_ref[0, j, g * VT_ROWS:(g + 1) * VT_ROWS, :], wn, ms)

    ms = tuple(jnp.full((1, nq), NEG_INF, F32) for _ in range(N_KV_HEADS))
    if has_cache:
        ms = lax.fori_loop(0, ntc, p3_cache, ms)
    lax.fori_loop(0, ntn, p3_new, ms)

    for g in range(N_KV_HEADS):
        acc = acc_ref[g]
        og = (acc[0:HEAD_DIM] / acc[HEAD_DIM:HEAD_DIM + 1]).T
        for a in range(GROUP):
            hh = GROUP * g + a
            o_ref[0, :, hh * HEAD_DIM:(hh + 1) * HEAD_DIM] = og[a * tq:(a + 1) * tq, :]


def _dsa(qi, wi, q, kin, kn, vtn, cache, pos0, tq, tk):
    b, t, _ = q.shape
    has_cache = cache is not None
    l_cache = cache[0].shape[1] if has_cache else 0
    topk = min(TOPK_MAX, (l_cache + t) // 4)
    wn = min(tk, t)
    ntn = t // wn
    nt = l_cache // tk + ntn
    kvd = N_KV_HEADS * HEAD_DIM
    nq = GROUP * tq
    res = pl.Buffered(1) if b == 1 else None
    in_specs = [pl.BlockSpec((1, tq, IDX_HEADS * IDX_DIM), lambda bb, i: (bb, i, 0)),
                pl.BlockSpec((1, tq, LANES), lambda bb, i: (bb, i, 0)),
                pl.BlockSpec((1, tq, N_HEADS * HEAD_DIM), lambda bb, i: (bb, i, 0)),
                pl.BlockSpec((1, t, IDX_DIM), lambda bb, i: (bb, 0, 0), pipeline_mode=res),
                pl.BlockSpec((1, t, kvd), lambda bb, i: (bb, 0, 0), pipeline_mode=res),
                pl.BlockSpec((1, ntn, N_KV_HEADS * VT_ROWS, wn), lambda bb, i: (bb, 0, 0, 0), pipeline_mode=res)]
    args = [qi, wi, q, kin, kn, vtn]
    if has_cache:
        in_specs += [pl.BlockSpec((1, l_cache, IDX_DIM), lambda bb, i: (bb, 0, 0)),
                     pl.BlockSpec((1, l_cache, kvd), lambda bb, i: (bb, 0, 0)),
                     pl.BlockSpec((1, l_cache, kvd), lambda bb, i: (bb, 0, 0))]
        args += list(cache)
    kern = functools.partial(_dsa_kernel, tq=tq, tk=tk, t_new=t, l_cache=l_cache, pos0=pos0, topk=topk,
                             has_cache=has_cache)
    return pl.pallas_call(
        kern,
        out_shape=jax.ShapeDtypeStruct((b, t, N_HEADS * HEAD_DIM), F32),
        grid=(b, t // tq),
        in_specs=in_specs,
        out_specs=pl.BlockSpec((1, tq, N_HEADS * HEAD_DIM), lambda bb, i: (bb, i, 0)),
        scratch_shapes=[pltpu.VMEM((nt, tk, tq), F32),
                        pltpu.VMEM((IDX_DIM, IDX_HEADS * tq), BF16),
                        pltpu.VMEM((LANES, tq), F32),
                        pltpu.VMEM((N_KV_HEADS, HEAD_DIM, nq), BF16),
                        pltpu.VMEM((N_KV_HEADS, VT_ROWS, nq), F32)],
        compiler_params=_cparams(("arbitrary", "arbitrary")),
        name="dsa",
    )(*args)


def _hgrn_kernel(*refs, c, nchunk, nh, has_state):
    if has_state:
        q_ref, lf_ref, kk_ref, v_ref, hg_ref, nw_ref, s0_ref, o_ref, sout_ref, st_ref = refs
    else:
        q_ref, lf_ref, kk_ref, v_ref, hg_ref, nw_ref, o_ref, sout_ref, st_ref = refs
        s0_ref = None
    r = pl.program_id(2)

    @pl.when(r == 0)
    def _():
        for hh in range(nh):
            if has_state:
                st_ref[hh] = s0_ref[0, hh].T
            else:
                st_ref[hh] = jnp.zeros((HG_DV, HG_DK), F32)

    ri = lax.broadcasted_iota(jnp.int32, (c, c), 0)
    ci = lax.broadcasted_iota(jnp.int32, (c, c), 1)
    tri = jnp.where(ri >= ci, 1.0, 0.0).astype(BF16)
    nsub = c // SUB_BLOCK
    pair = lax.broadcasted_iota(jnp.int32, (SUB_BLOCK * SUB_BLOCK, 1), 0)
    keep = (pair % SUB_BLOCK) >= (pair // SUB_BLOCK)
    ones_w = jnp.ones((HG_DK, LANES), BF16)

    def rep_rows(x, lo_):
        return jnp.concatenate(
            [jnp.broadcast_to(x[lo_ + sg:lo_ + sg + 1], (SUB_BLOCK, x.shape[1])) for sg in range(SUB_BLOCK)], axis=0)

    def one_head(base, hh):
        cs = slice(hh * HG_DK, (hh + 1) * HG_DK)
        q = q_ref[pl.ds(base, c), cs]
        lf = lf_ref[pl.ds(base, c), cs]
        kk = kk_ref[pl.ds(base, c), cs]
        v = v_ref[pl.ds(base, c), cs]
        l1 = lf.astype(BF16)
        r1 = lf - l1.astype(F32)
        l2 = r1.astype(BF16)
        l3 = (r1 - l2.astype(F32)).astype(BF16)
        G = _dot(tri, l1) + _dot(tri, l2) + _dot(tri, l3)
        st = st_ref[hh]
        vb = v.astype(BF16)
        o_inter = _dot_nt((q * jnp.exp(G)).astype(BF16), st.astype(BF16))
        parts = []
        for i in range(nsub):
            lo_, hi_ = i * SUB_BLOCK, (i + 1) * SUB_BLOCK
            qi_ = q[lo_:hi_]
            Gi = G[lo_:hi_]
            oi = o_inter[lo_:hi_]
            if i > 0:
                Gb = G[lo_ - 1:lo_]
                qt = qi_ * jnp.exp(Gi - Gb)
                kt = kk[:lo_] * jnp.exp(Gb - G[:lo_])
                A = _dot_nt(qt.astype(BF16), kt.astype(BF16))
                oi = oi + _dot(A.astype(BF16), vb[:lo_])
            qrep = jnp.concatenate([qi_] * SUB_BLOCK, axis=0)
            grep = jnp.concatenate([Gi] * SUB_BLOCK, axis=0)
            D = qrep * rep_rows(kk, lo_) * jnp.exp(jnp.minimum(grep - rep_rows(G, lo_), 0.0))
            D = jnp.where(keep, D, 0.0)
            rs = _dot(D.astype(BF16), ones_w)
            contrib = (rs * rep_rows(v, lo_)).reshape(SUB_BLOCK, SUB_BLOCK, HG_DV)
            parts.append(oi + jnp.sum(contrib, axis=0))
        o = jnp.concatenate(parts, axis=0) if nsub > 1 else parts[0]
        Gl = G[c - 1:c]
        kdec = kk * jnp.exp(Gl - G)
        st_ref[hh] = st * jnp.exp(Gl) + _dot_tn(vb, kdec.astype(BF16))
        on = o * lax.rsqrt(jnp.mean(o * o, axis=1, keepdims=True) + EPS) * nw_ref[...]
        hg = hg_ref[pl.ds(base, c), cs]
        o_ref[pl.ds(base, c), cs] = on * (hg * _sigmoid(hg))

    def chunk(n, carry):
        base = pl.multiple_of(n * c, c)
        for hh in range(nh):
            one_head(base, hh)
        return carry

    lax.fori_loop(0, nchunk, chunk, 0)

    @pl.when(r == pl.num_programs(2) - 1)
    def _():
        for hh in range(nh):
            sout_ref[0, hh] = st_ref[hh].T


def _hgrn(hq, lf, kk, hv, hg, nw, s0, b, t, rb, nh):
    c = min(CHUNK, t)
    nchunk = rb // c
    nr = t // rb
    has_state = s0 is not None
    blk = pl.BlockSpec((rb, nh * HG_DK), lambda bb, h, r: (bb * nr + r, h))
    sblk = pl.BlockSpec((1, nh, HG_DK, HG_DV), lambda bb, h, r: (bb, h, 0, 0))
    in_specs = [blk, blk, blk, blk, blk, pl.BlockSpec((1, HG_DV), lambda bb, h, r: (0, 0))]
    args = [hq, lf, kk, hv, hg, nw]
    if has_state:
        in_specs.append(sblk)
        args.append(s0)
    kern = functools.partial(_hgrn_kernel, c=c, nchunk=nchunk, nh=nh, has_state=has_state)
    return pl.pallas_call(
        kern,
        out_shape=(jax.ShapeDtypeStruct((b * t, HG_HEADS * HG_DV), F32),
                   jax.ShapeDtypeStruct((b, HG_HEADS, HG_DK, HG_DV), F32)),
        grid=(b, HG_HEADS // nh, nr),
        in_specs=in_specs,
        out_specs=(blk, sblk),
        scratch_shapes=[pltpu.VMEM((nh, HG_DV, HG_DK), F32)],
        compiler_params=_cparams(("arbitrary", "arbitrary", "arbitrary")),
        name="hgrn",
    )(*args)


def _merge_out_kernel(x_ref, oa_ref, oh_ref, ga_ref, gb_ref, w_ref, g1_ref, sc_ref, sh_ref, nw_ref,
                      x1_ref, h2_ref):
    bb, tb, d = x_ref.shape
    merged = _sigmoid(ga_ref[...]) * oa_ref[...] + _sigmoid(gb_ref[...]) * oh_ref[...]
    y = _dot(merged.astype(BF16), w_ref[...]).reshape(bb, tb, d)
    x1 = x_ref[...] + g1_ref[...] * y
    x1_ref[...] = x1
    ms = jnp.mean(x1 * x1, axis=-1, keepdims=True)
    xn = x1 * lax.rsqrt(ms + EPS) * nw_ref[...]
    h2 = xn * (1.0 + sc_ref[...]) + sh_ref[...]
    h2_ref[...] = h2.reshape(bb * tb, d).astype(BF16)


def _merge_out(x, oa, oh, ga, gb, w_out, g1, sc2, sh2, nw2, bb, tb):
    b, t, d = x.shape
    tm = bb * tb
    nt = t // tb

    def row(i, j):
        return (i * nt + j, 0)

    def mod(i, j):
        return (i, 0, 0)

    x3 = pl.BlockSpec((bb, tb, d), lambda i, j: (i, j, 0))
    r2 = pl.BlockSpec((tm, d), row)
    return pl.pallas_call(
        _merge_out_kernel,
        out_shape=(jax.ShapeDtypeStruct((b, t, d), F32), jax.ShapeDtypeStruct((b * t, d), BF16)),
        grid=(b // bb, nt),
        in_specs=[x3, r2, r2, r2, r2,
                  pl.BlockSpec((d, d), lambda i, j: (0, 0)),
                  pl.BlockSpec((bb, 1, d), mod), pl.BlockSpec((bb, 1, d), mod), pl.BlockSpec((bb, 1, d), mod),
                  pl.BlockSpec((1, 1, d), lambda i, j: (0, 0, 0))],
        out_specs=(x3, r2),
        compiler_params=_cparams(("arbitrary", "arbitrary")),
        name="merge_out",
    )(x, oa, oh, ga, gb, w_out, g1, sc2, sh2, nw2)


def _mlp_kernel(h_ref, wu_ref, wd_ref, x1_ref, g2_ref, fw_ref, y_ref, acc_ref):
    f = pl.program_id(2)

    @pl.when(f == 0)
    def _():
        acc_ref[...] = jnp.zeros(acc_ref.shape, F32)

    u = jnp.maximum(_dot(h_ref[...], wu_ref[...]), 0.0)
    acc_ref[...] += _dot((u * u).astype(BF16), wd_ref[...])

    @pl.when(f == pl.num_programs(2) - 1)
    def _():
        bb, tb, d = x1_ref.shape
        x2 = x1_ref[...] + g2_ref[...] * acc_ref[...].reshape(bb, tb, d)
        ms = jnp.mean(x2 * x2, axis=-1, keepdims=True)
        y_ref[...] = x2 * lax.rsqrt(ms + EPS) * fw_ref[...]


def _mlp(h2, w_up, w_down, x1, g2, fw, bb, tb, tf):
    b, t, d = x1.shape
    dff = w_up.shape[1]
    tm = bb * tb
    nt = t // tb
    x3 = pl.BlockSpec((bb, tb, d), lambda i, j, f: (i, j, 0))
    return pl.pallas_call(
        _mlp_kernel,
        out_shape=jax.ShapeDtypeStruct((b, t, d), F32),
        grid=(b // bb, nt, dff // tf),
        in_specs=[pl.BlockSpec((tm, d), lambda i, j, f: (i * nt + j, 0)),
                  pl.BlockSpec((d, tf), lambda i, j, f: (0, f)),
                  pl.BlockSpec((tf, d), lambda i, j, f: (f, 0)),
                  x3,
                  pl.BlockSpec((bb, 1, d), lambda i, j, f: (i, 0, 0)),
                  pl.BlockSpec((1, 1, d), lambda i, j, f: (0, 0, 0))],
        out_specs=x3,
        scratch_shapes=[pltpu.VMEM((tm, d), F32)],
        compiler_params=_cparams(("arbitrary", "arbitrary", "arbitrary")),
        name="mlp",
    )(h2, w_up, w_down, x1, g2, fw)


def _rope_tables(pos):
    half = ROT_DIM // 2
    inv_freq = ROPE_THETA ** (-(jnp.arange(half, dtype=F32) * (2.0 / ROT_DIM)))
    ang = pos.astype(F32)[:, None] * inv_freq[None, :]
    cos, sin = jnp.cos(ang), jnp.sin(ang)
    n = pos.shape[0]
    ones = jnp.ones((n, LANES - ROT_DIM), F32)
    zeros = jnp.zeros((n, LANES - ROT_DIM), F32)
    zh = jnp.zeros((n, half), F32)
    c_t = jnp.concatenate([cos, cos, ones], axis=1)
    s_up = jnp.concatenate([-sin, zh, zeros], axis=1)
    s_dn = jnp.concatenate([zh, sin, zeros], axis=1)
    return c_t, s_up, s_dn


def _trunk(x, mod, pos0, past, wts, blocks):
    (norm1_w, w_parts, lb_logits, hg_norm_w, w_out, norm2_w, w_up, w_down, final_w) = wts
    b, t, d = x.shape
    bb, tb, tm_proj, tq, tk, rb, mlp_bb, mlp_tb = blocks
    m = [mod[:, i:i + 1, :] for i in range(6)]
    sh1, sc1, g1, sh2, sc2, g2 = m
    h = _normmod(x, sc1, sh1, norm1_w.reshape(1, 1, d), bb, tb).reshape(b * t, d)

    pos = pos0 + jnp.arange(t, dtype=jnp.int32)
    tabs = tuple(jnp.tile(tb_, (b, 1)) for tb_ in _rope_tables(pos))
    wq, wk, wv, wqi, wki, wwi, whq, whf, whi, whg, wga, wgb = w_parts
    tm = tm_proj
    wn = min(tk, t)
    (q_bf,) = _proj("rope", h, wq, tm, 512, tabs, (BF16,), scale=HEAD_DIM ** -0.5 * LOG2E)
    k_f, k_bf = _proj("rope", h, wk, tm, 512, tabs, (F32, BF16))
    v_f, vt_bf = _proj_v(h, wv, tm, wn)
    (qi_bf,) = _proj("rope", h, wqi, tm, 512, tabs, (BF16,))
    ki_f, ki_bf = _proj("rope", h, wki, tm, LANES, tabs, (F32, BF16))
    (wi_f,) = _proj("plain", h, wwi, tm, LANES, (), (F32,), scale=IDX_HEADS ** -0.5 * IDX_DIM ** -0.5)
    (hq,) = _proj("plain", h, whq, tm, 512)
    lf, kk = _proj("forget", h, whf, tm, 512, (lb_logits,), (F32, F32))
    (hi,) = _proj("plain", h, whi, tm, 512)
    (hg,) = _proj("plain", h, whg, tm, 512)
    (ga,) = _proj("plain", h, wga, tm, 512)
    (gb,) = _proj("plain", h, wgb, tm, 512)

    kvd = N_KV_HEADS * HEAD_DIM
    r3 = lambda a: a.reshape(b, t, a.shape[-1])
    if past is None:
        cache, s0 = None, None
    else:
        ck, cv, cki, s0 = past
        lc = ck.shape[1]
        cache = (cki, ck.reshape(b, lc, kvd), cv.reshape(b, lc, kvd))
    vtn = vt_bf.reshape(b, t // wn, N_KV_HEADS * VT_ROWS, wn)
    o_attn = _dsa(r3(qi_bf), r3(wi_f), r3(q_bf), r3(ki_bf), r3(k_bf), vtn, cache, pos0, tq, tk)
    o_hg, s_new = _hgrn(hq, lf, kk, hi, hg, hg_norm_w.reshape(1, HG_DV), s0, b, t, rb, 4)

    x1, h2 = _merge_out(x, o_attn.reshape(b * t, d), o_hg, ga, gb, w_out, g1, sc2, sh2,
                        norm2_w.reshape(1, 1, d), bb, tb)
    y = _mlp(h2, w_up, w_down, x1, g2, final_w.reshape(1, 1, d), mlp_bb, mlp_tb, 512)
    return (y, k_f.reshape(b, t, N_KV_HEADS, HEAD_DIM), v_f.reshape(b, t, N_KV_HEADS, HEAD_DIM),
            ki_f.reshape(b, t, IDX_DIM), s_new)


def kernel(x_prompt, x_sample, cache_k, cache_v, cache_ki, state_hgrn, c_prompt, c_sample, w_ada, b_ada, norm1_w,
           w_in, hg_lb_logits, hg_norm_w, w_out, norm2_w, w_up, w_down, final_norm_w):
    depth = w_in.shape[0]
    assert depth == 1
    d = x_prompt.shape[-1]
    bp, tp, _ = x_prompt.shape
    bs, ts, _ = x_sample.shape
    past_len = cache_k.shape[2]

    c_all = jnp.concatenate([c_prompt, c_sample], axis=0)
    nrow = c_all.shape[0]
    pad = (-nrow) % SUBLANES
    c_all = jnp.pad(c_all, ((0, pad), (0, 0)))
    mod = _ada(c_all, w_ada[0], b_ada[0].reshape(1, -1)).reshape(nrow + pad, 6, d)

    sizes = (N_HEADS * HEAD_DIM, N_KV_HEADS * HEAD_DIM, N_KV_HEADS * HEAD_DIM, IDX_HEADS * IDX_DIM, IDX_DIM,
             IDX_HEADS, HG_HEADS * HG_DK, HG_HEADS * HG_DK, HG_HEADS * HG_DV, HG_HEADS * HG_DV, d, d)
    offs = np.concatenate([[0], np.cumsum(sizes)])
    w0 = w_in[0]
    w_parts = []
    for i, sz in enumerate(sizes):
        wp = w0[:, int(offs[i]):int(offs[i + 1])].astype(BF16)
        if sz % LANES:
            wp = jnp.pad(wp, ((0, 0), (0, LANES - sz % LANES)))
        w_parts.append(wp)
    wts = (norm1_w[0], tuple(w_parts), hg_lb_logits.astype(F32), hg_norm_w[0], w_out[0].astype(BF16), norm2_w[0],
           w_up[0].astype(BF16), w_down[0].astype(BF16), final_norm_w)

    yp, kp, vp, kip, sp = _trunk(x_prompt, mod[:bp], 0, None, wts, (1, 256, 1024, 128, 512, 512, 1, 512))
    past = (cache_k[0], cache_v[0], cache_ki[0], state_hgrn[0])
    ys, ks, vs, kis, ss = _trunk(x_sample, mod[bp:bp + bs], past_len, past, wts,
                                 (bs // 2, ts, bs * ts, ts, 512, ts, bs, ts))
    return (yp, ys, kp[None], vp[None], kip[None], sp[None], ks[None], vs[None], kis[None], ss[None])
```

```python
import functools
import math

import jax
import jax.numpy as jnp
import numpy as np
from jax import lax
from jax.experimental import pallas as pl
from jax.experimental.pallas import tpu as pltpu

CHUNK = 64
N_HEADS = 16
HEAD_DIM = 128
N_KV_HEADS = 4
GROUP = N_HEADS // N_KV_HEADS
ROT_DIM = HEAD_DIM // 4
ROPE_THETA = 500000.0
IDX_HEADS = 16
IDX_DIM = 128
TOPK_MAX = 256
HG_HEADS = 16
HG_DK = 128
HG_DV = 128
EPS = 1e-6
LANES = 128
SUBLANES = 8
BF16_ROWS = 16
SUB_BLOCK = 8
PROJ_TN = 512
CNT_ROWS = 64
VT_ROWS = HEAD_DIM + BF16_ROWS
VMEM_LIMIT = 56 * 1024 * 1024
NEG_INF = float("-inf")
POS_INF = float("inf")
LOG2E = math.log2(math.e)

F32 = jnp.float32
BF16 = jnp.bfloat16


def _cparams(sem):
    return pltpu.CompilerParams(dimension_semantics=sem, vmem_limit_bytes=VMEM_LIMIT)


def _dot_nt(a, b):
    return lax.dot_general(a, b, (((1,), (1,)), ((), ())), preferred_element_type=F32)


def _dot_tn(a, b):
    return lax.dot_general(a, b, (((0,), (0,)), ((), ())), preferred_element_type=F32)


def _dot(a, b):
    return jnp.dot(a, b, preferred_element_type=F32)


def _sigmoid(x):
    return 1.0 / (1.0 + jnp.exp(-x))


def _col_reduce(op, x):
    rows, n = x.shape
    if rows > CNT_ROWS and rows % CNT_ROWS == 0:
        x = op(x.reshape(rows // CNT_ROWS, CNT_ROWS, n), axis=0)
    return op(x, axis=0, keepdims=True)


def _ada_kernel(c_ref, w_ref, b_ref, o_ref):
    o_ref[...] = _dot(c_ref[...].astype(BF16), w_ref[...].astype(BF16)) + b_ref[...]


def _ada(c_all, w_ada, b_ada, tn=1024):
    r, d = c_all.shape
    n = w_ada.shape[1]
    return pl.pallas_call(
        _ada_kernel,
        out_shape=jax.ShapeDtypeStruct((r, n), F32),
        grid=(n // tn,),
        in_specs=[pl.BlockSpec((r, d), lambda j: (0, 0)),
                  pl.BlockSpec((d, tn), lambda j: (0, j)),
                  pl.BlockSpec((1, tn), lambda j: (0, j))],
        out_specs=pl.BlockSpec((r, tn), lambda j: (0, j)),
        compiler_params=_cparams(("arbitrary",)),
        name="ada",
    )(c_all, w_ada, b_ada)


def _normmod_kernel(x_ref, sc_ref, sh_ref, g_ref, o_ref):
    x = x_ref[...]
    ms = jnp.mean(x * x, axis=-1, keepdims=True)
    xn = x * lax.rsqrt(ms + EPS) * g_ref[...]
    o_ref[...] = (xn * (1.0 + sc_ref[...]) + sh_ref[...]).astype(o_ref.dtype)


def _normmod(x, sc, sh, g, bb, tb):
    b, t, d = x.shape
    return pl.pallas_call(
        _normmod_kernel,
        out_shape=jax.ShapeDtypeStruct((b, t, d), BF16),
        grid=(b // bb, t // tb),
        in_specs=[pl.BlockSpec((bb, tb, d), lambda i, j: (i, j, 0)),
                  pl.BlockSpec((bb, 1, d), lambda i, j: (i, 0, 0)),
                  pl.BlockSpec((bb, 1, d), lambda i, j: (i, 0, 0)),
                  pl.BlockSpec((1, 1, d), lambda i, j: (0, 0, 0))],
        out_specs=pl.BlockSpec((bb, tb, d), lambda i, j: (i, j, 0)),
        compiler_params=_cparams(("arbitrary", "arbitrary")),
        name="normmod",
    )(x, sc, sh, g)


def _proj_plain_kernel(h_ref, w_ref, *o_refs, scale):
    z = _dot(h_ref[...], w_ref[...])
    if scale != 1.0:
        z = z * scale
    for o_ref in o_refs:
        o_ref[...] = z.astype(o_ref.dtype)


def _proj_rope_kernel(h_ref, w_ref, cos_ref, sup_ref, sdn_ref, *o_refs, scale):
    z = _dot(h_ref[...], w_ref[...])
    tn = z.shape[1]
    reps = tn // LANES

    def wide(ref):
        t = ref[...]
        return t if reps == 1 else jnp.concatenate([t] * reps, axis=1)

    up = pltpu.roll(z, tn - ROT_DIM // 2, 1)
    dn = pltpu.roll(z, ROT_DIM // 2, 1)
    r = z * wide(cos_ref) + up * wide(sup_ref) + dn * wide(sdn_ref)
    for o_ref in o_refs:
        if o_ref.dtype == BF16 and scale != 1.0:
            o_ref[...] = (r * scale).astype(BF16)
        else:
            o_ref[...] = r.astype(o_ref.dtype)


def _proj_forget_kernel(h_ref, w_ref, lbl_ref, lf_ref, kk_ref):
    z = _dot(h_ref[...], w_ref[...])
    lbl = lbl_ref[...]
    mx = jnp.max(lbl, axis=0, keepdims=True)
    e = jnp.exp(lbl - mx)
    lb = e[0:1, :] / jnp.sum(e, axis=0, keepdims=True)
    f = lb + (1.0 - lb) * _sigmoid(z)
    lf_ref[...] = jnp.log(f) * LOG2E
    kk_ref[...] = 1.0 - f


def _proj(kind, h, wspec, tm, tn, extra=(), out_dtypes=(F32,), scale=1.0):
    w, c0, n = wspec
    m, k = h.shape
    off = c0 // tn
    assert c0 % tn == 0 and n % tn == 0
    in_specs = [pl.BlockSpec((tm, k), lambda i, j: (i, 0)),
                pl.BlockSpec((k, tn), lambda i, j: (0, off + j))]
    if kind == "rope":
        kern = functools.partial(_proj_rope_kernel, scale=scale)
        in_specs += [pl.BlockSpec((tm, LANES), lambda i, j: (i, 0))] * 3
    elif kind == "forget":
        kern = _proj_forget_kernel
        in_specs += [pl.BlockSpec((extra[0].shape[0], tn), lambda i, j: (0, j))]
    else:
        kern = functools.partial(_proj_plain_kernel, scale=scale)
    outs = tuple(jax.ShapeDtypeStruct((m, n), dt) for dt in out_dtypes)
    out_specs = tuple(pl.BlockSpec((tm, tn), lambda i, j: (i, j)) for _ in out_dtypes)
    res = pl.pallas_call(
        kern,
        out_shape=outs,
        grid=(m // tm, n // tn),
        in_specs=in_specs,
        out_specs=out_specs,
        compiler_params=_cparams(("arbitrary", "arbitrary")),
        name="proj_" + kind,
    )(h, w, *extra)
    return res


def _proj_v_kernel(h_ref, w_ref, vf_ref, vt_ref, *, wn):
    z = _dot(h_ref[...], w_ref[...])
    vf_ref[...] = z
    zt = z.T
    ones = jnp.ones((BF16_ROWS, wn), BF16)
    for u in range(z.shape[0] // wn):
        blk = zt[:, u * wn:(u + 1) * wn].astype(BF16)
        for g in range(N_KV_HEADS):
            vt_ref[u, g * VT_ROWS:g * VT_ROWS + HEAD_DIM, :] = blk[g * HEAD_DIM:(g + 1) * HEAD_DIM]
            vt_ref[u, g * VT_ROWS + HEAD_DIM:(g + 1) * VT_ROWS, :] = ones


def _proj_v(h, wspec, tm, wn):
    w, c0, n = wspec
    m, k = h.shape
    off = c0 // n
    assert c0 % n == 0
    return pl.pallas_call(
        functools.partial(_proj_v_kernel, wn=wn),
        out_shape=(jax.ShapeDtypeStruct((m, n), F32),
                   jax.ShapeDtypeStruct((m // wn, N_KV_HEADS * VT_ROWS, wn), BF16)),
        grid=(m // tm,),
        in_specs=[pl.BlockSpec((tm, k), lambda i: (i, 0)),
                  pl.BlockSpec((k, n), lambda i: (0, off))],
        out_specs=(pl.BlockSpec((tm, n), lambda i: (i, 0)),
                   pl.BlockSpec((tm // wn, N_KV_HEADS * VT_ROWS, wn), lambda i: (i, 0, 0))),
        compiler_params=_cparams(("arbitrary",)),
        name="proj_v",
    )(h, w)


def _dsa_kernel(*refs, tq, tk, t_new, l_cache, pos0, topk, has_cache):
    if has_cache:
        (qi_ref, wi_ref, q_ref, kin_ref, kn_ref, vtn_ref, cki_ref, ck_ref, cv_ref, o_ref,
         sc_ref, qit_ref, wt_ref, qt_ref, acc_ref, s_ref, p_ref) = refs
    else:
        (qi_ref, wi_ref, q_ref, kin_ref, kn_ref, vtn_ref, o_ref,
         sc_ref, qit_ref, wt_ref, qt_ref, acc_ref, s_ref, p_ref) = refs
        cki_ref = ck_ref = cv_ref = None
    i = pl.program_id(1)
    ntc = l_cache // tk
    wn = min(tk, t_new)
    nq = GROUP * tq
    q0 = pos0 + i * tq
    lane_q = lax.broadcasted_iota(jnp.int32, (1, tq), 1)
    qend = (((q0 + lane_q) >> 6) + 1) << 6
    last_end = (((q0 + tq - 1) >> 6) + 1) << 6
    nvis_new = jnp.minimum(last_end - pos0, t_new)
    ntn = (nvis_new + wn - 1) // wn
    ntiles = ntc + ntn

    def tpose(x):
        xf = x.astype(F32)
        if tq < LANES:
            xf = jnp.concatenate([xf, jnp.zeros((LANES - tq, LANES), F32)], axis=0)
        xt = xf.T
        return xt[:, :tq] if tq < LANES else xt

    qi_blk = qi_ref[0]
    for h in range(IDX_HEADS):
        qit_ref[:, h * tq:(h + 1) * tq] = tpose(qi_blk[:, h * IDX_DIM:(h + 1) * IDX_DIM]).astype(BF16)
    wt_ref[...] = tpose(wi_ref[0])
    q_blk = q_ref[0]
    for g in range(N_KV_HEADS):
        for a in range(GROUP):
            hh = GROUP * g + a
            qt_ref[g, :, a * tq:(a + 1) * tq] = tpose(q_blk[:, hh * HEAD_DIM:(hh + 1) * HEAD_DIM]).astype(BF16)

    def score_tile(ki_tile, kpos0, w):
        acc = jnp.zeros((w, tq), F32)
        hpd = 2 if 2 * tq % LANES == 0 else 1
        for hp in range(IDX_HEADS // hpd):
            lg = _dot(ki_tile, qit_ref[:, hpd * hp * tq:(hpd * hp + hpd) * tq])
            for e in range(hpd):
                h = hpd * hp + e
                acc = acc + wt_ref[h:h + 1, :] * jnp.maximum(lg[:, e * tq:(e + 1) * tq], 0.0)
        kpos = kpos0 + lax.broadcasted_iota(jnp.int32, (w, 1), 0)
        vis = kpos < qend
        s = jnp.where(vis, acc, NEG_INF)
        smin = _col_reduce(jnp.min, jnp.where(vis, acc, POS_INF))
        smax = _col_reduce(jnp.max, s)
        return s, smax, smin

    def p1_cache(t, carry):
        mx, mn = carry
        r0 = pl.multiple_of(t * tk, tk)
        s, smax, smin = score_tile(cki_ref[0, pl.ds(r0, tk), :].astype(BF16), r0, tk)
        sc_ref[t] = s
        return jnp.maximum(mx, smax), jnp.minimum(mn, smin)

    def p1_new(j, carry):
        mx, mn = carry
        r0 = pl.multiple_of(j * wn, wn)
        s, smax, smin = score_tile(kin_ref[0, pl.ds(r0, wn), :], pos0 + r0, wn)
        if wn == tk:
            sc_ref[ntc + j] = s
        else:
            sc_ref[ntc + j] = jnp.full((tk, tq), NEG_INF, F32)
            sc_ref[ntc + j, 0:wn, :] = s
        return jnp.maximum(mx, smax), jnp.minimum(mn, smin)

    carry = (jnp.full((1, tq), NEG_INF, F32), jnp.full((1, tq), POS_INF, F32))
    if has_cache:
        carry = lax.fori_loop(0, ntc, p1_cache, carry)
    mx, mn = lax.fori_loop(0, ntn, p1_new, carry)

    nvis = jnp.clip(qend - pos0, 0, t_new)
    if has_cache:
        nvis = nvis + jnp.minimum(qend, l_cache)
    act0 = jnp.where(nvis > topk, 1.0, 0.0)
    kf = float(topk)
    log_k = math.log(kf)

    def bis_cond(carry):
        return jnp.logical_and(carry[0] < 400, carry[1] > 0)

    def bis_body(carry):
        it, _, lo, hi, clo, chi, tau, act = carry
        half = lo * 0.5 + hi * 0.5
        frac = (jnp.log(clo) - log_k) / (jnp.log(clo) - jnp.log(chi))
        guess = lo + (hi - lo) * frac
        use_guess = jnp.logical_and(it % 2 == 0, jnp.logical_and(guess > lo, guess < hi))
        mid = jnp.where(use_guess, guess, half)

        def cnt_body(t, cnt):
            ind = jnp.where(sc_ref[t] >= mid, 1.0, 0.0)
            return cnt + jnp.sum(ind.reshape(tk // CNT_ROWS, CNT_ROWS, tq), axis=0)

        cnt = lax.fori_loop(0, ntiles, cnt_body, jnp.zeros((CNT_ROWS, tq), F32))
        c = jnp.sum(cnt, axis=0, keepdims=True)
        found = jnp.where(c == kf, act, 0.0)
        stuck = jnp.where(jnp.logical_or(mid <= lo, mid >= hi), act, 0.0) * (1.0 - found)
        above = jnp.where(c > kf, act, 0.0) * (1.0 - stuck)
        below = act * (1.0 - above) * (1.0 - found) * (1.0 - stuck)
        tau = jnp.where(found > 0, mid, tau)
        tau = jnp.where(stuck > 0, lo, tau)
        lo = jnp.where(above > 0, mid, lo)
        hi = jnp.where(below > 0, mid, hi)
        clo = jnp.where(above > 0, c, clo)
        chi = jnp.where(below > 0, c, chi)
        nact = act * (1.0 - found) * (1.0 - stuck)
        return it + 1, (jnp.max(nact) > 0).astype(jnp.int32), lo, hi, clo, chi, tau, nact

    init = (jnp.int32(0), (jnp.max(act0) > 0).astype(jnp.int32), mn, mx, nvis.astype(F32),
            jnp.ones((1, tq), F32), mn, act0)
    tau = lax.while_loop(bis_cond, bis_body, init)[6]

    acc_ref[...] = jnp.zeros(acc_ref.shape, F32)

    def attend(t, k_of_g, vt_of_g, w, ms):
        s_idx = sc_ref[t]
        if w < tk:
            s_idx = s_idx[0:w]
        bias = jnp.where(s_idx >= tau, 0.0, NEG_INF)
        bias4 = jnp.concatenate([bias] * GROUP, axis=1)
        def qk(g):
            s = _dot(k_of_g(g), qt_ref[g]) + bias4
            s_ref[g % 2, 0:w] = s
            return _col_reduce(jnp.max, s)

        out = []
        tile_max = qk(0)
        for g in range(N_KV_HEADS):
            slot = g % 2
            next_max = qk(g + 1) if g + 1 < N_KV_HEADS else None
            m_old = ms[g]
            m_new = jnp.maximum(m_old, tile_max)
            m_safe = jnp.where(m_new == NEG_INF, 0.0, m_new)
            alpha = jnp.exp2(m_old - m_safe)
            p_ref[slot, 0:w] = jnp.exp2(s_ref[slot, 0:w] - m_safe).astype(BF16)
            acc_ref[g] = acc_ref[g] * alpha + _dot(vt_of_g(g), p_ref[slot, 0:w])
            out.append(m_new)
            tile_max = next_max
        return tuple(out)

    def p3_cache(t, ms):
        r0 = pl.multiple_of(t * tk, tk)

        def vt(g):
            vg = cv_ref[0, pl.ds(r0, tk), g, :]
            return jnp.concatenate([vg.T, jnp.ones((BF16_ROWS, tk), F32)], axis=0).astype(BF16)

        return attend(t, lambda g: ck_ref[0, pl.ds(r0, tk), g, :].astype(BF16), vt, tk, ms)

    def p3_new(j, ms):
        r0 = pl.multiple_of(j * wn, wn)
        return attend(ntc + j,
                      lambda g: kn_ref[0, pl.ds(r0, wn), g * HEAD_DIM:(g + 1) * HEAD_DIM],
                      lambda g: vtn_ref[0, j, g * VT_ROWS:(g + 1) * VT_ROWS, :], wn, ms)

    ms = tuple(jnp.full((1, nq), NEG_INF, F32) for _ in range(N_KV_HEADS))
    if has_cache:
        ms = lax.fori_loop(0, ntc, p3_cache, ms)
    lax.fori_loop(0, ntn, p3_new, ms)

    for g in range(N_KV_HEADS):
        acc = acc_ref[g]
        og = (acc[0:HEAD_DIM] / acc[HEAD_DIM:HEAD_DIM + 1]).T
        for a in range(GROUP):
            hh = GROUP * g + a
            o_ref[0, :, hh * HEAD_DIM:(hh + 1) * HEAD_DIM] = og[a * tq:(a + 1) * tq, :]


def _dsa(qi, wi, q, kin, kn, vtn, cache, pos0, tq, tk):
    b, t, _ = q.shape
    has_cache = cache is not None
    l_cache = cache[0].shape[1] if has_cache else 0
    topk = min(TOPK_MAX, (l_cache + t) // 4)
    wn = min(tk, t)
    ntn = t // wn
    nt = l_cache // tk + ntn
    kvd = N_KV_HEADS * HEAD_DIM
    nq = GROUP * tq
    res = pl.Buffered(1) if b == 1 else None
    in_specs = [pl.BlockSpec((1, tq, IDX_HEADS * IDX_DIM), lambda bb, i: (bb, i, 0)),
                pl.BlockSpec((1, tq, LANES), lambda bb, i: (bb, i, 0)),
                pl.BlockSpec((1, tq, N_HEADS * HEAD_DIM), lambda bb, i: (bb, i, 0)),
                pl.BlockSpec((1, t, IDX_DIM), lambda bb, i: (bb, 0, 0), pipeline_mode=res),
                pl.BlockSpec((1, t, kvd), lambda bb, i: (bb, 0, 0), pipeline_mode=res),
                pl.BlockSpec((1, ntn, N_KV_HEADS * VT_ROWS, wn), lambda bb, i: (bb, 0, 0, 0), pipeline_mode=res)]
    args = [qi, wi, q, kin, kn, vtn]
    if has_cache:
        in_specs += [pl.BlockSpec((1, l_cache, IDX_DIM), lambda bb, i: (bb, 0, 0)),
                     pl.BlockSpec((1, l_cache, N_KV_HEADS, HEAD_DIM), lambda bb, i: (bb, 0, 0, 0)),
                     pl.BlockSpec((1, l_cache, N_KV_HEADS, HEAD_DIM), lambda bb, i: (bb, 0, 0, 0))]
        args += list(cache)
    kern = functools.partial(_dsa_kernel, tq=tq, tk=tk, t_new=t, l_cache=l_cache, pos0=pos0, topk=topk,
                             has_cache=has_cache)
    return pl.pallas_call(
        kern,
        out_shape=jax.ShapeDtypeStruct((b, t, N_HEADS * HEAD_DIM), F32),
        grid=(b, t // tq),
        in_specs=in_specs,
        out_specs=pl.BlockSpec((1, tq, N_HEADS * HEAD_DIM), lambda bb, i: (bb, i, 0)),
        scratch_shapes=[pltpu.VMEM((nt, tk, tq), F32),
                        pltpu.VMEM((IDX_DIM, IDX_HEADS * tq), BF16),
                        pltpu.VMEM((LANES, tq), F32),
                        pltpu.VMEM((N_KV_HEADS, HEAD_DIM, nq), BF16),
                        pltpu.VMEM((N_KV_HEADS, VT_ROWS, nq), F32),
                        pltpu.VMEM((2, tk, nq), F32),
                        pltpu.VMEM((2, tk, nq), BF16)],
        compiler_params=_cparams(("arbitrary", "arbitrary")),
        name="dsa",
    )(*args)


def _hgrn_kernel(*refs, c, nchunk, nh, has_state):
    if has_state:
        q_ref, lf_ref, kk_ref, v_ref, hg_ref, nw_ref, s0_ref, o_ref, sout_ref, st_ref = refs
    else:
        q_ref, lf_ref, kk_ref, v_ref, hg_ref, nw_ref, o_ref, sout_ref, st_ref = refs
        s0_ref = None
    r = pl.program_id(2)

    @pl.when(r == 0)
    def _():
        for hh in range(nh):
            if has_state:
                st_ref[hh] = s0_ref[0, hh].T
            else:
                st_ref[hh] = jnp.zeros((HG_DV, HG_DK), F32)

    ri = lax.broadcasted_iota(jnp.int32, (c, c), 0)
    ci = lax.broadcasted_iota(jnp.int32, (c, c), 1)
    tri = jnp.where(ri >= ci, 1.0, 0.0).astype(BF16)
    nsub = c // SUB_BLOCK
    pair = lax.broadcasted_iota(jnp.int32, (SUB_BLOCK * SUB_BLOCK, HG_DK), 0)
    cap = jnp.where((pair % SUB_BLOCK) >= (pair // SUB_BLOCK), 0.0, NEG_INF)
    ones_w = jnp.ones((HG_DK, LANES), BF16)

    def rep_rows(x, lo_):
        return jnp.concatenate(
            [jnp.broadcast_to(x[lo_ + sg:lo_ + sg + 1], (SUB_BLOCK, x.shape[1])) for sg in range(SUB_BLOCK)], axis=0)

    def stage_a(base, hh):
        cs = slice(hh * HG_DK, (hh + 1) * HG_DK)
        lf = lf_ref[pl.ds(base, c), cs]
        l1 = lf.astype(BF16)
        r1 = lf - l1.astype(F32)
        l2 = r1.astype(BF16)
        l3 = (r1 - l2.astype(F32)).astype(BF16)
        G = _dot(tri, l1) + _dot(tri, l2) + _dot(tri, l3)
        return dict(cs=cs, G=G, q=q_ref[pl.ds(base, c), cs], kk=kk_ref[pl.ds(base, c), cs],
                    v=v_ref[pl.ds(base, c), cs])

    def stage_b(hh, d):
        q, kk, v, G = d["q"], d["kk"], d["v"], d["G"]
        st = st_ref[hh]
        vb = v.astype(BF16)
        d["vb"] = vb
        d["o_inter"] = _dot_nt((q * jnp.exp2(G)).astype(BF16), st.astype(BF16))
        d["A"], d["rs"] = [], []
        for i in range(nsub):
            lo_, hi_ = i * SUB_BLOCK, (i + 1) * SUB_BLOCK
            qi_ = q[lo_:hi_]
            Gi = G[lo_:hi_]
            if i > 0:
                Gb = G[lo_ - 1:lo_]
                qt = qi_ * jnp.exp2(Gi - Gb)
                kt = kk[:lo_] * jnp.exp2(Gb - G[:lo_])
                d["A"].append(_dot_nt(qt.astype(BF16), kt.astype(BF16)))
            qrep = jnp.concatenate([qi_] * SUB_BLOCK, axis=0)
            grep = jnp.concatenate([Gi] * SUB_BLOCK, axis=0)
            D = qrep * rep_rows(kk, lo_) * jnp.exp2(jnp.minimum(grep - rep_rows(G, lo_), cap))
            d["rs"].append(_dot(D.astype(BF16), ones_w))
        Gl = G[c - 1:c]
        kdec = kk * jnp.exp2(Gl - G)
        st_ref[hh] = st * jnp.exp2(Gl) + _dot_tn(vb, kdec.astype(BF16))

    def stage_c(base, d):
        parts = []
        for i in range(nsub):
            lo_, hi_ = i * SUB_BLOCK, (i + 1) * SUB_BLOCK
            oi = d["o_inter"][lo_:hi_]
            if i > 0:
                oi = oi + _dot(d["A"][i - 1].astype(BF16), d["vb"][:lo_])
            contrib = (d["rs"][i] * rep_rows(d["v"], lo_)).reshape(SUB_BLOCK, SUB_BLOCK, HG_DV)
            parts.append(oi + jnp.sum(contrib, axis=0))
        o = jnp.concatenate(parts, axis=0) if nsub > 1 else parts[0]
        on = o * lax.rsqrt(jnp.mean(o * o, axis=1, keepdims=True) + EPS) * nw_ref[...]
        hg = hg_ref[pl.ds(base, c), d["cs"]]
        o_ref[pl.ds(base, c), d["cs"]] = on * (hg * _sigmoid(hg))

    def chunk(n, carry):
        base = pl.multiple_of(n * c, c)
        heads = [stage_a(base, hh) for hh in range(nh)]
        for hh, d in enumerate(heads):
            stage_b(hh, d)
        for d in heads:
            stage_c(base, d)
        return carry

    lax.fori_loop(0, nchunk, chunk, 0)

    @pl.when(r == pl.num_programs(2) - 1)
    def _():
        for hh in range(nh):
            sout_ref[0, hh] = st_ref[hh].T


def _hgrn(hq, lf, kk, hv, hg, nw, s0, b, t, rb, nh):
    c = min(CHUNK, t)
    nchunk = rb // c
    nr = t // rb
    has_state = s0 is not None
    blk = pl.BlockSpec((rb, nh * HG_DK), lambda bb, h, r: (bb * nr + r, h))
    sblk = pl.BlockSpec((1, nh, HG_DK, HG_DV), lambda bb, h, r: (bb, h, 0, 0))
    in_specs = [blk, blk, blk, blk, blk, pl.BlockSpec((1, HG_DV), lambda bb, h, r: (0, 0))]
    args = [hq, lf, kk, hv, hg, nw]
    if has_state:
        in_specs.append(sblk)
        args.append(s0)
    kern = functools.partial(_hgrn_kernel, c=c, nchunk=nchunk, nh=nh, has_state=has_state)
    return pl.pallas_call(
        kern,
        out_shape=(jax.ShapeDtypeStruct((b * t, HG_HEADS * HG_DV), F32),
                   jax.ShapeDtypeStruct((b, HG_HEADS, HG_DK, HG_DV), F32)),
        grid=(b, HG_HEADS // nh, nr),
        in_specs=in_specs,
        out_specs=(blk, sblk),
        scratch_shapes=[pltpu.VMEM((nh, HG_DV, HG_DK), F32)],
        compiler_params=_cparams(("arbitrary", "arbitrary", "arbitrary")),
        name="hgrn",
    )(*args)


def _merge_out_kernel(x_ref, oa_ref, oh_ref, ga_ref, gb_ref, w_ref, g1_ref, sc_ref, sh_ref, nw_ref,
                      x1_ref, h2_ref):
    bb, tb, d = x_ref.shape
    merged = _sigmoid(ga_ref[...]) * oa_ref[...] + _sigmoid(gb_ref[...]) * oh_ref[...]
    y = _dot(merged.astype(BF16), w_ref[...]).reshape(bb, tb, d)
    x1 = x_ref[...] + g1_ref[...] * y
    x1_ref[...] = x1
    ms = jnp.mean(x1 * x1, axis=-1, keepdims=True)
    xn = x1 * lax.rsqrt(ms + EPS) * nw_ref[...]
    h2 = xn * (1.0 + sc_ref[...]) + sh_ref[...]
    h2_ref[...] = h2.reshape(bb * tb, d).astype(BF16)


def _merge_out(x, oa, oh, ga, gb, w_out, g1, sc2, sh2, nw2, bb, tb):
    b, t, d = x.shape
    tm = bb * tb
    nt = t // tb

    def row(i, j):
        return (i * nt + j, 0)

    def mod(i, j):
        return (i, 0, 0)

    x3 = pl.BlockSpec((bb, tb, d), lambda i, j: (i, j, 0))
    r2 = pl.BlockSpec((tm, d), row)
    return pl.pallas_call(
        _merge_out_kernel,
        out_shape=(jax.ShapeDtypeStruct((b, t, d), F32), jax.ShapeDtypeStruct((b * t, d), BF16)),
        grid=(b // bb, nt),
        in_specs=[x3, r2, r2, r2, r2,
                  pl.BlockSpec((d, d), lambda i, j: (0, 0)),
                  pl.BlockSpec((bb, 1, d), mod), pl.BlockSpec((bb, 1, d), mod), pl.BlockSpec((bb, 1, d), mod),
                  pl.BlockSpec((1, 1, d), lambda i, j: (0, 0, 0))],
        out_specs=(x3, r2),
        compiler_params=_cparams(("arbitrary", "arbitrary")),
        name="merge_out",
    )(x, oa, oh, ga, gb, w_out, g1, sc2, sh2, nw2)


def _mlp_kernel(h_ref, wu_ref, wd_ref, x1_ref, g2_ref, fw_ref, y_ref, acc_ref):
    f = pl.program_id(2)

    @pl.when(f == 0)
    def _():
        acc_ref[...] = jnp.zeros(acc_ref.shape, F32)

    u = jnp.maximum(_dot(h_ref[...], wu_ref[...]), 0.0)
    acc_ref[...] += _dot((u * u).astype(BF16), wd_ref[...])

    @pl.when(f == pl.num_programs(2) - 1)
    def _():
        bb, tb, d = x1_ref.shape
        x2 = x1_ref[...] + g2_ref[...] * acc_ref[...].reshape(bb, tb, d)
        ms = jnp.mean(x2 * x2, axis=-1, keepdims=True)
        y_ref[...] = x2 * lax.rsqrt(ms + EPS) * fw_ref[...]


def _mlp(h2, w_up, w_down, x1, g2, fw, bb, tb, tf):
    b, t, d = x1.shape
    dff = w_up.shape[1]
    tm = bb * tb
    nt = t // tb
    x3 = pl.BlockSpec((bb, tb, d), lambda i, j, f: (i, j, 0))
    return pl.pallas_call(
        _mlp_kernel,
        out_shape=jax.ShapeDtypeStruct((b, t, d), F32),
        grid=(b // bb, nt, dff // tf),
        in_specs=[pl.BlockSpec((tm, d), lambda i, j, f: (i * nt + j, 0)),
                  pl.BlockSpec((d, tf), lambda i, j, f: (0, f)),
                  pl.BlockSpec((tf, d), lambda i, j, f: (f, 0)),
                  x3,
                  pl.BlockSpec((bb, 1, d), lambda i, j, f: (i, 0, 0)),
                  pl.BlockSpec((1, 1, d), lambda i, j, f: (0, 0, 0))],
        out_specs=x3,
        scratch_shapes=[pltpu.VMEM((tm, d), F32)],
        compiler_params=_cparams(("arbitrary", "arbitrary", "arbitrary")),
        name="mlp",
    )(h2, w_up, w_down, x1, g2, fw)


def _rope_tables(pos):
    half = ROT_DIM // 2
    inv_freq = ROPE_THETA ** (-(jnp.arange(half, dtype=F32) * (2.0 / ROT_DIM)))
    ang = pos.astype(F32)[:, None] * inv_freq[None, :]
    cos, sin = jnp.cos(ang), jnp.sin(ang)
    n = pos.shape[0]
    ones = jnp.ones((n, LANES - ROT_DIM), F32)
    zeros = jnp.zeros((n, LANES - ROT_DIM), F32)
    zh = jnp.zeros((n, half), F32)
    c_t = jnp.concatenate([cos, cos, ones], axis=1)
    s_up = jnp.concatenate([-sin, zh, zeros], axis=1)
    s_dn = jnp.concatenate([zh, sin, zeros], axis=1)
    return c_t, s_up, s_dn


def _trunk(x, mod, pos0, past, wts, blocks):
    (norm1_w, w_parts, lb_logits, hg_norm_w, w_out, norm2_w, w_up, w_down, final_w) = wts
    b, t, d = x.shape
    bb, tb, tm_proj, tq, tk, rb, mlp_bb, mlp_tb = blocks
    m = [mod[:, i:i + 1, :] for i in range(6)]
    sh1, sc1, g1, sh2, sc2, g2 = m
    h = _normmod(x, sc1, sh1, norm1_w.reshape(1, 1, d), bb, tb).reshape(b * t, d)

    pos = pos0 + jnp.arange(t, dtype=jnp.int32)
    tabs = tuple(jnp.tile(tb_, (b, 1)) for tb_ in _rope_tables(pos))
    wq, wk, wv, wqi, wki, wwi, whq, whf, whi, whg, wga, wgb = w_parts
    tm = tm_proj
    wn = min(tk, t)
    (q_bf,) = _proj("rope", h, wq, tm, PROJ_TN, tabs, (BF16,), scale=HEAD_DIM ** -0.5 * LOG2E)
    k_f, k_bf = _proj("rope", h, wk, tm, PROJ_TN, tabs, (F32, BF16))
    v_f, vt_bf = _proj_v(h, wv, tm, wn)
    (qi_bf,) = _proj("rope", h, wqi, tm, PROJ_TN, tabs, (BF16,))
    ki_f, ki_bf = _proj("rope", h, wki, tm, LANES, tabs, (F32, BF16))
    (wi_f,) = _proj("plain", h, wwi, tm, LANES, (), (F32,), scale=IDX_HEADS ** -0.5 * IDX_DIM ** -0.5)
    (hq,) = _proj("plain", h, whq, tm, PROJ_TN)
    lf, kk = _proj("forget", h, whf, tm, PROJ_TN, (lb_logits,), (F32, F32))
    (hi,) = _proj("plain", h, whi, tm, PROJ_TN)
    (hg,) = _proj("plain", h, whg, tm, PROJ_TN)
    (ga,) = _proj("plain", h, wga, tm, PROJ_TN)
    (gb,) = _proj("plain", h, wgb, tm, PROJ_TN)

    kvd = N_KV_HEADS * HEAD_DIM
    r3 = lambda a: a.reshape(b, t, a.shape[-1])
    if past is None:
        cache, s0 = None, None
    else:
        ck, cv, cki, s0 = past
        lc = ck.shape[1]
        cache = (cki, ck, cv)
    vtn = vt_bf.reshape(b, t // wn, N_KV_HEADS * VT_ROWS, wn)
    o_attn = _dsa(r3(qi_bf), r3(wi_f), r3(q_bf), r3(ki_bf), r3(k_bf), vtn, cache, pos0, tq, tk)
    o_hg, s_new = _hgrn(hq, lf, kk, hi, hg, hg_norm_w.reshape(1, HG_DV), s0, b, t, rb, 8)

    x1, h2 = _merge_out(x, o_attn.reshape(b * t, d), o_hg, ga, gb, w_out, g1, sc2, sh2,
                        norm2_w.reshape(1, 1, d), bb, tb)
    y = _mlp(h2, w_up, w_down, x1, g2, final_w.reshape(1, 1, d), mlp_bb, mlp_tb, 512)
    return (y, k_f.reshape(b, t, N_KV_HEADS, HEAD_DIM), v_f.reshape(b, t, N_KV_HEADS, HEAD_DIM),
            ki_f.reshape(b, t, IDX_DIM), s_new)


def kernel(x_prompt, x_sample, cache_k, cache_v, cache_ki, state_hgrn, c_prompt, c_sample, w_ada, b_ada, norm1_w,
           w_in, hg_lb_logits, hg_norm_w, w_out, norm2_w, w_up, w_down, final_norm_w):
    depth = w_in.shape[0]
    assert depth == 1
    d = x_prompt.shape[-1]
    bp, tp, _ = x_prompt.shape
    bs, ts, _ = x_sample.shape
    past_len = cache_k.shape[2]

    c_all = jnp.concatenate([c_prompt, c_sample], axis=0)
    nrow = c_all.shape[0]
    pad = (-nrow) % SUBLANES
    c_all = jnp.pad(c_all, ((0, pad), (0, 0)))
    mod = _ada(c_all, w_ada[0], b_ada[0].reshape(1, -1)).reshape(nrow + pad, 6, d)

    sizes = (N_HEADS * HEAD_DIM, N_KV_HEADS * HEAD_DIM, N_KV_HEADS * HEAD_DIM, IDX_HEADS * IDX_DIM, IDX_DIM,
             IDX_HEADS, HG_HEADS * HG_DK, HG_HEADS * HG_DK, HG_HEADS * HG_DV, HG_HEADS * HG_DV, d, d)
    offs = np.concatenate([[0], np.cumsum(sizes)])
    narrow_end = int(offs[6])
    pad_cols = (-narrow_end) % PROJ_TN
    w0 = w_in[0]
    w_all = jnp.concatenate([w0[:, :narrow_end], jnp.zeros((d, pad_cols), w0.dtype), w0[:, narrow_end:]],
                            axis=1).astype(BF16)
    w_parts = []
    for i, sz in enumerate(sizes):
        c0 = int(offs[i]) + (pad_cols if i >= 6 else 0)
        w_parts.append((w_all, c0, sz + (-sz) % LANES))
    wts = (norm1_w[0], tuple(w_parts), hg_lb_logits.astype(F32), hg_norm_w[0], w_out[0].astype(BF16), norm2_w[0],
           w_up[0].astype(BF16), w_down[0].astype(BF16), final_norm_w)

    yp, kp, vp, kip, sp = _trunk(x_prompt, mod[:bp], 0, None, wts, (1, 256, 1024, 128, 1024, 512, 1, 512))
    past = (cache_k[0], cache_v[0], cache_ki[0], state_hgrn[0])
    ys, ks, vs, kis, ss = _trunk(x_sample, mod[bp:bp + bs], past_len, past, wts,
                                 (bs // 2, ts, bs * ts, ts, 512, ts, bs, ts))
    return (yp, ys, kp[None], vp[None], kip[None], sp[None], ks[None], vs[None], kis[None], ss[None])
```

```python
import functools
import math

import jax
import jax.numpy as jnp
import numpy as np
from jax import lax
from jax.experimental import pallas as pl
from jax.experimental.pallas import tpu as pltpu

CHUNK = 64
N_HEADS = 16
HEAD_DIM = 128
N_KV_HEADS = 4
GROUP = N_HEADS // N_KV_HEADS
ROT_DIM = HEAD_DIM // 4
ROPE_THETA = 500000.0
IDX_HEADS = 16
IDX_DIM = 128
TOPK_MAX = 256
HG_HEADS = 16
HG_DK = 128
HG_DV = 128
EPS = 1e-6
LANES = 128
SUBLANES = 8
BF16_ROWS = 16
SUB_BLOCK = 8
PROJ_TN = 512
COUNT_UNIT = 512
CNT_ROWS = 64
VT_ROWS = HEAD_DIM + BF16_ROWS
VMEM_LIMIT = 56 * 1024 * 1024
NEG_INF = float("-inf")
POS_INF = float("inf")
LOG2E = math.log2(math.e)

F32 = jnp.float32
BF16 = jnp.bfloat16


def _cparams(sem):
    return pltpu.CompilerParams(dimension_semantics=sem, vmem_limit_bytes=VMEM_LIMIT)


def _dot_nt(a, b):
    return lax.dot_general(a, b, (((1,), (1,)), ((), ())), preferred_element_type=F32)


def _dot_tn(a, b):
    return lax.dot_general(a, b, (((0,), (0,)), ((), ())), preferred_element_type=F32)


def _dot(a, b):
    return jnp.dot(a, b, preferred_element_type=F32)


def _sigmoid(x):
    return 1.0 / (1.0 + jnp.exp(-x))


def _col_reduce(op, x):
    rows, n = x.shape
    if rows > CNT_ROWS and rows % CNT_ROWS == 0:
        x = op(x.reshape(rows // CNT_ROWS, CNT_ROWS, n), axis=0)
    return op(x, axis=0, keepdims=True)


def _ada_kernel(c_ref, w_ref, b_ref, o_ref):
    o_ref[...] = _dot(c_ref[...].astype(BF16), w_ref[...].astype(BF16)) + b_ref[...]


def _ada(c_all, w_ada, b_ada, tn=1024):
    r, d = c_all.shape
    n = w_ada.shape[1]
    return pl.pallas_call(
        _ada_kernel,
        out_shape=jax.ShapeDtypeStruct((r, n), F32),
        grid=(n // tn,),
        in_specs=[pl.BlockSpec((r, d), lambda j: (0, 0)),
                  pl.BlockSpec((d, tn), lambda j: (0, j)),
                  pl.BlockSpec((1, tn), lambda j: (0, j))],
        out_specs=pl.BlockSpec((r, tn), lambda j: (0, j)),
        compiler_params=_cparams(("arbitrary",)),
        name="ada",
    )(c_all, w_ada, b_ada)


def _repack_kernel(*refs, n_direct, shift):
    srcs, o_ref = refs[:-1], refs[-1]
    j = pl.program_id(0)

    @pl.when(j < n_direct)
    def _():
        o_ref[...] = jnp.concatenate([r[0] for r in srcs[:-1]], axis=1).astype(BF16)

    @pl.when(j >= n_direct)
    def _():
        wide = jnp.concatenate([r[0] for r in srcs], axis=1)
        o_ref[...] = wide[:, shift:shift + PROJ_TN].astype(BF16)


def _repack_w_in(w_in, narrow_end):
    _, k, n = w_in.shape
    per = PROJ_TN // LANES
    n_direct = -(-narrow_end // PROJ_TN)
    n_shift = -(-(n - narrow_end) // PROJ_TN)
    first = narrow_end // LANES

    def src(e):
        return pl.BlockSpec((1, k, LANES),
                            lambda j: (0, 0, jnp.where(j < n_direct, per * j, first + per * (j - n_direct)) + e))

    return pl.pallas_call(
        functools.partial(_repack_kernel, n_direct=n_direct, shift=narrow_end % LANES),
        out_shape=jax.ShapeDtypeStruct((k, (n_direct + n_shift) * PROJ_TN), BF16),
        grid=(n_direct + n_shift,),
        in_specs=[src(e) for e in range(per + 1)],
        out_specs=pl.BlockSpec((k, PROJ_TN), lambda j: (0, j)),
        compiler_params=_cparams(("arbitrary",)),
        name="repack_w_in",
    )(*([w_in] * (per + 1)))


def _normmod_kernel(x_ref, sc_ref, sh_ref, g_ref, o_ref):
    x = x_ref[...]
    ms = jnp.mean(x * x, axis=-1, keepdims=True)
    xn = x * lax.rsqrt(ms + EPS) * g_ref[...]
    o_ref[...] = (xn * (1.0 + sc_ref[...]) + sh_ref[...]).astype(o_ref.dtype)


def _normmod(x, sc, sh, g, bb, tb):
    b, t, d = x.shape
    return pl.pallas_call(
        _normmod_kernel,
        out_shape=jax.ShapeDtypeStruct((b, t, d), BF16),
        grid=(b // bb, t // tb),
        in_specs=[pl.BlockSpec((bb, tb, d), lambda i, j: (i, j, 0)),
                  pl.BlockSpec((bb, 1, d), lambda i, j: (i, 0, 0)),
                  pl.BlockSpec((bb, 1, d), lambda i, j: (i, 0, 0)),
                  pl.BlockSpec((1, 1, d), lambda i, j: (0, 0, 0))],
        out_specs=pl.BlockSpec((bb, tb, d), lambda i, j: (i, j, 0)),
        compiler_params=_cparams(("arbitrary", "arbitrary")),
        name="normmod",
    )(x, sc, sh, g)


def _proj_plain_kernel(h_ref, w_ref, *o_refs, scale):
    z = _dot(h_ref[...], w_ref[...])
    if scale != 1.0:
        z = z * scale
    for o_ref in o_refs:
        o_ref[...] = z.astype(o_ref.dtype)


def _proj_rope_kernel(h_ref, w_ref, cos_ref, sup_ref, sdn_ref, *o_refs, scale):
    z = _dot(h_ref[...], w_ref[...])
    tn = z.shape[1]
    reps = tn // LANES

    def wide(ref):
        t = ref[...]
        return t if reps == 1 else jnp.concatenate([t] * reps, axis=1)

    up = pltpu.roll(z, tn - ROT_DIM // 2, 1)
    dn = pltpu.roll(z, ROT_DIM // 2, 1)
    r = z * wide(cos_ref) + up * wide(sup_ref) + dn * wide(sdn_ref)
    for o_ref in o_refs:
        if o_ref.dtype == BF16 and scale != 1.0:
            o_ref[...] = (r * scale).astype(BF16)
        else:
            o_ref[...] = r.astype(o_ref.dtype)


def _proj_forget_kernel(h_ref, w_ref, lbl_ref, lf_ref, kk_ref):
    z = _dot(h_ref[...], w_ref[...])
    lbl = lbl_ref[...]
    mx = jnp.max(lbl, axis=0, keepdims=True)
    e = jnp.exp(lbl - mx)
    lb = e[0:1, :] / jnp.sum(e, axis=0, keepdims=True)
    f = lb + (1.0 - lb) * _sigmoid(z)
    lf_ref[...] = jnp.log(f) * LOG2E
    kk_ref[...] = 1.0 - f


def _proj(kind, h, wspec, tm, tn, extra=(), out_dtypes=(F32,), scale=1.0):
    w, c0, n = wspec
    m, k = h.shape
    off = c0 // tn
    assert c0 % tn == 0 and n % tn == 0
    in_specs = [pl.BlockSpec((tm, k), lambda i, j: (i, 0)),
                pl.BlockSpec((k, tn), lambda i, j: (0, off + j))]
    if kind == "rope":
        kern = functools.partial(_proj_rope_kernel, scale=scale)
        in_specs += [pl.BlockSpec((tm, LANES), lambda i, j: (i, 0))] * 3
    elif kind == "forget":
        kern = _proj_forget_kernel
        in_specs += [pl.BlockSpec((extra[0].shape[0], tn), lambda i, j: (0, j))]
    else:
        kern = functools.partial(_proj_plain_kernel, scale=scale)
    outs = tuple(jax.ShapeDtypeStruct((m, n), dt) for dt in out_dtypes)
    out_specs = tuple(pl.BlockSpec((tm, tn), lambda i, j: (i, j)) for _ in out_dtypes)
    res = pl.pallas_call(
        kern,
        out_shape=outs,
        grid=(m // tm, n // tn),
        in_specs=in_specs,
        out_specs=out_specs,
        compiler_params=_cparams(("arbitrary", "arbitrary")),
        name="proj_" + kind,
    )(h, w, *extra)
    return res


def _proj_v_kernel(h_ref, w_ref, vf_ref, vt_ref, *, wn):
    z = _dot(h_ref[...], w_ref[...])
    vf_ref[...] = z
    zt = z.T
    ones = jnp.ones((BF16_ROWS, wn), BF16)
    for u in range(z.shape[0] // wn):
        blk = zt[:, u * wn:(u + 1) * wn].astype(BF16)
        for g in range(N_KV_HEADS):
            vt_ref[u, g * VT_ROWS:g * VT_ROWS + HEAD_DIM, :] = blk[g * HEAD_DIM:(g + 1) * HEAD_DIM]
            vt_ref[u, g * VT_ROWS + HEAD_DIM:(g + 1) * VT_ROWS, :] = ones


def _proj_v(h, wspec, tm, wn):
    w, c0, n = wspec
    m, k = h.shape
    off = c0 // n
    assert c0 % n == 0
    return pl.pallas_call(
        functools.partial(_proj_v_kernel, wn=wn),
        out_shape=(jax.ShapeDtypeStruct((m, n), F32),
                   jax.ShapeDtypeStruct((m // wn, N_KV_HEADS * VT_ROWS, wn), BF16)),
        grid=(m // tm,),
        in_specs=[pl.BlockSpec((tm, k), lambda i: (i, 0)),
                  pl.BlockSpec((k, n), lambda i: (0, off))],
        out_specs=(pl.BlockSpec((tm, n), lambda i: (i, 0)),
                   pl.BlockSpec((tm // wn, N_KV_HEADS * VT_ROWS, wn), lambda i: (i, 0, 0))),
        compiler_params=_cparams(("arbitrary",)),
        name="proj_v",
    )(h, w)


def _dsa_kernel(*refs, tq, tk, t_new, l_cache, pos0, topk, has_cache):
    if has_cache:
        (qi_ref, wi_ref, q_ref, kin_ref, kn_ref, vtn_ref, cki_ref, ck_ref, cv_ref, o_ref,
         sc_ref, qit_ref, wt_ref, qt_ref, acc_ref, s_ref, p_ref) = refs
    else:
        (qi_ref, wi_ref, q_ref, kin_ref, kn_ref, vtn_ref, o_ref,
         sc_ref, qit_ref, wt_ref, qt_ref, acc_ref, s_ref, p_ref) = refs
        cki_ref = ck_ref = cv_ref = None
    i = pl.program_id(1)
    ntc = l_cache // tk
    wn = min(tk, t_new)
    nq = GROUP * tq
    q0 = pos0 + i * tq
    lane_q = lax.broadcasted_iota(jnp.int32, (1, tq), 1)
    qend = (((q0 + lane_q) >> 6) + 1) << 6
    last_end = (((q0 + tq - 1) >> 6) + 1) << 6
    nvis_new = jnp.minimum(last_end - pos0, t_new)
    ntn = (nvis_new + wn - 1) // wn
    ntiles = ntc + ntn

    def tpose(x):
        xf = x.astype(F32)
        if tq < LANES:
            xf = jnp.concatenate([xf, jnp.zeros((LANES - tq, LANES), F32)], axis=0)
        xt = xf.T
        return xt[:, :tq] if tq < LANES else xt

    qi_blk = qi_ref[0]
    for h in range(IDX_HEADS):
        qit_ref[:, h * tq:(h + 1) * tq] = tpose(qi_blk[:, h * IDX_DIM:(h + 1) * IDX_DIM]).astype(BF16)
    wt_ref[...] = tpose(wi_ref[0])
    q_blk = q_ref[0]
    for g in range(N_KV_HEADS):
        for a in range(GROUP):
            hh = GROUP * g + a
            qt_ref[g, :, a * tq:(a + 1) * tq] = tpose(q_blk[:, hh * HEAD_DIM:(hh + 1) * HEAD_DIM]).astype(BF16)

    def score_tile(ki_tile, kpos0, w):
        acc = jnp.zeros((w, tq), F32)
        hpd = 2 if 2 * tq % LANES == 0 else 1
        for hp in range(IDX_HEADS // hpd):
            lg = _dot(ki_tile, qit_ref[:, hpd * hp * tq:(hpd * hp + hpd) * tq])
            for e in range(hpd):
                h = hpd * hp + e
                acc = acc + wt_ref[h:h + 1, :] * jnp.maximum(lg[:, e * tq:(e + 1) * tq], 0.0)
        kpos = kpos0 + lax.broadcasted_iota(jnp.int32, (w, 1), 0)
        vis = kpos < qend
        s = jnp.where(vis, acc, NEG_INF)
        smin = _col_reduce(jnp.min, jnp.where(vis, acc, POS_INF))
        smax = _col_reduce(jnp.max, s)
        return s, smax, smin

    def p1_cache(t, carry):
        mx, mn = carry
        r0 = pl.multiple_of(t * tk, tk)
        s, smax, smin = score_tile(cki_ref[0, pl.ds(r0, tk), :].astype(BF16), r0, tk)
        sc_ref[t] = s
        return jnp.maximum(mx, smax), jnp.minimum(mn, smin)

    def p1_new(j, carry):
        mx, mn = carry
        r0 = pl.multiple_of(j * wn, wn)
        s, smax, smin = score_tile(kin_ref[0, pl.ds(r0, wn), :], pos0 + r0, wn)
        if wn == tk:
            sc_ref[ntc + j] = s
        else:
            sc_ref[ntc + j] = jnp.full((tk, tq), NEG_INF, F32)
            sc_ref[ntc + j, 0:wn, :] = s
        return jnp.maximum(mx, smax), jnp.minimum(mn, smin)

    carry = (jnp.full((1, tq), NEG_INF, F32), jnp.full((1, tq), POS_INF, F32))
    if has_cache:
        carry = lax.fori_loop(0, ntc, p1_cache, carry)
    mx, mn = lax.fori_loop(0, ntn, p1_new, carry)

    nvis = jnp.clip(qend - pos0, 0, t_new)
    if has_cache:
        nvis = nvis + jnp.minimum(qend, l_cache)
    act0 = jnp.where(nvis > topk, 1.0, 0.0)
    kf = float(topk)
    log_k = math.log(kf)
    cu = min(tk, COUNT_UNIT)
    upt = tk // cu
    n_units = ntc * upt + (nvis_new + cu - 1) // cu

    def bis_cond(carry):
        return jnp.logical_and(carry[0] < 400, carry[1] > 0)

    def bis_body(carry):
        it, _, lo, hi, clo, chi, tau, act = carry
        half = lo * 0.5 + hi * 0.5
        frac = (jnp.log(clo) - log_k) / (jnp.log(clo) - jnp.log(chi))
        guess = lo + (hi - lo) * frac
        use_guess = jnp.logical_and(it % 2 == 0, jnp.logical_and(guess > lo, guess < hi))
        mid = jnp.where(use_guess, guess, half)

        def cnt_body(u, cnt):
            r0 = pl.multiple_of((u % upt) * cu, cu)
            ind = jnp.where(sc_ref[u // upt, pl.ds(r0, cu), :] >= mid, 1.0, 0.0)
            return cnt + jnp.sum(ind.reshape(cu // CNT_ROWS, CNT_ROWS, tq), axis=0)

        cnt = lax.fori_loop(0, n_units, cnt_body, jnp.zeros((CNT_ROWS, tq), F32))
        c = jnp.sum(cnt, axis=0, keepdims=True)
        found = jnp.where(c == kf, act, 0.0)
        stuck = jnp.where(jnp.logical_or(mid <= lo, mid >= hi), act, 0.0) * (1.0 - found)
        above = jnp.where(c > kf, act, 0.0) * (1.0 - stuck)
        below = act * (1.0 - above) * (1.0 - found) * (1.0 - stuck)
        tau = jnp.where(found > 0, mid, tau)
        tau = jnp.where(stuck > 0, lo, tau)
        lo = jnp.where(above > 0, mid, lo)
        hi = jnp.where(below > 0, mid, hi)
        clo = jnp.where(above > 0, c, clo)
        chi = jnp.where(below > 0, c, chi)
        nact = act * (1.0 - found) * (1.0 - stuck)
        return it + 1, (jnp.max(nact) > 0).astype(jnp.int32), lo, hi, clo, chi, tau, nact

    init = (jnp.int32(0), (jnp.max(act0) > 0).astype(jnp.int32), mn, mx, nvis.astype(F32),
            jnp.ones((1, tq), F32), mn, act0)
    tau = lax.while_loop(bis_cond, bis_body, init)[6]

    acc_ref[...] = jnp.zeros(acc_ref.shape, F32)

    def attend(t, k_of_g, vt_of_g, w, ms):
        s_idx = sc_ref[t]
        if w < tk:
            s_idx = s_idx[0:w]
        bias = jnp.where(s_idx >= tau, 0.0, NEG_INF)
        bias4 = jnp.concatenate([bias] * GROUP, axis=1)
        def qk(g):
            s = _dot(k_of_g(g), qt_ref[g]) + bias4
            s_ref[g % 2, 0:w] = s
            return _col_reduce(jnp.max, s)

        out = []
        tile_max = qk(0)
        for g in range(N_KV_HEADS):
            slot = g % 2
            next_max = qk(g + 1) if g + 1 < N_KV_HEADS else None
            m_old = ms[g]
            m_new = jnp.maximum(m_old, tile_max)
            m_safe = jnp.where(m_new == NEG_INF, 0.0, m_new)
            alpha = jnp.exp2(m_old - m_safe)
            p_ref[slot, 0:w] = jnp.exp2(s_ref[slot, 0:w] - m_safe).astype(BF16)
            acc_ref[g] = acc_ref[g] * alpha + _dot(vt_of_g(g), p_ref[slot, 0:w])
            out.append(m_new)
            tile_max = next_max
        return tuple(out)

    def p3_cache(t, ms):
        r0 = pl.multiple_of(t * tk, tk)

        def vt(g):
            vg = cv_ref[0, pl.ds(r0, tk), g, :]
            return jnp.concatenate([vg.T, jnp.ones((BF16_ROWS, tk), F32)], axis=0).astype(BF16)

        return attend(t, lambda g: ck_ref[0, pl.ds(r0, tk), g, :].astype(BF16), vt, tk, ms)

    def p3_new(j, ms):
        r0 = pl.multiple_of(j * wn, wn)
        return attend(ntc + j,
                      lambda g: kn_ref[0, pl.ds(r0, wn), g * HEAD_DIM:(g + 1) * HEAD_DIM],
                      lambda g: vtn_ref[0, j, g * VT_ROWS:(g + 1) * VT_ROWS, :], wn, ms)

    ms = tuple(jnp.full((1, nq), NEG_INF, F32) for _ in range(N_KV_HEADS))
    if has_cache:
        ms = lax.fori_loop(0, ntc, p3_cache, ms)
    lax.fori_loop(0, ntn, p3_new, ms)

    for g in range(N_KV_HEADS):
        acc = acc_ref[g]
        og = (acc[0:HEAD_DIM] / acc[HEAD_DIM:HEAD_DIM + 1]).T
        for a in range(GROUP):
            hh = GROUP * g + a
            o_ref[0, :, hh * HEAD_DIM:(hh + 1) * HEAD_DIM] = og[a * tq:(a + 1) * tq, :].astype(o_ref.dtype)


def _dsa(qi, wi, q, kin, kn, vtn, cache, pos0, tq, tk):
    b, t, _ = q.shape
    has_cache = cache is not None
    l_cache = cache[0].shape[1] if has_cache else 0
    topk = min(TOPK_MAX, (l_cache + t) // 4)
    wn = min(tk, t)
    ntn = t // wn
    nt = l_cache // tk + ntn
    kvd = N_KV_HEADS * HEAD_DIM
    nq = GROUP * tq
    res = pl.Buffered(1) if b == 1 else None
    in_specs = [pl.BlockSpec((1, tq, IDX_HEADS * IDX_DIM), lambda bb, i: (bb, i, 0)),
                pl.BlockSpec((1, tq, LANES), lambda bb, i: (bb, i, 0)),
                pl.BlockSpec((1, tq, N_HEADS * HEAD_DIM), lambda bb, i: (bb, i, 0)),
                pl.BlockSpec((1, t, IDX_DIM), lambda bb, i: (bb, 0, 0), pipeline_mode=res),
                pl.BlockSpec((1, t, kvd), lambda bb, i: (bb, 0, 0), pipeline_mode=res),
                pl.BlockSpec((1, ntn, N_KV_HEADS * VT_ROWS, wn), lambda bb, i: (bb, 0, 0, 0), pipeline_mode=res)]
    args = [qi, wi, q, kin, kn, vtn]
    if has_cache:
        in_specs += [pl.BlockSpec((1, l_cache, IDX_DIM), lambda bb, i: (bb, 0, 0)),
                     pl.BlockSpec((1, l_cache, N_KV_HEADS, HEAD_DIM), lambda bb, i: (bb, 0, 0, 0)),
                     pl.BlockSpec((1, l_cache, N_KV_HEADS, HEAD_DIM), lambda bb, i: (bb, 0, 0, 0))]
        args += list(cache)
    kern = functools.partial(_dsa_kernel, tq=tq, tk=tk, t_new=t, l_cache=l_cache, pos0=pos0, topk=topk,
                             has_cache=has_cache)
    return pl.pallas_call(
        kern,
        out_shape=jax.ShapeDtypeStruct((b, t, N_HEADS * HEAD_DIM), BF16),
        grid=(b, t // tq),
        in_specs=in_specs,
        out_specs=pl.BlockSpec((1, tq, N_HEADS * HEAD_DIM), lambda bb, i: (bb, i, 0)),
        scratch_shapes=[pltpu.VMEM((nt, tk, tq), F32),
                        pltpu.VMEM((IDX_DIM, IDX_HEADS * tq), BF16),
                        pltpu.VMEM((LANES, tq), F32),
                        pltpu.VMEM((N_KV_HEADS, HEAD_DIM, nq), BF16),
                        pltpu.VMEM((N_KV_HEADS, VT_ROWS, nq), F32),
                        pltpu.VMEM((2, tk, nq), F32),
                        pltpu.VMEM((2, tk, nq), BF16)],
        compiler_params=_cparams(("arbitrary", "arbitrary")),
        name="dsa",
    )(*args)


def _hgrn_kernel(*refs, c, nchunk, nh, has_state):
    if has_state:
        q_ref, lf_ref, kk_ref, v_ref, hg_ref, nw_ref, s0_ref, o_ref, sout_ref, st_ref = refs
    else:
        q_ref, lf_ref, kk_ref, v_ref, hg_ref, nw_ref, o_ref, sout_ref, st_ref = refs
        s0_ref = None
    r = pl.program_id(2)

    @pl.when(r == 0)
    def _():
        for hh in range(nh):
            if has_state:
                st_ref[hh] = s0_ref[0, hh].T
            else:
                st_ref[hh] = jnp.zeros((HG_DV, HG_DK), F32)

    ri = lax.broadcasted_iota(jnp.int32, (c, c), 0)
    ci = lax.broadcasted_iota(jnp.int32, (c, c), 1)
    tri = jnp.where(ri >= ci, 1.0, 0.0).astype(BF16)
    nsub = c // SUB_BLOCK
    pair = lax.broadcasted_iota(jnp.int32, (SUB_BLOCK * SUB_BLOCK, HG_DK), 0)
    cap = jnp.where((pair % SUB_BLOCK) >= (pair // SUB_BLOCK), 0.0, NEG_INF)
    ones_w = jnp.ones((HG_DK, LANES), BF16)

    def rep_rows(x, lo_):
        return jnp.concatenate(
            [jnp.broadcast_to(x[lo_ + sg:lo_ + sg + 1], (SUB_BLOCK, x.shape[1])) for sg in range(SUB_BLOCK)], axis=0)

    def stage_a(base, hh):
        cs = slice(hh * HG_DK, (hh + 1) * HG_DK)
        lf = lf_ref[pl.ds(base, c), cs]
        l1 = lf.astype(BF16)
        r1 = lf - l1.astype(F32)
        l2 = r1.astype(BF16)
        l3 = (r1 - l2.astype(F32)).astype(BF16)
        G = _dot(tri, l1) + _dot(tri, l2) + _dot(tri, l3)
        return dict(cs=cs, G=G, q=q_ref[pl.ds(base, c), cs], kk=kk_ref[pl.ds(base, c), cs],
                    v=v_ref[pl.ds(base, c), cs])

    def stage_b(hh, d):
        q, kk, v, G = d["q"], d["kk"], d["v"], d["G"]
        st = st_ref[hh]
        vb = v.astype(BF16)
        d["vb"] = vb
        d["o_inter"] = _dot_nt((q * jnp.exp2(G)).astype(BF16), st.astype(BF16))
        d["A"], d["rs"] = [], []
        for i in range(nsub):
            lo_, hi_ = i * SUB_BLOCK, (i + 1) * SUB_BLOCK
            qi_ = q[lo_:hi_]
            Gi = G[lo_:hi_]
            if i > 0:
                Gb = G[lo_ - 1:lo_]
                qt = qi_ * jnp.exp2(Gi - Gb)
                kt = kk[:lo_] * jnp.exp2(Gb - G[:lo_])
                d["A"].append(_dot_nt(qt.astype(BF16), kt.astype(BF16)))
            qrep = jnp.concatenate([qi_] * SUB_BLOCK, axis=0)
            grep = jnp.concatenate([Gi] * SUB_BLOCK, axis=0)
            D = qrep * rep_rows(kk, lo_) * jnp.exp2(jnp.minimum(grep - rep_rows(G, lo_), cap))
            d["rs"].append(_dot(D.astype(BF16), ones_w))
        Gl = G[c - 1:c]
        kdec = kk * jnp.exp2(Gl - G)
        st_ref[hh] = st * jnp.exp2(Gl) + _dot_tn(vb, kdec.astype(BF16))

    def stage_c(base, d):
        parts = []
        for i in range(nsub):
            lo_, hi_ = i * SUB_BLOCK, (i + 1) * SUB_BLOCK
            oi = d["o_inter"][lo_:hi_]
            if i > 0:
                oi = oi + _dot(d["A"][i - 1].astype(BF16), d["vb"][:lo_])
            contrib = (d["rs"][i] * rep_rows(d["v"], lo_)).reshape(SUB_BLOCK, SUB_BLOCK, HG_DV)
            parts.append(oi + jnp.sum(contrib, axis=0))
        o = jnp.concatenate(parts, axis=0) if nsub > 1 else parts[0]
        on = o * lax.rsqrt(jnp.mean(o * o, axis=1, keepdims=True) + EPS) * nw_ref[...]
        hg = hg_ref[pl.ds(base, c), d["cs"]]
        o_ref[pl.ds(base, c), d["cs"]] = (on * (hg * _sigmoid(hg))).astype(o_ref.dtype)

    def chunk(n, carry):
        base = pl.multiple_of(n * c, c)
        heads = [stage_a(base, hh) for hh in range(nh)]
        for hh, d in enumerate(heads):
            stage_b(hh, d)
        for d in heads:
            stage_c(base, d)
        return carry

    lax.fori_loop(0, nchunk, chunk, 0)

    @pl.when(r == pl.num_programs(2) - 1)
    def _():
        for hh in range(nh):
            sout_ref[0, hh] = st_ref[hh].T


def _hgrn(hq, lf, kk, hv, hg, nw, s0, b, t, rb, nh):
    c = min(CHUNK, t)
    nchunk = rb // c
    nr = t // rb
    has_state = s0 is not None
    blk = pl.BlockSpec((rb, nh * HG_DK), lambda bb, h, r: (bb * nr + r, h))
    sblk = pl.BlockSpec((1, nh, HG_DK, HG_DV), lambda bb, h, r: (bb, h, 0, 0))
    in_specs = [blk, blk, blk, blk, blk, pl.BlockSpec((1, HG_DV), lambda bb, h, r: (0, 0))]
    args = [hq, lf, kk, hv, hg, nw]
    if has_state:
        in_specs.append(sblk)
        args.append(s0)
    kern = functools.partial(_hgrn_kernel, c=c, nchunk=nchunk, nh=nh, has_state=has_state)
    return pl.pallas_call(
        kern,
        out_shape=(jax.ShapeDtypeStruct((b * t, HG_HEADS * HG_DV), BF16),
                   jax.ShapeDtypeStruct((b, HG_HEADS, HG_DK, HG_DV), F32)),
        grid=(b, HG_HEADS // nh, nr),
        in_specs=in_specs,
        out_specs=(blk, sblk),
        scratch_shapes=[pltpu.VMEM((nh, HG_DV, HG_DK), F32)],
        compiler_params=_cparams(("arbitrary", "arbitrary", "arbitrary")),
        name="hgrn",
    )(*args)


def _merge_out_kernel(x_ref, oa_ref, oh_ref, ga_ref, gb_ref, w_ref, g1_ref, sc_ref, sh_ref, nw_ref,
                      x1_ref, h2_ref):
    bb, tb, d = x_ref.shape
    merged = (_sigmoid(ga_ref[...].astype(F32)) * oa_ref[...].astype(F32)
              + _sigmoid(gb_ref[...].astype(F32)) * oh_ref[...].astype(F32))
    y = _dot(merged.astype(BF16), w_ref[...]).reshape(bb, tb, d)
    x1 = x_ref[...] + g1_ref[...] * y
    x1_ref[...] = x1
    ms = jnp.mean(x1 * x1, axis=-1, keepdims=True)
    xn = x1 * lax.rsqrt(ms + EPS) * nw_ref[...]
    h2 = xn * (1.0 + sc_ref[...]) + sh_ref[...]
    h2_ref[...] = h2.reshape(bb * tb, d).astype(BF16)


def _merge_out(x, oa, oh, ga, gb, w_out, g1, sc2, sh2, nw2, bb, tb):
    b, t, d = x.shape
    tm = bb * tb
    nt = t // tb

    def row(i, j):
        return (i * nt + j, 0)

    def mod(i, j):
        return (i, 0, 0)

    x3 = pl.BlockSpec((bb, tb, d), lambda i, j: (i, j, 0))
    r2 = pl.BlockSpec((tm, d), row)
    return pl.pallas_call(
        _merge_out_kernel,
        out_shape=(jax.ShapeDtypeStruct((b, t, d), F32), jax.ShapeDtypeStruct((b * t, d), BF16)),
        grid=(b // bb, nt),
        in_specs=[x3, r2, r2, r2, r2,
                  pl.BlockSpec((d, d), lambda i, j: (0, 0)),
                  pl.BlockSpec((bb, 1, d), mod), pl.BlockSpec((bb, 1, d), mod), pl.BlockSpec((bb, 1, d), mod),
                  pl.BlockSpec((1, 1, d), lambda i, j: (0, 0, 0))],
        out_specs=(x3, r2),
        compiler_params=_cparams(("arbitrary", "arbitrary")),
        name="merge_out",
    )(x, oa, oh, ga, gb, w_out, g1, sc2, sh2, nw2)


def _mlp_kernel(h_ref, wu_ref, wd_ref, x1_ref, g2_ref, fw_ref, y_ref, acc_ref):
    f = pl.program_id(2)

    @pl.when(f == 0)
    def _():
        acc_ref[...] = jnp.zeros(acc_ref.shape, F32)

    u = jnp.maximum(_dot(h_ref[...], wu_ref[...]), 0.0)
    acc_ref[...] += _dot((u * u).astype(BF16), wd_ref[...])

    @pl.when(f == pl.num_programs(2) - 1)
    def _():
        bb, tb, d = x1_ref.shape
        x2 = x1_ref[...] + g2_ref[...] * acc_ref[...].reshape(bb, tb, d)
        ms = jnp.mean(x2 * x2, axis=-1, keepdims=True)
        y_ref[...] = x2 * lax.rsqrt(ms + EPS) * fw_ref[...]


def _mlp(h2, w_up, w_down, x1, g2, fw, bb, tb, tf):
    b, t, d = x1.shape
    dff = w_up.shape[1]
    tm = bb * tb
    nt = t // tb
    x3 = pl.BlockSpec((bb, tb, d), lambda i, j, f: (i, j, 0))
    return pl.pallas_call(
        _mlp_kernel,
        out_shape=jax.ShapeDtypeStruct((b, t, d), F32),
        grid=(b // bb, nt, dff // tf),
        in_specs=[pl.BlockSpec((tm, d), lambda i, j, f: (i * nt + j, 0)),
                  pl.BlockSpec((d, tf), lambda i, j, f: (0, f)),
                  pl.BlockSpec((tf, d), lambda i, j, f: (f, 0)),
                  x3,
                  pl.BlockSpec((bb, 1, d), lambda i, j, f: (i, 0, 0)),
                  pl.BlockSpec((1, 1, d), lambda i, j, f: (0, 0, 0))],
        out_specs=x3,
        scratch_shapes=[pltpu.VMEM((tm, d), F32)],
        compiler_params=_cparams(("arbitrary", "arbitrary", "arbitrary")),
        name="mlp",
    )(h2, w_up, w_down, x1, g2, fw)


def _rope_tables(pos):
    half = ROT_DIM // 2
    inv_freq = ROPE_THETA ** (-(jnp.arange(half, dtype=F32) * (2.0 / ROT_DIM)))
    ang = pos.astype(F32)[:, None] * inv_freq[None, :]
    cos, sin = jnp.cos(ang), jnp.sin(ang)
    n = pos.shape[0]
    ones = jnp.ones((n, LANES - ROT_DIM), F32)
    zeros = jnp.zeros((n, LANES - ROT_DIM), F32)
    zh = jnp.zeros((n, half), F32)
    c_t = jnp.concatenate([cos, cos, ones], axis=1)
    s_up = jnp.concatenate([-sin, zh, zeros], axis=1)
    s_dn = jnp.concatenate([zh, sin, zeros], axis=1)
    return c_t, s_up, s_dn


def _trunk(x, mod, pos0, past, wts, blocks):
    (norm1_w, w_parts, lb_logits, hg_norm_w, w_out, norm2_w, w_up, w_down, final_w) = wts
    b, t, d = x.shape
    bb, tb, tm_proj, tq, tk, rb, mlp_bb, mlp_tb = blocks
    m = [mod[:, i:i + 1, :] for i in range(6)]
    sh1, sc1, g1, sh2, sc2, g2 = m
    h = _normmod(x, sc1, sh1, norm1_w.reshape(1, 1, d), bb, tb).reshape(b * t, d)

    pos = pos0 + jnp.arange(t, dtype=jnp.int32)
    tabs = tuple(jnp.tile(tb_, (b, 1)) for tb_ in _rope_tables(pos))
    wq, wk, wv, wqi, wki, wwi, whq, whf, whi, whg, wga, wgb = w_parts
    tm = tm_proj
    wn = min(tk, t)
    (q_bf,) = _proj("rope", h, wq, tm, PROJ_TN, tabs, (BF16,), scale=HEAD_DIM ** -0.5 * LOG2E)
    k_f, k_bf = _proj("rope", h, wk, tm, PROJ_TN, tabs, (F32, BF16))
    v_f, vt_bf = _proj_v(h, wv, tm, wn)
    (qi_bf,) = _proj("rope", h, wqi, tm, PROJ_TN, tabs, (BF16,))
    ki_f, ki_bf = _proj("rope", h, wki, tm, LANES, tabs, (F32, BF16))
    (wi_f,) = _proj("plain", h, wwi, tm, LANES, (), (F32,), scale=IDX_HEADS ** -0.5 * IDX_DIM ** -0.5)
    (hq,) = _proj("plain", h, whq, tm, PROJ_TN)
    lf, kk = _proj("forget", h, whf, tm, PROJ_TN, (lb_logits,), (F32, F32))
    (hi,) = _proj("plain", h, whi, tm, PROJ_TN)
    (hg,) = _proj("plain", h, whg, tm, PROJ_TN)
    (ga,) = _proj("plain", h, wga, tm, PROJ_TN, (), (BF16,))
    (gb,) = _proj("plain", h, wgb, tm, PROJ_TN, (), (BF16,))

    kvd = N_KV_HEADS * HEAD_DIM
    r3 = lambda a: a.reshape(b, t, a.shape[-1])
    if past is None:
        cache, s0 = None, None
    else:
        ck, cv, cki, s0 = past
        lc = ck.shape[1]
        cache = (cki, ck, cv)
    vtn = vt_bf.reshape(b, t // wn, N_KV_HEADS * VT_ROWS, wn)
    o_attn = _dsa(r3(qi_bf), r3(wi_f), r3(q_bf), r3(ki_bf), r3(k_bf), vtn, cache, pos0, tq, tk)
    o_hg, s_new = _hgrn(hq, lf, kk, hi, hg, hg_norm_w.reshape(1, HG_DV), s0, b, t, rb, 8)

    x1, h2 = _merge_out(x, o_attn.reshape(b * t, d), o_hg, ga, gb, w_out, g1, sc2, sh2,
                        norm2_w.reshape(1, 1, d), bb, tb)
    y = _mlp(h2, w_up, w_down, x1, g2, final_w.reshape(1, 1, d), mlp_bb, mlp_tb, 512)
    return (y, k_f.reshape(b, t, N_KV_HEADS, HEAD_DIM), v_f.reshape(b, t, N_KV_HEADS, HEAD_DIM),
            ki_f.reshape(b, t, IDX_DIM), s_new)


def kernel(x_prompt, x_sample, cache_k, cache_v, cache_ki, state_hgrn, c_prompt, c_sample, w_ada, b_ada, norm1_w,
           w_in, hg_lb_logits, hg_norm_w, w_out, norm2_w, w_up, w_down, final_norm_w):
    depth = w_in.shape[0]
    assert depth == 1
    d = x_prompt.shape[-1]
    bp, tp, _ = x_prompt.shape
    bs, ts, _ = x_sample.shape
    past_len = cache_k.shape[2]

    c_all = jnp.concatenate([c_prompt, c_sample], axis=0)
    nrow = c_all.shape[0]
    pad = (-nrow) % SUBLANES
    c_all = jnp.pad(c_all, ((0, pad), (0, 0)))
    mod = _ada(c_all, w_ada[0], b_ada[0].reshape(1, -1)).reshape(nrow + pad, 6, d)

    sizes = (N_HEADS * HEAD_DIM, N_KV_HEADS * HEAD_DIM, N_KV_HEADS * HEAD_DIM, IDX_HEADS * IDX_DIM, IDX_DIM,
             IDX_HEADS, HG_HEADS * HG_DK, HG_HEADS * HG_DK, HG_HEADS * HG_DV, HG_HEADS * HG_DV, d, d)
    offs = np.concatenate([[0], np.cumsum(sizes)])
    narrow_end = int(offs[6])
    pad_cols = (-narrow_end) % PROJ_TN
    w_all = _repack_w_in(w_in, narrow_end)
    w_parts = []
    for i, sz in enumerate(sizes):
        c0 = int(offs[i]) + (pad_cols if i >= 6 else 0)
        w_parts.append((w_all, c0, sz + (-sz) % LANES))
    wts = (norm1_w[0], tuple(w_parts), hg_lb_logits.astype(F32), hg_norm_w[0], w_out[0].astype(BF16), norm2_w[0],
           w_up[0].astype(BF16), w_down[0].astype(BF16), final_norm_w)

    yp, kp, vp, kip, sp = _trunk(x_prompt, mod[:bp], 0, None, wts, (1, 256, 1024, 128, 1024, 512, 1, 512))
    past = (cache_k[0], cache_v[0], cache_ki[0], state_hgrn[0])
    ys, ks, vs, kis, ss = _trunk(x_sample, mod[bp:bp + bs], past_len, past, wts,
                                 (bs // 2, ts, bs * ts, ts, 512, ts, bs, ts))
    return (yp, ys, kp[None], vp[None], kip[None], sp[None], ks[None], vs[None], kis[None], ss[None])
```

```python
import functools
import math

import jax
import jax.numpy as jnp
import numpy as np
from jax import lax
from jax.experimental import pallas as pl
from jax.experimental.pallas import tpu as pltpu

CHUNK = 64
N_HEADS = 16
HEAD_DIM = 128
N_KV_HEADS = 4
GROUP = N_HEADS // N_KV_HEADS
ROT_DIM = HEAD_DIM // 4
ROPE_THETA = 500000.0
IDX_HEADS = 16
IDX_DIM = 128
TOPK_MAX = 256
HG_HEADS = 16
HG_DK = 128
HG_DV = 128
EPS = 1e-6
LANES = 128
SUBLANES = 8
BF16_ROWS = 16
SUB_BLOCK = 8
PROJ_TN = 512
COUNT_UNIT = 512
CNT_ROWS = 64
VT_ROWS = HEAD_DIM + BF16_ROWS
VMEM_LIMIT = 56 * 1024 * 1024
NEG_INF = float("-inf")
POS_INF = float("inf")
LOG2E = math.log2(math.e)

F32 = jnp.float32
BF16 = jnp.bfloat16


def _cparams(sem):
    return pltpu.CompilerParams(dimension_semantics=sem, vmem_limit_bytes=VMEM_LIMIT)


def _dot_nt(a, b):
    return lax.dot_general(a, b, (((1,), (1,)), ((), ())), preferred_element_type=F32)


def _dot_tn(a, b):
    return lax.dot_general(a, b, (((0,), (0,)), ((), ())), preferred_element_type=F32)


def _dot(a, b):
    return jnp.dot(a, b, preferred_element_type=F32)


def _sigmoid(x):
    return 1.0 / (1.0 + jnp.exp(-x))


def _col_reduce(op, x):
    rows, n = x.shape
    if rows > CNT_ROWS and rows % CNT_ROWS == 0:
        x = op(x.reshape(rows // CNT_ROWS, CNT_ROWS, n), axis=0)
    return op(x, axis=0, keepdims=True)


def _ada_kernel(c_ref, w_ref, b_ref, o_ref):
    o_ref[...] = _dot(c_ref[...].astype(BF16), w_ref[...].astype(BF16)) + b_ref[...]


def _ada(c_all, w_ada, b_ada, tn=1024):
    r, d = c_all.shape
    n = w_ada.shape[1]
    return pl.pallas_call(
        _ada_kernel,
        out_shape=jax.ShapeDtypeStruct((r, n), F32),
        grid=(n // tn,),
        in_specs=[pl.BlockSpec((r, d), lambda j: (0, 0)),
                  pl.BlockSpec((d, tn), lambda j: (0, j)),
                  pl.BlockSpec((1, tn), lambda j: (0, j))],
        out_specs=pl.BlockSpec((r, tn), lambda j: (0, j)),
        compiler_params=_cparams(("arbitrary",)),
        name="ada",
    )(c_all, w_ada, b_ada)


def _normmod_kernel(x_ref, sc_ref, sh_ref, g_ref, o_ref):
    x = x_ref[...]
    ms = jnp.mean(x * x, axis=-1, keepdims=True)
    xn = x * lax.rsqrt(ms + EPS) * g_ref[...]
    o_ref[...] = (xn * (1.0 + sc_ref[...]) + sh_ref[...]).astype(o_ref.dtype)


def _normmod(x, sc, sh, g, bb, tb):
    b, t, d = x.shape
    return pl.pallas_call(
        _normmod_kernel,
        out_shape=jax.ShapeDtypeStruct((b, t, d), BF16),
        grid=(b // bb, t // tb),
        in_specs=[pl.BlockSpec((bb, tb, d), lambda i, j: (i, j, 0)),
                  pl.BlockSpec((bb, 1, d), lambda i, j: (i, 0, 0)),
                  pl.BlockSpec((bb, 1, d), lambda i, j: (i, 0, 0)),
                  pl.BlockSpec((1, 1, d), lambda i, j: (0, 0, 0))],
        out_specs=pl.BlockSpec((bb, tb, d), lambda i, j: (i, j, 0)),
        compiler_params=_cparams(("arbitrary", "arbitrary")),
        name="normmod",
    )(x, sc, sh, g)


def _proj_plain_kernel(h_ref, w_ref, *o_refs, scale):
    z = _dot_nt(h_ref[...], w_ref[...].astype(BF16))
    if scale != 1.0:
        z = z * scale
    for o_ref in o_refs:
        o_ref[...] = z.astype(o_ref.dtype)


def _proj_rope_kernel(h_ref, w_ref, cos_ref, sup_ref, sdn_ref, *o_refs, scale):
    z = _dot_nt(h_ref[...], w_ref[...].astype(BF16))
    tn = z.shape[1]
    reps = tn // LANES

    def wide(ref):
        t = ref[...]
        return t if reps == 1 else jnp.concatenate([t] * reps, axis=1)

    up = pltpu.roll(z, tn - ROT_DIM // 2, 1)
    dn = pltpu.roll(z, ROT_DIM // 2, 1)
    r = z * wide(cos_ref) + up * wide(sup_ref) + dn * wide(sdn_ref)
    for o_ref in o_refs:
        if o_ref.dtype == BF16 and scale != 1.0:
            o_ref[...] = (r * scale).astype(BF16)
        else:
            o_ref[...] = r.astype(o_ref.dtype)


def _proj_forget_kernel(h_ref, w_ref, lbl_ref, lf_ref, kk_ref):
    z = _dot_nt(h_ref[...], w_ref[...].astype(BF16))
    lbl = lbl_ref[...]
    mx = jnp.max(lbl, axis=0, keepdims=True)
    e = jnp.exp(lbl - mx)
    lb = e[0:1, :] / jnp.sum(e, axis=0, keepdims=True)
    f = lb + (1.0 - lb) * _sigmoid(z)
    lf_ref[...] = jnp.log(f) * LOG2E
    kk_ref[...] = 1.0 - f


def _proj(kind, h, wspec, tm, tn, extra=(), out_dtypes=(F32,), scale=1.0):
    w, c0, n = wspec
    m, k = h.shape
    assert c0 % SUBLANES == 0 and n % tn == 0
    in_specs = [pl.BlockSpec((tm, k), lambda i, j: (i, 0)),
                pl.BlockSpec((pl.Element(tn), pl.Element(k)),
                             lambda i, j: (pl.multiple_of(c0 + j * tn, SUBLANES), 0))]
    if kind == "rope":
        kern = functools.partial(_proj_rope_kernel, scale=scale)
        in_specs += [pl.BlockSpec((tm, LANES), lambda i, j: (i, 0))] * 3
    elif kind == "forget":
        kern = _proj_forget_kernel
        in_specs += [pl.BlockSpec((extra[0].shape[0], tn), lambda i, j: (0, j))]
    else:
        kern = functools.partial(_proj_plain_kernel, scale=scale)
    outs = tuple(jax.ShapeDtypeStruct((m, n), dt) for dt in out_dtypes)
    out_specs = tuple(pl.BlockSpec((tm, tn), lambda i, j: (i, j)) for _ in out_dtypes)
    res = pl.pallas_call(
        kern,
        out_shape=outs,
        grid=(m // tm, n // tn),
        in_specs=in_specs,
        out_specs=out_specs,
        compiler_params=_cparams(("arbitrary", "arbitrary")),
        name="proj_" + kind,
    )(h, w, *extra)
    return res


def _proj_v_kernel(h_ref, w_ref, vf_ref, vt_ref, *, wn):
    z = _dot_nt(h_ref[...], w_ref[...].astype(BF16))
    vf_ref[...] = z
    zt = z.T
    ones = jnp.ones((BF16_ROWS, wn), BF16)
    for u in range(z.shape[0] // wn):
        blk = zt[:, u * wn:(u + 1) * wn].astype(BF16)
        for g in range(N_KV_HEADS):
            vt_ref[u, g * VT_ROWS:g * VT_ROWS + HEAD_DIM, :] = blk[g * HEAD_DIM:(g + 1) * HEAD_DIM]
            vt_ref[u, g * VT_ROWS + HEAD_DIM:(g + 1) * VT_ROWS, :] = ones


def _proj_v(h, wspec, tm, wn):
    w, c0, n = wspec
    m, k = h.shape
    assert c0 % SUBLANES == 0
    return pl.pallas_call(
        functools.partial(_proj_v_kernel, wn=wn),
        out_shape=(jax.ShapeDtypeStruct((m, n), F32),
                   jax.ShapeDtypeStruct((m // wn, N_KV_HEADS * VT_ROWS, wn), BF16)),
        grid=(m // tm,),
        in_specs=[pl.BlockSpec((tm, k), lambda i: (i, 0)),
                  pl.BlockSpec((pl.Element(n), pl.Element(k)), lambda i: (c0, 0))],
        out_specs=(pl.BlockSpec((tm, n), lambda i: (i, 0)),
                   pl.BlockSpec((tm // wn, N_KV_HEADS * VT_ROWS, wn), lambda i: (i, 0, 0))),
        compiler_params=_cparams(("arbitrary",)),
        name="proj_v",
    )(h, w)


def _dsa_kernel(*refs, tq, tk, t_new, l_cache, pos0, topk, has_cache):
    if has_cache:
        (qi_ref, wi_ref, q_ref, kin_ref, kn_ref, vtn_ref, cki_ref, ck_ref, cv_ref, o_ref,
         sc_ref, qit_ref, wt_ref, qt_ref, acc_ref, s_ref, p_ref, tau_ref) = refs
    else:
        (qi_ref, wi_ref, q_ref, kin_ref, kn_ref, vtn_ref, o_ref,
         sc_ref, qit_ref, wt_ref, qt_ref, acc_ref, s_ref, p_ref, tau_ref) = refs
        cki_ref = ck_ref = cv_ref = None
    i = pl.program_id(1)
    ntc = l_cache // tk
    wn = min(tk, t_new)
    nq = GROUP * tq
    q0 = pos0 + i * tq
    lane_q = lax.broadcasted_iota(jnp.int32, (1, tq), 1)
    qend = (((q0 + lane_q) >> 6) + 1) << 6
    last_end = (((q0 + tq - 1) >> 6) + 1) << 6
    nvis_new = jnp.minimum(last_end - pos0, t_new)
    ntn = (nvis_new + wn - 1) // wn
    ntiles = ntc + ntn

    def tpose(x):
        xf = x.astype(F32)
        if tq < LANES:
            xf = jnp.concatenate([xf, jnp.zeros((LANES - tq, LANES), F32)], axis=0)
        xt = xf.T
        return xt[:, :tq] if tq < LANES else xt

    qi_blk = qi_ref[0]
    for h in range(IDX_HEADS):
        qit_ref[:, h * tq:(h + 1) * tq] = tpose(qi_blk[:, h * IDX_DIM:(h + 1) * IDX_DIM]).astype(BF16)
    wt_ref[...] = tpose(wi_ref[0])
    q_blk = q_ref[0]
    for g in range(N_KV_HEADS):
        for a in range(GROUP):
            hh = GROUP * g + a
            qt_ref[g, :, a * tq:(a + 1) * tq] = tpose(q_blk[:, hh * HEAD_DIM:(hh + 1) * HEAD_DIM]).astype(BF16)

    def score_tile(ki_tile, kpos0, w):
        acc = jnp.zeros((w, tq), F32)
        hpd = 2 if 2 * tq % LANES == 0 else 1
        for hp in range(IDX_HEADS // hpd):
            lg = _dot(ki_tile, qit_ref[:, hpd * hp * tq:(hpd * hp + hpd) * tq])
            for e in range(hpd):
                h = hpd * hp + e
                acc = acc + wt_ref[h:h + 1, :] * jnp.maximum(lg[:, e * tq:(e + 1) * tq], 0.0)
        kpos = kpos0 + lax.broadcasted_iota(jnp.int32, (w, 1), 0)
        vis = kpos < qend
        s = jnp.where(vis, acc, NEG_INF)
        smin = _col_reduce(jnp.min, jnp.where(vis, acc, POS_INF))
        smax = _col_reduce(jnp.max, s)
        return s, smax, smin

    def p1_cache(t, carry):
        mx, mn = carry
        r0 = pl.multiple_of(t * tk, tk)
        s, smax, smin = score_tile(cki_ref[0, pl.ds(r0, tk), :].astype(BF16), r0, tk)
        sc_ref[t] = s
        return jnp.maximum(mx, smax), jnp.minimum(mn, smin)

    def p1_new(j, carry):
        mx, mn = carry
        r0 = pl.multiple_of(j * wn, wn)
        s, smax, smin = score_tile(kin_ref[0, pl.ds(r0, wn), :], pos0 + r0, wn)
        if wn == tk:
            sc_ref[ntc + j] = s
        else:
            sc_ref[ntc + j] = jnp.full((tk, tq), NEG_INF, F32)
            sc_ref[ntc + j, 0:wn, :] = s
        return jnp.maximum(mx, smax), jnp.minimum(mn, smin)

    carry = (jnp.full((1, tq), NEG_INF, F32), jnp.full((1, tq), POS_INF, F32))
    if has_cache:
        carry = lax.fori_loop(0, ntc, p1_cache, carry)
    mx, mn = lax.fori_loop(0, ntn, p1_new, carry)

    nvis = jnp.clip(qend - pos0, 0, t_new)
    if has_cache:
        nvis = nvis + jnp.minimum(qend, l_cache)
    act0 = jnp.where(nvis > topk, 1.0, 0.0)
    kf = float(topk)
    cu = min(tk, COUNT_UNIT)
    upt = tk // cu
    n_units = ntc * upt + (nvis_new + cu - 1) // cu

    def bis_cond(carry):
        return jnp.logical_and(carry[0] < 400, carry[1] > 0)

    def unit(u):
        return sc_ref.at[u // upt, pl.ds(pl.multiple_of((u % upt) * cu, cu), cu), :]

    def count_where(pred):
        def body(u, cnt):
            ind = jnp.where(pred(unit(u)[...]), 1.0, 0.0)
            return cnt + jnp.sum(ind.reshape(cu // CNT_ROWS, CNT_ROWS, tq), axis=0)

        cnt = lax.fori_loop(0, n_units, body, jnp.zeros((CNT_ROWS, tq), F32))
        return jnp.sum(cnt, axis=0, keepdims=True)

    def bis_body(carry):
        it, _, lo, hi, tau, act, tied = carry
        mid = lo * 0.5 + hi * 0.5
        c = count_where(lambda s: s >= mid)
        found = jnp.where(c == kf, act, 0.0)
        stuck = jnp.where(jnp.logical_or(mid <= lo, mid >= hi), act, 0.0) * (1.0 - found)
        above = jnp.where(c > kf, act, 0.0) * (1.0 - stuck)
        below = act * (1.0 - above) * (1.0 - found) * (1.0 - stuck)
        tau = jnp.where(found > 0, mid, tau)
        tau = jnp.where(stuck > 0, lo, tau)
        lo = jnp.where(above > 0, mid, lo)
        hi = jnp.where(below > 0, mid, hi)
        nact = act * (1.0 - found) * (1.0 - stuck)
        return it + 1, (jnp.max(nact) > 0).astype(jnp.int32), lo, hi, tau, nact, jnp.maximum(tied, stuck)

    init = (jnp.int32(0), (jnp.max(act0) > 0).astype(jnp.int32), mn, mx, mn, act0, jnp.zeros((1, tq), F32))
    res = lax.while_loop(bis_cond, bis_body, init)
    tau_ref[...] = res[4]
    tied = res[6]

    @pl.when(jnp.max(tied) > 0)
    def _():
        lo_t = tau_ref[...]

        def tmin_body(u, acc):
            s = unit(u)[...]
            return jnp.minimum(acc, _col_reduce(jnp.min, jnp.where(s >= lo_t, s, POS_INF)))

        tstar = lax.fori_loop(0, n_units, tmin_body, jnp.full((1, tq), POS_INF, F32))
        tstar = jnp.where(tied > 0, tstar, lo_t)
        need = kf - count_where(lambda s: s > tstar)
        ri = lax.broadcasted_iota(jnp.int32, (cu, cu), 0)
        ci = lax.broadcasted_iota(jnp.int32, (cu, cu), 1)
        tri = jnp.where(ri >= ci, 1.0, 0.0).astype(BF16)

        def strike_body(u, run):
            s = unit(u)[...]
            eq = jnp.where(jnp.logical_and(s == tstar, tied > 0), 1.0, 0.0)
            rank = run + _dot(tri, eq.astype(BF16))
            unit(u)[...] = jnp.where(jnp.logical_and(eq > 0, rank > need), NEG_INF, s)
            return rank[cu - 1:cu, :]

        lax.fori_loop(0, n_units, strike_body, jnp.zeros((1, tq), F32))
        tau_ref[...] = tstar

    tau = tau_ref[...]

    acc_ref[...] = jnp.zeros(acc_ref.shape, F32)

    def attend(t, k_of_g, vt_of_g, w, ms):
        s_idx = sc_ref[t]
        if w < tk:
            s_idx = s_idx[0:w]
        bias = jnp.where(s_idx >= tau, 0.0, NEG_INF)
        bias4 = jnp.concatenate([bias] * GROUP, axis=1)
        def qk(g):
            s = _dot(k_of_g(g), qt_ref[g]) + bias4
            s_ref[g % 2, 0:w] = s
            return _col_reduce(jnp.max, s)

        out = []
        tile_max = qk(0)
        for g in range(N_KV_HEADS):
            slot = g % 2
            next_max = qk(g + 1) if g + 1 < N_KV_HEADS else None
            m_old = ms[g]
            m_new = jnp.maximum(m_old, tile_max)
            m_safe = jnp.where(m_new == NEG_INF, 0.0, m_new)
            alpha = jnp.exp2(m_old - m_safe)
            p_ref[slot, 0:w] = jnp.exp2(s_ref[slot, 0:w] - m_safe).astype(BF16)
            acc_ref[g] = acc_ref[g] * alpha + _dot(vt_of_g(g), p_ref[slot, 0:w])
            out.append(m_new)
            tile_max = next_max
        return tuple(out)

    def p3_cache(t, ms):
        r0 = pl.multiple_of(t * tk, tk)

        def vt(g):
            vg = cv_ref[0, pl.ds(r0, tk), g, :]
            return jnp.concatenate([vg.T, jnp.ones((BF16_ROWS, tk), F32)], axis=0).astype(BF16)

        return attend(t, lambda g: ck_ref[0, pl.ds(r0, tk), g, :].astype(BF16), vt, tk, ms)

    def p3_new(j, ms):
        r0 = pl.multiple_of(j * wn, wn)
        return attend(ntc + j,
                      lambda g: kn_ref[0, pl.ds(r0, wn), g * HEAD_DIM:(g + 1) * HEAD_DIM],
                      lambda g: vtn_ref[0, j, g * VT_ROWS:(g + 1) * VT_ROWS, :], wn, ms)

    ms = tuple(jnp.full((1, nq), NEG_INF, F32) for _ in range(N_KV_HEADS))
    if has_cache:
        ms = lax.fori_loop(0, ntc, p3_cache, ms)
    lax.fori_loop(0, ntn, p3_new, ms)

    for g in range(N_KV_HEADS):
        acc = acc_ref[g]
        og = (acc[0:HEAD_DIM] / acc[HEAD_DIM:HEAD_DIM + 1]).T
        for a in range(GROUP):
            hh = GROUP * g + a
            o_ref[0, :, hh * HEAD_DIM:(hh + 1) * HEAD_DIM] = og[a * tq:(a + 1) * tq, :].astype(o_ref.dtype)


def _dsa(qi, wi, q, kin, kn, vtn, cache, pos0, tq, tk):
    b, t, _ = q.shape
    has_cache = cache is not None
    l_cache = cache[0].shape[1] if has_cache else 0
    topk = min(TOPK_MAX, (l_cache + t) // 4)
    wn = min(tk, t)
    ntn = t // wn
    nt = l_cache // tk + ntn
    kvd = N_KV_HEADS * HEAD_DIM
    nq = GROUP * tq
    res = pl.Buffered(1) if b == 1 else None
    in_specs = [pl.BlockSpec((1, tq, IDX_HEADS * IDX_DIM), lambda bb, i: (bb, i, 0)),
                pl.BlockSpec((1, tq, LANES), lambda bb, i: (bb, i, 0)),
                pl.BlockSpec((1, tq, N_HEADS * HEAD_DIM), lambda bb, i: (bb, i, 0)),
                pl.BlockSpec((1, t, IDX_DIM), lambda bb, i: (bb, 0, 0), pipeline_mode=res),
                pl.BlockSpec((1, t, kvd), lambda bb, i: (bb, 0, 0), pipeline_mode=res),
                pl.BlockSpec((1, ntn, N_KV_HEADS * VT_ROWS, wn), lambda bb, i: (bb, 0, 0, 0), pipeline_mode=res)]
    args = [qi, wi, q, kin, kn, vtn]
    if has_cache:
        in_specs += [pl.BlockSpec((1, l_cache, IDX_DIM), lambda bb, i: (bb, 0, 0)),
                     pl.BlockSpec((1, l_cache, N_KV_HEADS, HEAD_DIM), lambda bb, i: (bb, 0, 0, 0)),
                     pl.BlockSpec((1, l_cache, N_KV_HEADS, HEAD_DIM), lambda bb, i: (bb, 0, 0, 0))]
        args += list(cache)
    kern = functools.partial(_dsa_kernel, tq=tq, tk=tk, t_new=t, l_cache=l_cache, pos0=pos0, topk=topk,
                             has_cache=has_cache)
    return pl.pallas_call(
        kern,
        out_shape=jax.ShapeDtypeStruct((b, t, N_HEADS * HEAD_DIM), BF16),
        grid=(b, t // tq),
        in_specs=in_specs,
        out_specs=pl.BlockSpec((1, tq, N_HEADS * HEAD_DIM), lambda bb, i: (bb, i, 0)),
        scratch_shapes=[pltpu.VMEM((nt, tk, tq), F32),
                        pltpu.VMEM((IDX_DIM, IDX_HEADS * tq), BF16),
                        pltpu.VMEM((LANES, tq), F32),
                        pltpu.VMEM((N_KV_HEADS, HEAD_DIM, nq), BF16),
                        pltpu.VMEM((N_KV_HEADS, VT_ROWS, nq), F32),
                        pltpu.VMEM((2, tk, nq), F32),
                        pltpu.VMEM((2, tk, nq), BF16),
                        pltpu.VMEM((1, tq), F32)],
        compiler_params=_cparams(("arbitrary", "arbitrary")),
        name="dsa",
    )(*args)


def _hgrn_kernel(*refs, c, nchunk, nh, has_state):
    if has_state:
        q_ref, lf_ref, kk_ref, v_ref, hg_ref, nw_ref, s0_ref, o_ref, sout_ref, st_ref = refs
    else:
        q_ref, lf_ref, kk_ref, v_ref, hg_ref, nw_ref, o_ref, sout_ref, st_ref = refs
        s0_ref = None
    r = pl.program_id(2)

    @pl.when(r == 0)
    def _():
        for hh in range(nh):
            if has_state:
                st_ref[hh] = s0_ref[0, hh].T
            else:
                st_ref[hh] = jnp.zeros((HG_DV, HG_DK), F32)

    ri = lax.broadcasted_iota(jnp.int32, (c, c), 0)
    ci = lax.broadcasted_iota(jnp.int32, (c, c), 1)
    tri = jnp.where(ri >= ci, 1.0, 0.0).astype(BF16)
    nsub = c // SUB_BLOCK
    pair = lax.broadcasted_iota(jnp.int32, (SUB_BLOCK * SUB_BLOCK, HG_DK), 0)
    cap = jnp.where((pair % SUB_BLOCK) >= (pair // SUB_BLOCK), 0.0, NEG_INF)
    ones_w = jnp.ones((HG_DK, LANES), BF16)

    def rep_rows(x, lo_):
        return jnp.concatenate(
            [jnp.broadcast_to(x[lo_ + sg:lo_ + sg + 1], (SUB_BLOCK, x.shape[1])) for sg in range(SUB_BLOCK)], axis=0)

    def stage_a(base, hh):
        cs = slice(hh * HG_DK, (hh + 1) * HG_DK)
        lf = lf_ref[pl.ds(base, c), cs]
        l1 = lf.astype(BF16)
        r1 = lf - l1.astype(F32)
        l2 = r1.astype(BF16)
        l3 = (r1 - l2.astype(F32)).astype(BF16)
        G = _dot(tri, l1) + _dot(tri, l2) + _dot(tri, l3)
        return dict(cs=cs, G=G, q=q_ref[pl.ds(base, c), cs], kk=kk_ref[pl.ds(base, c), cs],
                    v=v_ref[pl.ds(base, c), cs])

    def stage_b(hh, d):
        q, kk, v, G = d["q"], d["kk"], d["v"], d["G"]
        st = st_ref[hh]
        vb = v.astype(BF16)
        d["vb"] = vb
        d["o_inter"] = _dot_nt((q * jnp.exp2(G)).astype(BF16), st.astype(BF16))
        d["A"], d["rs"] = [], []
        for i in range(nsub):
            lo_, hi_ = i * SUB_BLOCK, (i + 1) * SUB_BLOCK
            qi_ = q[lo_:hi_]
            Gi = G[lo_:hi_]
            if i > 0:
                Gb = G[lo_ - 1:lo_]
                qt = qi_ * jnp.exp2(Gi - Gb)
                kt = kk[:lo_] * jnp.exp2(Gb - G[:lo_])
                d["A"].append(_dot_nt(qt.astype(BF16), kt.astype(BF16)))
            qrep = jnp.concatenate([qi_] * SUB_BLOCK, axis=0)
            grep = jnp.concatenate([Gi] * SUB_BLOCK, axis=0)
            D = qrep * rep_rows(kk, lo_) * jnp.exp2(jnp.minimum(grep - rep_rows(G, lo_), cap))
            d["rs"].append(_dot(D.astype(BF16), ones_w))
        Gl = G[c - 1:c]
        kdec = kk * jnp.exp2(Gl - G)
        st_ref[hh] = st * jnp.exp2(Gl) + _dot_tn(vb, kdec.astype(BF16))

    def stage_c(base, d):
        parts = []
        for i in range(nsub):
            lo_, hi_ = i * SUB_BLOCK, (i + 1) * SUB_BLOCK
            oi = d["o_inter"][lo_:hi_]
            if i > 0:
                oi = oi + _dot(d["A"][i - 1].astype(BF16), d["vb"][:lo_])
            contrib = (d["rs"][i] * rep_rows(d["v"], lo_)).reshape(SUB_BLOCK, SUB_BLOCK, HG_DV)
            parts.append(oi + jnp.sum(contrib, axis=0))
        o = jnp.concatenate(parts, axis=0) if nsub > 1 else parts[0]
        on = o * lax.rsqrt(jnp.mean(o * o, axis=1, keepdims=True) + EPS) * nw_ref[...]
        hg = hg_ref[pl.ds(base, c), d["cs"]]
        o_ref[pl.ds(base, c), d["cs"]] = (on * (hg * _sigmoid(hg))).astype(o_ref.dtype)

    def chunk(n, carry):
        base = pl.multiple_of(n * c, c)
        heads = [stage_a(base, hh) for hh in range(nh)]
        for hh, d in enumerate(heads):
            stage_b(hh, d)
        for d in heads:
            stage_c(base, d)
        return carry

    lax.fori_loop(0, nchunk, chunk, 0)

    @pl.when(r == pl.num_programs(2) - 1)
    def _():
        for hh in range(nh):
            sout_ref[0, hh] = st_ref[hh].T


def _hgrn(hq, lf, kk, hv, hg, nw, s0, b, t, rb, nh):
    c = min(CHUNK, t)
    nchunk = rb // c
    nr = t // rb
    has_state = s0 is not None
    blk = pl.BlockSpec((rb, nh * HG_DK), lambda bb, h, r: (bb * nr + r, h))
    sblk = pl.BlockSpec((1, nh, HG_DK, HG_DV), lambda bb, h, r: (bb, h, 0, 0))
    in_specs = [blk, blk, blk, blk, blk, pl.BlockSpec((1, HG_DV), lambda bb, h, r: (0, 0))]
    args = [hq, lf, kk, hv, hg, nw]
    if has_state:
        in_specs.append(sblk)
        args.append(s0)
    kern = functools.partial(_hgrn_kernel, c=c, nchunk=nchunk, nh=nh, has_state=has_state)
    return pl.pallas_call(
        kern,
        out_shape=(jax.ShapeDtypeStruct((b * t, HG_HEADS * HG_DV), BF16),
                   jax.ShapeDtypeStruct((b, HG_HEADS, HG_DK, HG_DV), F32)),
        grid=(b, HG_HEADS // nh, nr),
        in_specs=in_specs,
        out_specs=(blk, sblk),
        scratch_shapes=[pltpu.VMEM((nh, HG_DV, HG_DK), F32)],
        compiler_params=_cparams(("arbitrary", "arbitrary", "arbitrary")),
        name="hgrn",
    )(*args)


def _merge_out_kernel(x_ref, oa_ref, oh_ref, ga_ref, gb_ref, w_ref, g1_ref, sc_ref, sh_ref, nw_ref,
                      x1_ref, h2_ref):
    bb, tb, d = x_ref.shape
    merged = (_sigmoid(ga_ref[...].astype(F32)) * oa_ref[...].astype(F32)
              + _sigmoid(gb_ref[...].astype(F32)) * oh_ref[...].astype(F32))
    y = _dot(merged.astype(BF16), w_ref[...]).reshape(bb, tb, d)
    x1 = x_ref[...] + g1_ref[...] * y
    x1_ref[...] = x1
    ms = jnp.mean(x1 * x1, axis=-1, keepdims=True)
    xn = x1 * lax.rsqrt(ms + EPS) * nw_ref[...]
    h2 = xn * (1.0 + sc_ref[...]) + sh_ref[...]
    h2_ref[...] = h2.reshape(bb * tb, d).astype(BF16)


def _merge_out(x, oa, oh, ga, gb, w_out, g1, sc2, sh2, nw2, bb, tb):
    b, t, d = x.shape
    tm = bb * tb
    nt = t // tb

    def row(i, j):
        return (i * nt + j, 0)

    def mod(i, j):
        return (i, 0, 0)

    x3 = pl.BlockSpec((bb, tb, d), lambda i, j: (i, j, 0))
    r2 = pl.BlockSpec((tm, d), row)
    return pl.pallas_call(
        _merge_out_kernel,
        out_shape=(jax.ShapeDtypeStruct((b, t, d), F32), jax.ShapeDtypeStruct((b * t, d), BF16)),
        grid=(b // bb, nt),
        in_specs=[x3, r2, r2, r2, r2,
                  pl.BlockSpec((d, d), lambda i, j: (0, 0)),
                  pl.BlockSpec((bb, 1, d), mod), pl.BlockSpec((bb, 1, d), mod), pl.BlockSpec((bb, 1, d), mod),
                  pl.BlockSpec((1, 1, d), lambda i, j: (0, 0, 0))],
        out_specs=(x3, r2),
        compiler_params=_cparams(("arbitrary", "arbitrary")),
        name="merge_out",
    )(x, oa, oh, ga, gb, w_out, g1, sc2, sh2, nw2)


def _mlp_kernel(h_ref, wu_ref, wd_ref, x1_ref, g2_ref, fw_ref, y_ref, acc_ref):
    f = pl.program_id(2)

    @pl.when(f == 0)
    def _():
        acc_ref[...] = jnp.zeros(acc_ref.shape, F32)

    u = jnp.maximum(_dot(h_ref[...], wu_ref[...]), 0.0)
    acc_ref[...] += _dot((u * u).astype(BF16), wd_ref[...])

    @pl.when(f == pl.num_programs(2) - 1)
    def _():
        bb, tb, d = x1_ref.shape
        x2 = x1_ref[...] + g2_ref[...] * acc_ref[...].reshape(bb, tb, d)
        ms = jnp.mean(x2 * x2, axis=-1, keepdims=True)
        y_ref[...] = x2 * lax.rsqrt(ms + EPS) * fw_ref[...]


def _mlp(h2, w_up, w_down, x1, g2, fw, bb, tb, tf):
    b, t, d = x1.shape
    dff = w_up.shape[1]
    tm = bb * tb
    nt = t // tb
    x3 = pl.BlockSpec((bb, tb, d), lambda i, j, f: (i, j, 0))
    return pl.pallas_call(
        _mlp_kernel,
        out_shape=jax.ShapeDtypeStruct((b, t, d), F32),
        grid=(b // bb, nt, dff // tf),
        in_specs=[pl.BlockSpec((tm, d), lambda i, j, f: (i * nt + j, 0)),
                  pl.BlockSpec((d, tf), lambda i, j, f: (0, f)),
                  pl.BlockSpec((tf, d), lambda i, j, f: (f, 0)),
                  x3,
                  pl.BlockSpec((bb, 1, d), lambda i, j, f: (i, 0, 0)),
                  pl.BlockSpec((1, 1, d), lambda i, j, f: (0, 0, 0))],
        out_specs=x3,
        scratch_shapes=[pltpu.VMEM((tm, d), F32)],
        compiler_params=_cparams(("arbitrary", "arbitrary", "arbitrary")),
        name="mlp",
    )(h2, w_up, w_down, x1, g2, fw)


def _rope_tables(pos):
    half = ROT_DIM // 2
    inv_freq = ROPE_THETA ** (-(jnp.arange(half, dtype=F32) * (2.0 / ROT_DIM)))
    ang = pos.astype(F32)[:, None] * inv_freq[None, :]
    cos, sin = jnp.cos(ang), jnp.sin(ang)
    n = pos.shape[0]
    ones = jnp.ones((n, LANES - ROT_DIM), F32)
    zeros = jnp.zeros((n, LANES - ROT_DIM), F32)
    zh = jnp.zeros((n, half), F32)
    c_t = jnp.concatenate([cos, cos, ones], axis=1)
    s_up = jnp.concatenate([-sin, zh, zeros], axis=1)
    s_dn = jnp.concatenate([zh, sin, zeros], axis=1)
    return c_t, s_up, s_dn


def _trunk(x, mod, pos0, past, wts, blocks):
    (norm1_w, w_parts, lb_logits, hg_norm_w, w_out, norm2_w, w_up, w_down, final_w) = wts
    b, t, d = x.shape
    bb, tb, tm_proj, tq, tk, rb, mlp_bb, mlp_tb = blocks
    m = [mod[:, i:i + 1, :] for i in range(6)]
    sh1, sc1, g1, sh2, sc2, g2 = m
    h = _normmod(x, sc1, sh1, norm1_w.reshape(1, 1, d), bb, tb).reshape(b * t, d)

    pos = pos0 + jnp.arange(t, dtype=jnp.int32)
    tabs = tuple(jnp.tile(tb_, (b, 1)) for tb_ in _rope_tables(pos))
    wq, wk, wv, wqi, wki, wwi, whq, whf, whi, whg, wga, wgb = w_parts
    tm = tm_proj
    wn = min(tk, t)
    (q_bf,) = _proj("rope", h, wq, tm, PROJ_TN, tabs, (BF16,), scale=HEAD_DIM ** -0.5 * LOG2E)
    k_f, k_bf = _proj("rope", h, wk, tm, PROJ_TN, tabs, (F32, BF16))
    v_f, vt_bf = _proj_v(h, wv, tm, wn)
    (qi_bf,) = _proj("rope", h, wqi, tm, PROJ_TN, tabs, (BF16,))
    ki_f, ki_bf = _proj("rope", h, wki, tm, LANES, tabs, (F32, BF16))
    (wi_f,) = _proj("plain", h, wwi, tm, LANES, (), (F32,), scale=IDX_HEADS ** -0.5 * IDX_DIM ** -0.5)
    (hq,) = _proj("plain", h, whq, tm, PROJ_TN)
    lf, kk = _proj("forget", h, whf, tm, PROJ_TN, (lb_logits,), (F32, F32))
    (hi,) = _proj("plain", h, whi, tm, PROJ_TN)
    (hg,) = _proj("plain", h, whg, tm, PROJ_TN)
    (ga,) = _proj("plain", h, wga, tm, PROJ_TN, (), (BF16,))
    (gb,) = _proj("plain", h, wgb, tm, PROJ_TN, (), (BF16,))

    kvd = N_KV_HEADS * HEAD_DIM
    r3 = lambda a: a.reshape(b, t, a.shape[-1])
    if past is None:
        cache, s0 = None, None
    else:
        ck, cv, cki, s0 = past
        lc = ck.shape[1]
        cache = (cki, ck, cv)
    vtn = vt_bf.reshape(b, t // wn, N_KV_HEADS * VT_ROWS, wn)
    o_attn = _dsa(r3(qi_bf), r3(wi_f), r3(q_bf), r3(ki_bf), r3(k_bf), vtn, cache, pos0, tq, tk)
    o_hg, s_new = _hgrn(hq, lf, kk, hi, hg, hg_norm_w.reshape(1, HG_DV), s0, b, t, rb, 8)

    x1, h2 = _merge_out(x, o_attn.reshape(b * t, d), o_hg, ga, gb, w_out, g1, sc2, sh2,
                        norm2_w.reshape(1, 1, d), bb, tb)
    y = _mlp(h2, w_up, w_down, x1, g2, final_w.reshape(1, 1, d), mlp_bb, mlp_tb, 512)
    return (y, k_f.reshape(b, t, N_KV_HEADS, HEAD_DIM), v_f.reshape(b, t, N_KV_HEADS, HEAD_DIM),
            ki_f.reshape(b, t, IDX_DIM), s_new)


def kernel(x_prompt, x_sample, cache_k, cache_v, cache_ki, state_hgrn, c_prompt, c_sample, w_ada, b_ada, norm1_w,
           w_in, hg_lb_logits, hg_norm_w, w_out, norm2_w, w_up, w_down, final_norm_w):
    depth = w_in.shape[0]
    assert depth == 1
    d = x_prompt.shape[-1]
    bp, tp, _ = x_prompt.shape
    bs, ts, _ = x_sample.shape
    past_len = cache_k.shape[2]

    c_all = jnp.concatenate([c_prompt, c_sample], axis=0)
    nrow = c_all.shape[0]
    pad = (-nrow) % SUBLANES
    c_all = jnp.pad(c_all, ((0, pad), (0, 0)))
    mod = _ada(c_all, w_ada[0], b_ada[0].reshape(1, -1)).reshape(nrow + pad, 6, d)

    sizes = (N_HEADS * HEAD_DIM, N_KV_HEADS * HEAD_DIM, N_KV_HEADS * HEAD_DIM, IDX_HEADS * IDX_DIM, IDX_DIM,
             IDX_HEADS, HG_HEADS * HG_DK, HG_HEADS * HG_DK, HG_HEADS * HG_DV, HG_HEADS * HG_DV, d, d)
    offs = np.concatenate([[0], np.cumsum(sizes)])
    w_t = jnp.transpose(w_in[0])
    w_parts = [(w_t, int(offs[i]), sz + (-sz) % LANES) for i, sz in enumerate(sizes)]
    wts = (norm1_w[0], tuple(w_parts), hg_lb_logits.astype(F32), hg_norm_w[0], w_out[0].astype(BF16), norm2_w[0],
           w_up[0].astype(BF16), w_down[0].astype(BF16), final_norm_w)

    yp, kp, vp, kip, sp = _trunk(x_prompt, mod[:bp], 0, None, wts, (1, 256, 1024, 128, 1024, 512, 1, 512))
    past = (cache_k[0], cache_v[0], cache_ki[0], state_hgrn[0])
    ys, ks, vs, kis, ss = _trunk(x_sample, mod[bp:bp + bs], past_len, past, wts,
                                 (bs // 2, ts, bs * ts, ts, 512, ts, bs, ts))
    return (yp, ys, kp[None], vp[None], kip[None], sp[None], ks[None], vs[None], kis[None], ss[None])
```

```python
import functools
import math

import jax
import jax.numpy as jnp
import numpy as np
from jax import lax
from jax.experimental import pallas as pl
from jax.experimental.pallas import tpu as pltpu

CHUNK = 64
N_HEADS = 16
HEAD_DIM = 128
N_KV_HEADS = 4
GROUP = N_HEADS // N_KV_HEADS
ROT_DIM = HEAD_DIM // 4
ROPE_THETA = 500000.0
IDX_HEADS = 16
IDX_DIM = 128
TOPK_MAX = 256
HG_HEADS = 16
HG_DK = 128
HG_DV = 128
EPS = 1e-6
LANES = 128
SUBLANES = 8
BF16_ROWS = 16
SUB_BLOCK = 8
PROJ_TN = 512
COUNT_UNIT = 512
CNT_ROWS = 64
VT_ROWS = HEAD_DIM + BF16_ROWS
VMEM_LIMIT = 56 * 1024 * 1024
NEG_INF = float("-inf")
POS_INF = float("inf")
LOG2E = math.log2(math.e)

F32 = jnp.float32
BF16 = jnp.bfloat16


def _cparams(sem):
    return pltpu.CompilerParams(dimension_semantics=sem, vmem_limit_bytes=VMEM_LIMIT)


def _dot_nt(a, b):
    return lax.dot_general(a, b, (((1,), (1,)), ((), ())), preferred_element_type=F32)


def _dot_tn(a, b):
    return lax.dot_general(a, b, (((0,), (0,)), ((), ())), preferred_element_type=F32)


def _dot(a, b):
    return jnp.dot(a, b, preferred_element_type=F32)


def _sigmoid(x):
    return 1.0 / (1.0 + jnp.exp(-x))


def _col_reduce(op, x):
    rows, n = x.shape
    if rows > CNT_ROWS and rows % CNT_ROWS == 0:
        x = op(x.reshape(rows // CNT_ROWS, CNT_ROWS, n), axis=0)
    return op(x, axis=0, keepdims=True)


def _ada_kernel(c_ref, w_ref, b_ref, o_ref):
    o_ref[...] = _dot(c_ref[...].astype(BF16), w_ref[...].astype(BF16)) + b_ref[...]


def _ada(c_all, w_ada, b_ada, tn=1024):
    r, d = c_all.shape
    n = w_ada.shape[1]
    return pl.pallas_call(
        _ada_kernel,
        out_shape=jax.ShapeDtypeStruct((r, n), F32),
        grid=(n // tn,),
        in_specs=[pl.BlockSpec((r, d), lambda j: (0, 0)),
                  pl.BlockSpec((d, tn), lambda j: (0, j)),
                  pl.BlockSpec((1, tn), lambda j: (0, j))],
        out_specs=pl.BlockSpec((r, tn), lambda j: (0, j)),
        compiler_params=_cparams(("arbitrary",)),
        name="ada",
    )(c_all, w_ada, b_ada)


def _normmod_kernel(x_ref, sc_ref, sh_ref, g_ref, o_ref):
    x = x_ref[...]
    ms = jnp.mean(x * x, axis=-1, keepdims=True)
    xn = x * lax.rsqrt(ms + EPS) * g_ref[...]
    o_ref[...] = (xn * (1.0 + sc_ref[...]) + sh_ref[...]).astype(o_ref.dtype)


def _normmod(x, sc, sh, g, bb, tb):
    b, t, d = x.shape
    return pl.pallas_call(
        _normmod_kernel,
        out_shape=jax.ShapeDtypeStruct((b, t, d), BF16),
        grid=(b // bb, t // tb),
        in_specs=[pl.BlockSpec((bb, tb, d), lambda i, j: (i, j, 0)),
                  pl.BlockSpec((bb, 1, d), lambda i, j: (i, 0, 0)),
                  pl.BlockSpec((bb, 1, d), lambda i, j: (i, 0, 0)),
                  pl.BlockSpec((1, 1, d), lambda i, j: (0, 0, 0))],
        out_specs=pl.BlockSpec((bb, tb, d), lambda i, j: (i, j, 0)),
        compiler_params=_cparams(("arbitrary", "arbitrary")),
        name="normmod",
    )(x, sc, sh, g)


def _proj_plain_kernel(h_ref, w_ref, *o_refs, scale):
    z = _dot_nt(h_ref[...], w_ref[...].astype(BF16))
    if scale != 1.0:
        z = z * scale
    for o_ref in o_refs:
        o_ref[...] = z.astype(o_ref.dtype)


def _proj_rope_kernel(h_ref, w_ref, cos_ref, sup_ref, sdn_ref, *o_refs, scale):
    z = _dot_nt(h_ref[...], w_ref[...].astype(BF16))
    tn = z.shape[1]
    reps = tn // LANES

    def wide(ref):
        t = ref[...]
        return t if reps == 1 else jnp.concatenate([t] * reps, axis=1)

    up = pltpu.roll(z, tn - ROT_DIM // 2, 1)
    dn = pltpu.roll(z, ROT_DIM // 2, 1)
    r = z * wide(cos_ref) + up * wide(sup_ref) + dn * wide(sdn_ref)
    for o_ref in o_refs:
        if o_ref.dtype == BF16 and scale != 1.0:
            o_ref[...] = (r * scale).astype(BF16)
        else:
            o_ref[...] = r.astype(o_ref.dtype)


def _proj_forget_kernel(h_ref, w_ref, lbl_ref, lf_ref, kk_ref):
    z = _dot_nt(h_ref[...], w_ref[...].astype(BF16))
    lbl = lbl_ref[...]
    mx = jnp.max(lbl, axis=0, keepdims=True)
    e = jnp.exp(lbl - mx)
    lb = e[0:1, :] / jnp.sum(e, axis=0, keepdims=True)
    f = lb + (1.0 - lb) * _sigmoid(z)
    lf_ref[...] = jnp.log(f) * LOG2E
    kk_ref[...] = 1.0 - f


def _proj(kind, h, wspec, tm, tn, extra=(), out_dtypes=(F32,), scale=1.0):
    w, c0, n = wspec
    m, k = h.shape
    assert c0 % SUBLANES == 0 and n % tn == 0
    in_specs = [pl.BlockSpec((tm, k), lambda i, j: (i, 0)),
                pl.BlockSpec((pl.Element(tn), pl.Element(k)),
                             lambda i, j: (pl.multiple_of(c0 + j * tn, SUBLANES), 0))]
    if kind == "rope":
        kern = functools.partial(_proj_rope_kernel, scale=scale)
        in_specs += [pl.BlockSpec((tm, LANES), lambda i, j: (i, 0))] * 3
    elif kind == "forget":
        kern = _proj_forget_kernel
        in_specs += [pl.BlockSpec((extra[0].shape[0], tn), lambda i, j: (0, j))]
    else:
        kern = functools.partial(_proj_plain_kernel, scale=scale)
    outs = tuple(jax.ShapeDtypeStruct((m, n), dt) for dt in out_dtypes)
    out_specs = tuple(pl.BlockSpec((tm, tn), lambda i, j: (i, j)) for _ in out_dtypes)
    res = pl.pallas_call(
        kern,
        out_shape=outs,
        grid=(m // tm, n // tn),
        in_specs=in_specs,
        out_specs=out_specs,
        compiler_params=_cparams(("arbitrary", "arbitrary")),
        name="proj_" + kind,
    )(h, w, *extra)
    return res


def _proj_v_kernel(h_ref, w_ref, vf_ref, vt_ref, *, wn):
    z = _dot_nt(h_ref[...], w_ref[...].astype(BF16))
    vf_ref[...] = z
    zt = z.T
    ones = jnp.ones((BF16_ROWS, wn), BF16)
    for u in range(z.shape[0] // wn):
        blk = zt[:, u * wn:(u + 1) * wn].astype(BF16)
        for g in range(N_KV_HEADS):
            vt_ref[u, g * VT_ROWS:g * VT_ROWS + HEAD_DIM, :] = blk[g * HEAD_DIM:(g + 1) * HEAD_DIM]
            vt_ref[u, g * VT_ROWS + HEAD_DIM:(g + 1) * VT_ROWS, :] = ones


def _proj_v(h, wspec, tm, wn):
    w, c0, n = wspec
    m, k = h.shape
    assert c0 % SUBLANES == 0
    return pl.pallas_call(
        functools.partial(_proj_v_kernel, wn=wn),
        out_shape=(jax.ShapeDtypeStruct((m, n), F32),
                   jax.ShapeDtypeStruct((m // wn, N_KV_HEADS * VT_ROWS, wn), BF16)),
        grid=(m // tm,),
        in_specs=[pl.BlockSpec((tm, k), lambda i: (i, 0)),
                  pl.BlockSpec((pl.Element(n), pl.Element(k)), lambda i: (c0, 0))],
        out_specs=(pl.BlockSpec((tm, n), lambda i: (i, 0)),
                   pl.BlockSpec((tm // wn, N_KV_HEADS * VT_ROWS, wn), lambda i: (i, 0, 0))),
        compiler_params=_cparams(("arbitrary",)),
        name="proj_v",
    )(h, w)


def _dsa_kernel(*refs, tq, tk, t_new, l_cache, pos0, topk, has_cache):
    if has_cache:
        (qi_ref, wi_ref, q_ref, kin_ref, kn_ref, vtn_ref, cki_ref, ck_ref, cv_ref, o_ref,
         sc_ref, qit_ref, wt_ref, qt_ref, acc_ref, s_ref, p_ref, tau_ref) = refs
    else:
        (qi_ref, wi_ref, q_ref, kin_ref, kn_ref, vtn_ref, o_ref,
         sc_ref, qit_ref, wt_ref, qt_ref, acc_ref, s_ref, p_ref, tau_ref) = refs
        cki_ref = ck_ref = cv_ref = None
    i = pl.program_id(1)
    ntc = l_cache // tk
    wn = min(tk, t_new)
    nq = GROUP * tq
    q0 = pos0 + i * tq
    lane_q = lax.broadcasted_iota(jnp.int32, (1, tq), 1)
    qend = (((q0 + lane_q) >> 6) + 1) << 6
    last_end = (((q0 + tq - 1) >> 6) + 1) << 6
    nvis_new = jnp.minimum(last_end - pos0, t_new)
    ntn = (nvis_new + wn - 1) // wn
    ntiles = ntc + ntn

    def tpose(x):
        xf = x.astype(F32)
        if tq < LANES:
            xf = jnp.concatenate([xf, jnp.zeros((LANES - tq, LANES), F32)], axis=0)
        xt = xf.T
        return xt[:, :tq] if tq < LANES else xt

    qi_blk = qi_ref[0]
    for h in range(IDX_HEADS):
        qit_ref[:, h * tq:(h + 1) * tq] = tpose(qi_blk[:, h * IDX_DIM:(h + 1) * IDX_DIM]).astype(BF16)
    wt_ref[...] = tpose(wi_ref[0])
    q_blk = q_ref[0]
    for g in range(N_KV_HEADS):
        for a in range(GROUP):
            hh = GROUP * g + a
            qt_ref[g, :, a * tq:(a + 1) * tq] = tpose(q_blk[:, hh * HEAD_DIM:(hh + 1) * HEAD_DIM]).astype(BF16)

    def score_tile(ki_tile, kpos0, w):
        acc = jnp.zeros((w, tq), F32)
        hpd = 2 if 2 * tq % LANES == 0 else 1
        for hp in range(IDX_HEADS // hpd):
            lg = _dot(ki_tile, qit_ref[:, hpd * hp * tq:(hpd * hp + hpd) * tq])
            for e in range(hpd):
                h = hpd * hp + e
                acc = acc + wt_ref[h:h + 1, :] * jnp.maximum(lg[:, e * tq:(e + 1) * tq], 0.0)
        kpos = kpos0 + lax.broadcasted_iota(jnp.int32, (w, 1), 0)
        vis = kpos < qend
        s = jnp.where(vis, acc, NEG_INF)
        smin = _col_reduce(jnp.min, jnp.where(vis, acc, POS_INF))
        smax = _col_reduce(jnp.max, s)
        return s, smax, smin

    def p1_cache(t, carry):
        mx, mn = carry
        r0 = pl.multiple_of(t * tk, tk)
        s, smax, smin = score_tile(cki_ref[0, pl.ds(r0, tk), :].astype(BF16), r0, tk)
        sc_ref[t] = s
        return jnp.maximum(mx, smax), jnp.minimum(mn, smin)

    def p1_new(j, carry):
        mx, mn = carry
        r0 = pl.multiple_of(j * wn, wn)
        s, smax, smin = score_tile(kin_ref[0, pl.ds(r0, wn), :], pos0 + r0, wn)
        if wn == tk:
            sc_ref[ntc + j] = s
        else:
            sc_ref[ntc + j] = jnp.full((tk, tq), NEG_INF, F32)
            sc_ref[ntc + j, 0:wn, :] = s
        return jnp.maximum(mx, smax), jnp.minimum(mn, smin)

    carry = (jnp.full((1, tq), NEG_INF, F32), jnp.full((1, tq), POS_INF, F32))
    if has_cache:
        carry = lax.fori_loop(0, ntc, p1_cache, carry)
    mx, mn = lax.fori_loop(0, ntn, p1_new, carry)

    nvis = jnp.clip(qend - pos0, 0, t_new)
    if has_cache:
        nvis = nvis + jnp.minimum(qend, l_cache)
    act0 = jnp.where(nvis > topk, 1.0, 0.0)
    kf = float(topk)
    cu = min(tk, COUNT_UNIT)
    upt = tk // cu
    n_units = ntc * upt + (nvis_new + cu - 1) // cu

    def bis_cond(carry):
        return jnp.logical_and(carry[0] < 400, carry[1] > 0)

    def unit(u):
        return sc_ref.at[u // upt, pl.ds(pl.multiple_of((u % upt) * cu, cu), cu), :]

    def count_where(pred):
        def body(u, cnt):
            ind = jnp.where(pred(unit(u)[...]), 1.0, 0.0)
            return cnt + jnp.sum(ind.reshape(cu // CNT_ROWS, CNT_ROWS, tq), axis=0)

        cnt = lax.fori_loop(0, n_units, body, jnp.zeros((CNT_ROWS, tq), F32))
        return jnp.sum(cnt, axis=0, keepdims=True)

    def bis_body(carry):
        it, _, lo, hi, tau, act, tied = carry
        mid = lo * 0.5 + hi * 0.5
        c = count_where(lambda s: s >= mid)
        found = jnp.where(c == kf, act, 0.0)
        stuck = jnp.where(jnp.logical_or(mid <= lo, mid >= hi), act, 0.0) * (1.0 - found)
        above = jnp.where(c > kf, act, 0.0) * (1.0 - stuck)
        below = act * (1.0 - above) * (1.0 - found) * (1.0 - stuck)
        tau = jnp.where(found > 0, mid, tau)
        tau = jnp.where(stuck > 0, lo, tau)
        lo = jnp.where(above > 0, mid, lo)
        hi = jnp.where(below > 0, mid, hi)
        nact = act * (1.0 - found) * (1.0 - stuck)
        return it + 1, (jnp.max(nact) > 0).astype(jnp.int32), lo, hi, tau, nact, jnp.maximum(tied, stuck)

    init = (jnp.int32(0), (jnp.max(act0) > 0).astype(jnp.int32), mn, mx, mn, act0, jnp.zeros((1, tq), F32))
    res = lax.while_loop(bis_cond, lambda carry: bis_body(bis_body(carry)), init)
    tau_ref[...] = res[4]
    tied = res[6]

    @pl.when(jnp.max(tied) > 0)
    def _():
        lo_t = tau_ref[...]

        def tmin_body(u, acc):
            s = unit(u)[...]
            return jnp.minimum(acc, _col_reduce(jnp.min, jnp.where(s >= lo_t, s, POS_INF)))

        tstar = lax.fori_loop(0, n_units, tmin_body, jnp.full((1, tq), POS_INF, F32))
        tstar = jnp.where(tied > 0, tstar, lo_t)
        need = kf - count_where(lambda s: s > tstar)
        ri = lax.broadcasted_iota(jnp.int32, (cu, cu), 0)
        ci = lax.broadcasted_iota(jnp.int32, (cu, cu), 1)
        tri = jnp.where(ri >= ci, 1.0, 0.0).astype(BF16)

        def strike_body(u, run):
            s = unit(u)[...]
            eq = jnp.where(jnp.logical_and(s == tstar, tied > 0), 1.0, 0.0)
            rank = run + _dot(tri, eq.astype(BF16))
            unit(u)[...] = jnp.where(jnp.logical_and(eq > 0, rank > need), NEG_INF, s)
            return rank[cu - 1:cu, :]

        lax.fori_loop(0, n_units, strike_body, jnp.zeros((1, tq), F32))
        tau_ref[...] = tstar

    tau = tau_ref[...]

    acc_ref[...] = jnp.zeros(acc_ref.shape, F32)

    def attend(t, k_of_g, vt_of_g, w, ms):
        s_idx = sc_ref[t]
        if w < tk:
            s_idx = s_idx[0:w]
        bias = jnp.where(s_idx >= tau, 0.0, NEG_INF)
        bias4 = jnp.concatenate([bias] * GROUP, axis=1)
        def qk(g):
            s = _dot(k_of_g(g), qt_ref[g]) + bias4
            s_ref[g % 2, 0:w] = s
            return _col_reduce(jnp.max, s)

        out = []
        tile_max = qk(0)
        for g in range(N_KV_HEADS):
            slot = g % 2
            next_max = qk(g + 1) if g + 1 < N_KV_HEADS else None
            m_old = ms[g]
            m_new = jnp.maximum(m_old, tile_max)
            m_safe = jnp.where(m_new == NEG_INF, 0.0, m_new)
            alpha = jnp.exp2(m_old - m_safe)
            p_ref[slot, 0:w] = jnp.exp2(s_ref[slot, 0:w] - m_safe).astype(BF16)
            acc_ref[g] = acc_ref[g] * alpha + _dot(vt_of_g(g), p_ref[slot, 0:w])
            out.append(m_new)
            tile_max = next_max
        return tuple(out)

    def p3_cache(t, ms):
        r0 = pl.multiple_of(t * tk, tk)

        def vt(g):
            vg = cv_ref[0, pl.ds(r0 * N_KV_HEADS + g, tk, stride=N_KV_HEADS), :]
            return jnp.concatenate([vg.T, jnp.ones((BF16_ROWS, tk), F32)], axis=0).astype(BF16)

        return attend(
            t, lambda g: ck_ref[0, pl.ds(r0 * N_KV_HEADS + g, tk, stride=N_KV_HEADS), :].astype(BF16), vt, tk, ms)

    def p3_new(j, ms):
        r0 = pl.multiple_of(j * wn, wn)
        return attend(ntc + j,
                      lambda g: kn_ref[0, pl.ds(r0, wn), g * HEAD_DIM:(g + 1) * HEAD_DIM],
                      lambda g: vtn_ref[0, j, g * VT_ROWS:(g + 1) * VT_ROWS, :], wn, ms)

    ms = tuple(jnp.full((1, nq), NEG_INF, F32) for _ in range(N_KV_HEADS))
    if has_cache:
        ms = lax.fori_loop(0, ntc, p3_cache, ms)
    lax.fori_loop(0, ntn, p3_new, ms)

    for g in range(N_KV_HEADS):
        acc = acc_ref[g]
        og = (acc[0:HEAD_DIM] / acc[HEAD_DIM:HEAD_DIM + 1]).T
        for a in range(GROUP):
            hh = GROUP * g + a
            o_ref[0, :, hh * HEAD_DIM:(hh + 1) * HEAD_DIM] = og[a * tq:(a + 1) * tq, :].astype(o_ref.dtype)


def _dsa(qi, wi, q, kin, kn, vtn, cache, pos0, tq, tk):
    b, t, _ = q.shape
    has_cache = cache is not None
    l_cache = cache[0].shape[1] if has_cache else 0
    topk = min(TOPK_MAX, (l_cache + t) // 4)
    wn = min(tk, t)
    ntn = t // wn
    nt = l_cache // tk + ntn
    kvd = N_KV_HEADS * HEAD_DIM
    nq = GROUP * tq
    res = pl.Buffered(1) if b == 1 else None
    in_specs = [pl.BlockSpec((1, tq, IDX_HEADS * IDX_DIM), lambda bb, i: (bb, i, 0)),
                pl.BlockSpec((1, tq, LANES), lambda bb, i: (bb, i, 0)),
                pl.BlockSpec((1, tq, N_HEADS * HEAD_DIM), lambda bb, i: (bb, i, 0)),
                pl.BlockSpec((1, t, IDX_DIM), lambda bb, i: (bb, 0, 0), pipeline_mode=res),
                pl.BlockSpec((1, t, kvd), lambda bb, i: (bb, 0, 0), pipeline_mode=res),
                pl.BlockSpec((1, ntn, N_KV_HEADS * VT_ROWS, wn), lambda bb, i: (bb, 0, 0, 0), pipeline_mode=res)]
    args = [qi, wi, q, kin, kn, vtn]
    if has_cache:
        in_specs += [pl.BlockSpec((1, l_cache, IDX_DIM), lambda bb, i: (bb, 0, 0)),
                     pl.BlockSpec((1, l_cache * N_KV_HEADS, HEAD_DIM), lambda bb, i: (bb, 0, 0)),
                     pl.BlockSpec((1, l_cache * N_KV_HEADS, HEAD_DIM), lambda bb, i: (bb, 0, 0))]
        args += list(cache)
    kern = functools.partial(_dsa_kernel, tq=tq, tk=tk, t_new=t, l_cache=l_cache, pos0=pos0, topk=topk,
                             has_cache=has_cache)
    return pl.pallas_call(
        kern,
        out_shape=jax.ShapeDtypeStruct((b, t, N_HEADS * HEAD_DIM), BF16),
        grid=(b, t // tq),
        in_specs=in_specs,
        out_specs=pl.BlockSpec((1, tq, N_HEADS * HEAD_DIM), lambda bb, i: (bb, i, 0)),
        scratch_shapes=[pltpu.VMEM((nt, tk, tq), F32),
                        pltpu.VMEM((IDX_DIM, IDX_HEADS * tq), BF16),
                        pltpu.VMEM((LANES, tq), F32),
                        pltpu.VMEM((N_KV_HEADS, HEAD_DIM, nq), BF16),
                        pltpu.VMEM((N_KV_HEADS, VT_ROWS, nq), F32),
                        pltpu.VMEM((2, tk, nq), F32),
                        pltpu.VMEM((2, tk, nq), BF16),
                        pltpu.VMEM((1, tq), F32)],
        compiler_params=_cparams(("arbitrary", "arbitrary")),
        name="dsa",
    )(*args)


def _hgrn_kernel(*refs, c, nchunk, nh, has_state):
    if has_state:
        q_ref, lf_ref, kk_ref, v_ref, hg_ref, nw_ref, s0_ref, o_ref, sout_ref, st_ref = refs
    else:
        q_ref, lf_ref, kk_ref, v_ref, hg_ref, nw_ref, o_ref, sout_ref, st_ref = refs
        s0_ref = None
    r = pl.program_id(2)

    @pl.when(r == 0)
    def _():
        for hh in range(nh):
            if has_state:
                st_ref[hh] = s0_ref[0, hh].T
            else:
                st_ref[hh] = jnp.zeros((HG_DV, HG_DK), F32)

    ri = lax.broadcasted_iota(jnp.int32, (c, c), 0)
    ci = lax.broadcasted_iota(jnp.int32, (c, c), 1)
    tri = jnp.where(ri >= ci, 1.0, 0.0).astype(BF16)
    nsub = c // SUB_BLOCK
    pair = lax.broadcasted_iota(jnp.int32, (SUB_BLOCK * SUB_BLOCK, HG_DK), 0)
    cap = jnp.where((pair % SUB_BLOCK) >= (pair // SUB_BLOCK), 0.0, NEG_INF)
    ones_w = jnp.ones((HG_DK, LANES), BF16)

    def rep_rows(x, lo_):
        return jnp.concatenate(
            [jnp.broadcast_to(x[lo_ + sg:lo_ + sg + 1], (SUB_BLOCK, x.shape[1])) for sg in range(SUB_BLOCK)], axis=0)

    def stage_a(base, hh):
        cs = slice(hh * HG_DK, (hh + 1) * HG_DK)
        lf = lf_ref[pl.ds(base, c), cs]
        l1 = lf.astype(BF16)
        r1 = lf - l1.astype(F32)
        l2 = r1.astype(BF16)
        l3 = (r1 - l2.astype(F32)).astype(BF16)
        G = _dot(tri, l1) + _dot(tri, l2) + _dot(tri, l3)
        return dict(cs=cs, G=G, q=q_ref[pl.ds(base, c), cs], kk=kk_ref[pl.ds(base, c), cs],
                    v=v_ref[pl.ds(base, c), cs])

    def stage_b(hh, d):
        q, kk, v, G = d["q"], d["kk"], d["v"], d["G"]
        st = st_ref[hh]
        vb = v.astype(BF16)
        d["vb"] = vb
        d["o_inter"] = _dot_nt((q * jnp.exp2(G)).astype(BF16), st.astype(BF16))
        d["A"], d["rs"] = [], []
        for i in range(nsub):
            lo_, hi_ = i * SUB_BLOCK, (i + 1) * SUB_BLOCK
            qi_ = q[lo_:hi_]
            Gi = G[lo_:hi_]
            if i > 0:
                Gb = G[lo_ - 1:lo_]
                qt = qi_ * jnp.exp2(Gi - Gb)
                kt = kk[:lo_] * jnp.exp2(Gb - G[:lo_])
                d["A"].append(_dot_nt(qt.astype(BF16), kt.astype(BF16)))
            qrep = jnp.concatenate([qi_] * SUB_BLOCK, axis=0)
            grep = jnp.concatenate([Gi] * SUB_BLOCK, axis=0)
            D = qrep * rep_rows(kk, lo_) * jnp.exp2(jnp.minimum(grep - rep_rows(G, lo_), cap))
            d["rs"].append(_dot(D.astype(BF16), ones_w))
        Gl = G[c - 1:c]
        kdec = kk * jnp.exp2(Gl - G)
        st_ref[hh] = st * jnp.exp2(Gl) + _dot_tn(vb, kdec.astype(BF16))

    def stage_c(base, d):
        parts = []
        for i in range(nsub):
            lo_, hi_ = i * SUB_BLOCK, (i + 1) * SUB_BLOCK
            oi = d["o_inter"][lo_:hi_]
            if i > 0:
                oi = oi + _dot(d["A"][i - 1].astype(BF16), d["vb"][:lo_])
            contrib = (d["rs"][i] * rep_rows(d["v"], lo_)).reshape(SUB_BLOCK, SUB_BLOCK, HG_DV)
            parts.append(oi + jnp.sum(contrib, axis=0))
        o = jnp.concatenate(parts, axis=0) if nsub > 1 else parts[0]
        on = o * lax.rsqrt(jnp.mean(o * o, axis=1, keepdims=True) + EPS) * nw_ref[...]
        hg = hg_ref[pl.ds(base, c), d["cs"]]
        o_ref[pl.ds(base, c), d["cs"]] = (on * (hg * _sigmoid(hg))).astype(o_ref.dtype)

    def chunk(n, carry):
        base = pl.multiple_of(n * c, c)
        heads = [stage_a(base, hh) for hh in range(nh)]
        for hh, d in enumerate(heads):
            stage_b(hh, d)
        for d in heads:
            stage_c(base, d)
        return carry

    lax.fori_loop(0, nchunk, chunk, 0)

    @pl.when(r == pl.num_programs(2) - 1)
    def _():
        for hh in range(nh):
            sout_ref[0, hh] = st_ref[hh].T


def _hgrn(hq, lf, kk, hv, hg, nw, s0, b, t, rb, nh):
    c = min(CHUNK, t)
    nchunk = rb // c
    nr = t // rb
    has_state = s0 is not None
    blk = pl.BlockSpec((rb, nh * HG_DK), lambda bb, h, r: (bb * nr + r, h))
    sblk = pl.BlockSpec((1, nh, HG_DK, HG_DV), lambda bb, h, r: (bb, h, 0, 0))
    in_specs = [blk, blk, blk, blk, blk, pl.BlockSpec((1, HG_DV), lambda bb, h, r: (0, 0))]
    args = [hq, lf, kk, hv, hg, nw]
    if has_state:
        in_specs.append(sblk)
        args.append(s0)
    kern = functools.partial(_hgrn_kernel, c=c, nchunk=nchunk, nh=nh, has_state=has_state)
    return pl.pallas_call(
        kern,
        out_shape=(jax.ShapeDtypeStruct((b * t, HG_HEADS * HG_DV), BF16),
                   jax.ShapeDtypeStruct((b, HG_HEADS, HG_DK, HG_DV), F32)),
        grid=(b, HG_HEADS // nh, nr),
        in_specs=in_specs,
        out_specs=(blk, sblk),
        scratch_shapes=[pltpu.VMEM((nh, HG_DV, HG_DK), F32)],
        compiler_params=_cparams(("arbitrary", "arbitrary", "arbitrary")),
        name="hgrn",
    )(*args)


def _merge_out_kernel(x_ref, oa_ref, oh_ref, ga_ref, gb_ref, w_ref, g1_ref, sc_ref, sh_ref, nw_ref,
                      x1_ref, h2_ref):
    bb, tb, d = x_ref.shape
    merged = (_sigmoid(ga_ref[...].astype(F32)) * oa_ref[...].astype(F32)
              + _sigmoid(gb_ref[...].astype(F32)) * oh_ref[...].astype(F32))
    y = _dot(merged.astype(BF16), w_ref[...]).reshape(bb, tb, d)
    x1 = x_ref[...] + g1_ref[...] * y
    x1_ref[...] = x1
    ms = jnp.mean(x1 * x1, axis=-1, keepdims=True)
    xn = x1 * lax.rsqrt(ms + EPS) * nw_ref[...]
    h2 = xn * (1.0 + sc_ref[...]) + sh_ref[...]
    h2_ref[...] = h2.reshape(bb * tb, d).astype(BF16)


def _merge_out(x, oa, oh, ga, gb, w_out, g1, sc2, sh2, nw2, bb, tb):
    b, t, d = x.shape
    tm = bb * tb
    nt = t // tb

    def row(i, j):
        return (i * nt + j, 0)

    def mod(i, j):
        return (i, 0, 0)

    x3 = pl.BlockSpec((bb, tb, d), lambda i, j: (i, j, 0))
    r2 = pl.BlockSpec((tm, d), row)
    return pl.pallas_call(
        _merge_out_kernel,
        out_shape=(jax.ShapeDtypeStruct((b, t, d), F32), jax.ShapeDtypeStruct((b * t, d), BF16)),
        grid=(b // bb, nt),
        in_specs=[x3, r2, r2, r2, r2,
                  pl.BlockSpec((d, d), lambda i, j: (0, 0)),
                  pl.BlockSpec((bb, 1, d), mod), pl.BlockSpec((bb, 1, d), mod), pl.BlockSpec((bb, 1, d), mod),
                  pl.BlockSpec((1, 1, d), lambda i, j: (0, 0, 0))],
        out_specs=(x3, r2),
        compiler_params=_cparams(("arbitrary", "arbitrary")),
        name="merge_out",
    )(x, oa, oh, ga, gb, w_out, g1, sc2, sh2, nw2)


def _mlp_kernel(h_ref, wu_ref, wd_ref, x1_ref, g2_ref, fw_ref, y_ref, acc_ref):
    f = pl.program_id(2)

    @pl.when(f == 0)
    def _():
        acc_ref[...] = jnp.zeros(acc_ref.shape, F32)

    u = jnp.maximum(_dot(h_ref[...], wu_ref[...]), 0.0)
    acc_ref[...] += _dot((u * u).astype(BF16), wd_ref[...])

    @pl.when(f == pl.num_programs(2) - 1)
    def _():
        bb, tb, d = x1_ref.shape
        x2 = x1_ref[...] + g2_ref[...] * acc_ref[...].reshape(bb, tb, d)
        ms = jnp.mean(x2 * x2, axis=-1, keepdims=True)
        y_ref[...] = x2 * lax.rsqrt(ms + EPS) * fw_ref[...]


def _mlp(h2, w_up, w_down, x1, g2, fw, bb, tb, tf):
    b, t, d = x1.shape
    dff = w_up.shape[1]
    tm = bb * tb
    nt = t // tb
    x3 = pl.BlockSpec((bb, tb, d), lambda i, j, f: (i, j, 0))
    return pl.pallas_call(
        _mlp_kernel,
        out_shape=jax.ShapeDtypeStruct((b, t, d), F32),
        grid=(b // bb, nt, dff // tf),
        in_specs=[pl.BlockSpec((tm, d), lambda i, j, f: (i * nt + j, 0)),
                  pl.BlockSpec((d, tf), lambda i, j, f: (0, f)),
                  pl.BlockSpec((tf, d), lambda i, j, f: (f, 0)),
                  x3,
                  pl.BlockSpec((bb, 1, d), lambda i, j, f: (i, 0, 0)),
                  pl.BlockSpec((1, 1, d), lambda i, j, f: (0, 0, 0))],
        out_specs=x3,
        scratch_shapes=[pltpu.VMEM((tm, d), F32)],
        compiler_params=_cparams(("arbitrary", "arbitrary", "arbitrary")),
        name="mlp",
    )(h2, w_up, w_down, x1, g2, fw)


def _rope_tables(pos):
    half = ROT_DIM // 2
    inv_freq = ROPE_THETA ** (-(jnp.arange(half, dtype=F32) * (2.0 / ROT_DIM)))
    ang = pos.astype(F32)[:, None] * inv_freq[None, :]
    cos, sin = jnp.cos(ang), jnp.sin(ang)
    n = pos.shape[0]
    ones = jnp.ones((n, LANES - ROT_DIM), F32)
    zeros = jnp.zeros((n, LANES - ROT_DIM), F32)
    zh = jnp.zeros((n, half), F32)
    c_t = jnp.concatenate([cos, cos, ones], axis=1)
    s_up = jnp.concatenate([-sin, zh, zeros], axis=1)
    s_dn = jnp.concatenate([zh, sin, zeros], axis=1)
    return c_t, s_up, s_dn


def _trunk(x, mod, pos0, past, wts, blocks):
    (norm1_w, w_parts, lb_logits, hg_norm_w, w_out, norm2_w, w_up, w_down, final_w) = wts
    b, t, d = x.shape
    bb, tb, tm_proj, tq, tk, rb, mlp_bb, mlp_tb = blocks
    m = [mod[:, i:i + 1, :] for i in range(6)]
    sh1, sc1, g1, sh2, sc2, g2 = m
    h = _normmod(x, sc1, sh1, norm1_w.reshape(1, 1, d), bb, tb).reshape(b * t, d)

    pos = pos0 + jnp.arange(t, dtype=jnp.int32)
    tabs = tuple(jnp.tile(tb_, (b, 1)) for tb_ in _rope_tables(pos))
    wq, wk, wv, wqi, wki, wwi, whq, whf, whi, whg, wga, wgb = w_parts
    tm = min(tm_proj, b * t)
    wn = min(tk, t)
    (q_bf,) = _proj("rope", h, wq, tm, PROJ_TN, tabs, (BF16,), scale=HEAD_DIM ** -0.5 * LOG2E)
    k_f, k_bf = _proj("rope", h, wk, tm, PROJ_TN, tabs, (F32, BF16))
    v_f, vt_bf = _proj_v(h, wv, tm, wn)
    (qi_bf,) = _proj("rope", h, wqi, tm, PROJ_TN, tabs, (BF16,))
    ki_f, ki_bf = _proj("rope", h, wki, tm, LANES, tabs, (F32, BF16))
    (wi_f,) = _proj("plain", h, wwi, tm, LANES, (), (F32,), scale=IDX_HEADS ** -0.5 * IDX_DIM ** -0.5)
    (hq,) = _proj("plain", h, whq, tm, PROJ_TN)
    lf, kk = _proj("forget", h, whf, tm, PROJ_TN, (lb_logits,), (F32, F32))
    (hi,) = _proj("plain", h, whi, tm, PROJ_TN)
    (hg,) = _proj("plain", h, whg, tm, PROJ_TN)
    (ga,) = _proj("plain", h, wga, tm, PROJ_TN, (), (BF16,))
    (gb,) = _proj("plain", h, wgb, tm, PROJ_TN, (), (BF16,))

    kvd = N_KV_HEADS * HEAD_DIM
    r3 = lambda a: a.reshape(b, t, a.shape[-1])
    if past is None:
        cache, s0 = None, None
    else:
        ck, cv, cki, s0 = past
        lc = ck.shape[1]
        cache = (cki, ck.reshape(b, lc * N_KV_HEADS, HEAD_DIM), cv.reshape(b, lc * N_KV_HEADS, HEAD_DIM))
    vtn = vt_bf.reshape(b, t // wn, N_KV_HEADS * VT_ROWS, wn)
    o_attn = _dsa(r3(qi_bf), r3(wi_f), r3(q_bf), r3(ki_bf), r3(k_bf), vtn, cache, pos0, tq, tk)
    o_hg, s_new = _hgrn(hq, lf, kk, hi, hg, hg_norm_w.reshape(1, HG_DV), s0, b, t, rb, 8)

    x1, h2 = _merge_out(x, o_attn.reshape(b * t, d), o_hg, ga, gb, w_out, g1, sc2, sh2,
                        norm2_w.reshape(1, 1, d), bb, tb)
    y = _mlp(h2, w_up, w_down, x1, g2, final_w.reshape(1, 1, d), mlp_bb, mlp_tb, 512)
    return (y, k_f.reshape(b, t, N_KV_HEADS, HEAD_DIM), v_f.reshape(b, t, N_KV_HEADS, HEAD_DIM),
            ki_f.reshape(b, t, IDX_DIM), s_new)


def kernel(x_prompt, x_sample, cache_k, cache_v, cache_ki, state_hgrn, c_prompt, c_sample, w_ada, b_ada, norm1_w,
           w_in, hg_lb_logits, hg_norm_w, w_out, norm2_w, w_up, w_down, final_norm_w):
    depth = w_in.shape[0]
    assert depth == 1
    d = x_prompt.shape[-1]
    bp, tp, _ = x_prompt.shape
    bs, ts, _ = x_sample.shape
    past_len = cache_k.shape[2]

    c_all = jnp.concatenate([c_prompt, c_sample], axis=0)
    nrow = c_all.shape[0]
    pad = (-nrow) % SUBLANES
    c_all = jnp.pad(c_all, ((0, pad), (0, 0)))
    mod = _ada(c_all, w_ada[0], b_ada[0].reshape(1, -1)).reshape(nrow + pad, 6, d)

    sizes = (N_HEADS * HEAD_DIM, N_KV_HEADS * HEAD_DIM, N_KV_HEADS * HEAD_DIM, IDX_HEADS * IDX_DIM, IDX_DIM,
             IDX_HEADS, HG_HEADS * HG_DK, HG_HEADS * HG_DK, HG_HEADS * HG_DV, HG_HEADS * HG_DV, d, d)
    offs = np.concatenate([[0], np.cumsum(sizes)])
    w_t = jnp.transpose(w_in[0])
    w_parts = [(w_t, int(offs[i]), sz + (-sz) % LANES) for i, sz in enumerate(sizes)]
    wts = (norm1_w[0], tuple(w_parts), hg_lb_logits.astype(F32), hg_norm_w[0], w_out[0].astype(BF16), norm2_w[0],
           w_up[0].astype(BF16), w_down[0].astype(BF16), final_norm_w)

    yp, kp, vp, kip, sp = _trunk(x_prompt, mod[:bp], 0, None, wts, (1, 256, 2048, 128, 1024, 512, 1, 512))
    past = (cache_k[0], cache_v[0], cache_ki[0], state_hgrn[0])
    ys, ks, vs, kis, ss = _trunk(x_sample, mod[bp:bp + bs], past_len, past, wts,
                                 (bs // 2, ts, bs * ts, ts, 512, ts, bs, ts))
    return (yp, ys, kp[None], vp[None], kip[None], sp[None], ks[None], vs[None], kis[None], ss[None])
```

```python
import functools
import math

import jax
import jax.numpy as jnp
import numpy as np
from jax import lax
from jax.experimental import pallas as pl
from jax.experimental.pallas import tpu as pltpu

CHUNK = 64
N_HEADS = 16
HEAD_DIM = 128
N_KV_HEADS = 4
GROUP = N_HEADS // N_KV_HEADS
ROT_DIM = HEAD_DIM // 4
ROPE_THETA = 500000.0
IDX_HEADS = 16
IDX_DIM = 128
TOPK_MAX = 256
HG_HEADS = 16
HG_DK = 128
HG_DV = 128
EPS = 1e-6
LANES = 128
SUBLANES = 8
BF16_ROWS = 16
SUB_BLOCK = 8
PROJ_TN = 512
COUNT_UNIT = 512
CNT_ROWS = 64
VT_ROWS = HEAD_DIM + BF16_ROWS
VMEM_LIMIT = 56 * 1024 * 1024
NEG_INF = float("-inf")
POS_INF = float("inf")
LOG2E = math.log2(math.e)

F32 = jnp.float32
BF16 = jnp.bfloat16


def _cparams(sem):
    return pltpu.CompilerParams(dimension_semantics=sem, vmem_limit_bytes=VMEM_LIMIT)


def _dot_nt(a, b):
    return lax.dot_general(a, b, (((1,), (1,)), ((), ())), preferred_element_type=F32)


def _dot_tn(a, b):
    return lax.dot_general(a, b, (((0,), (0,)), ((), ())), preferred_element_type=F32)


def _dot(a, b):
    return jnp.dot(a, b, preferred_element_type=F32)


def _sigmoid(x):
    return 1.0 / (1.0 + jnp.exp(-x))


def _col_reduce(op, x):
    rows, n = x.shape
    if rows > CNT_ROWS and rows % CNT_ROWS == 0:
        x = op(x.reshape(rows // CNT_ROWS, CNT_ROWS, n), axis=0)
    return op(x, axis=0, keepdims=True)


def _ada_kernel(c_ref, w_ref, b_ref, o_ref):
    o_ref[...] = _dot(c_ref[...].astype(BF16), w_ref[...].astype(BF16)) + b_ref[...]


def _ada(c_all, w_ada, b_ada, tn=1024):
    r, d = c_all.shape
    n = w_ada.shape[1]
    return pl.pallas_call(
        _ada_kernel,
        out_shape=jax.ShapeDtypeStruct((r, n), F32),
        grid=(n // tn,),
        in_specs=[pl.BlockSpec((r, d), lambda j: (0, 0)),
                  pl.BlockSpec((d, tn), lambda j: (0, j)),
                  pl.BlockSpec((1, tn), lambda j: (0, j))],
        out_specs=pl.BlockSpec((r, tn), lambda j: (0, j)),
        compiler_params=_cparams(("arbitrary",)),
        name="ada",
    )(c_all, w_ada, b_ada)


def _normmod_kernel(x_ref, sc_ref, sh_ref, g_ref, o_ref):
    x = x_ref[...]
    ms = jnp.mean(x * x, axis=-1, keepdims=True)
    xn = x * lax.rsqrt(ms + EPS) * g_ref[...]
    o_ref[...] = (xn * (1.0 + sc_ref[...]) + sh_ref[...]).astype(o_ref.dtype)


def _normmod(x, sc, sh, g, bb, tb):
    b, t, d = x.shape
    return pl.pallas_call(
        _normmod_kernel,
        out_shape=jax.ShapeDtypeStruct((b, t, d), BF16),
        grid=(b // bb, t // tb),
        in_specs=[pl.BlockSpec((bb, tb, d), lambda i, j: (i, j, 0)),
                  pl.BlockSpec((bb, 1, d), lambda i, j: (i, 0, 0)),
                  pl.BlockSpec((bb, 1, d), lambda i, j: (i, 0, 0)),
                  pl.BlockSpec((1, 1, d), lambda i, j: (0, 0, 0))],
        out_specs=pl.BlockSpec((bb, tb, d), lambda i, j: (i, j, 0)),
        compiler_params=_cparams(("arbitrary", "arbitrary")),
        name="normmod",
    )(x, sc, sh, g)


def _proj_plain_kernel(h_ref, w_ref, *o_refs, scale):
    z = _dot_nt(h_ref[...], w_ref[...].astype(BF16))
    if scale != 1.0:
        z = z * scale
    for o_ref in o_refs:
        o_ref[...] = z.astype(o_ref.dtype)


def _proj_rope_kernel(h_ref, w_ref, cos_ref, sup_ref, sdn_ref, *o_refs, scale):
    z = _dot_nt(h_ref[...], w_ref[...].astype(BF16))
    tn = z.shape[1]
    reps = tn // LANES

    def wide(ref):
        t = ref[...]
        return t if reps == 1 else jnp.concatenate([t] * reps, axis=1)

    up = pltpu.roll(z, tn - ROT_DIM // 2, 1)
    dn = pltpu.roll(z, ROT_DIM // 2, 1)
    r = z * wide(cos_ref) + up * wide(sup_ref) + dn * wide(sdn_ref)
    for o_ref in o_refs:
        if o_ref.dtype == BF16 and scale != 1.0:
            o_ref[...] = (r * scale).astype(BF16)
        else:
            o_ref[...] = r.astype(o_ref.dtype)


def _proj_forget_kernel(h_ref, w_ref, lbl_ref, lf_ref, kk_ref):
    z = _dot_nt(h_ref[...], w_ref[...].astype(BF16))
    lbl = lbl_ref[...]
    mx = jnp.max(lbl, axis=0, keepdims=True)
    e = jnp.exp(lbl - mx)
    lb = e[0:1, :] / jnp.sum(e, axis=0, keepdims=True)
    f = lb + (1.0 - lb) * _sigmoid(z)
    lf_ref[...] = jnp.log(f) * LOG2E
    kk_ref[...] = 1.0 - f


def _proj(kind, h, wspec, tm, tn, extra=(), out_dtypes=(F32,), scale=1.0):
    w, c0, n = wspec
    m, k = h.shape
    assert c0 % SUBLANES == 0 and n % tn == 0
    in_specs = [pl.BlockSpec((tm, k), lambda i, j: (i, 0)),
                pl.BlockSpec((pl.Element(tn), pl.Element(k)),
                             lambda i, j: (pl.multiple_of(c0 + j * tn, SUBLANES), 0))]
    if kind == "rope":
        kern = functools.partial(_proj_rope_kernel, scale=scale)
        in_specs += [pl.BlockSpec((tm, LANES), lambda i, j: (i, 0))] * 3
    elif kind == "forget":
        kern = _proj_forget_kernel
        in_specs += [pl.BlockSpec((extra[0].shape[0], tn), lambda i, j: (0, j))]
    else:
        kern = functools.partial(_proj_plain_kernel, scale=scale)
    outs = tuple(jax.ShapeDtypeStruct((m, n), dt) for dt in out_dtypes)
    out_specs = tuple(pl.BlockSpec((tm, tn), lambda i, j: (i, j)) for _ in out_dtypes)
    res = pl.pallas_call(
        kern,
        out_shape=outs,
        grid=(m // tm, n // tn),
        in_specs=in_specs,
        out_specs=out_specs,
        compiler_params=_cparams(("arbitrary", "arbitrary")),
        name="proj_" + kind,
    )(h, w, *extra)
    return res


def _proj_v_kernel(h_ref, w_ref, vf_ref, vt_ref, *, wn):
    z = _dot_nt(h_ref[...], w_ref[...].astype(BF16))
    vf_ref[...] = z
    zt = z.T
    ones = jnp.ones((BF16_ROWS, wn), BF16)
    for u in range(z.shape[0] // wn):
        blk = zt[:, u * wn:(u + 1) * wn].astype(BF16)
        for g in range(N_KV_HEADS):
            vt_ref[u, g * VT_ROWS:g * VT_ROWS + HEAD_DIM, :] = blk[g * HEAD_DIM:(g + 1) * HEAD_DIM]
            vt_ref[u, g * VT_ROWS + HEAD_DIM:(g + 1) * VT_ROWS, :] = ones


def _proj_v(h, wspec, tm, wn):
    w, c0, n = wspec
    m, k = h.shape
    assert c0 % SUBLANES == 0
    return pl.pallas_call(
        functools.partial(_proj_v_kernel, wn=wn),
        out_shape=(jax.ShapeDtypeStruct((m, n), F32),
                   jax.ShapeDtypeStruct((m // wn, N_KV_HEADS * VT_ROWS, wn), BF16)),
        grid=(m // tm,),
        in_specs=[pl.BlockSpec((tm, k), lambda i: (i, 0)),
                  pl.BlockSpec((pl.Element(n), pl.Element(k)), lambda i: (c0, 0))],
        out_specs=(pl.BlockSpec((tm, n), lambda i: (i, 0)),
                   pl.BlockSpec((tm // wn, N_KV_HEADS * VT_ROWS, wn), lambda i: (i, 0, 0))),
        compiler_params=_cparams(("arbitrary",)),
        name="proj_v",
    )(h, w)


def _dsa_kernel(*refs, tq, tk, t_new, l_cache, pos0, topk, has_cache):
    if has_cache:
        (qi_ref, wi_ref, q_ref, kin_ref, kn_ref, vtn_ref, cki_ref, ck_ref, cv_ref, o_ref,
         sc_ref, qit_ref, wt_ref, qt_ref, acc_ref, s_ref, p_ref, tau_ref) = refs
    else:
        (qi_ref, wi_ref, q_ref, kin_ref, kn_ref, vtn_ref, o_ref,
         sc_ref, qit_ref, wt_ref, qt_ref, acc_ref, s_ref, p_ref, tau_ref) = refs
        cki_ref = ck_ref = cv_ref = None
    i = pl.program_id(1)
    ntc = l_cache // tk
    wn = min(tk, t_new)
    nq = GROUP * tq
    q0 = pos0 + i * tq
    lane_q = lax.broadcasted_iota(jnp.int32, (1, tq), 1)
    qend = (((q0 + lane_q) >> 6) + 1) << 6
    last_end = (((q0 + tq - 1) >> 6) + 1) << 6
    nvis_new = jnp.minimum(last_end - pos0, t_new)
    ntn = (nvis_new + wn - 1) // wn
    ntiles = ntc + ntn

    def tpose(x):
        xf = x.astype(F32)
        if tq < LANES:
            xf = jnp.concatenate([xf, jnp.zeros((LANES - tq, LANES), F32)], axis=0)
        xt = xf.T
        return xt[:, :tq] if tq < LANES else xt

    qi_blk = qi_ref[0]
    for h in range(IDX_HEADS):
        qit_ref[:, h * tq:(h + 1) * tq] = tpose(qi_blk[:, h * IDX_DIM:(h + 1) * IDX_DIM]).astype(BF16)
    wt_ref[...] = tpose(wi_ref[0])
    q_blk = q_ref[0]
    for g in range(N_KV_HEADS):
        for a in range(GROUP):
            hh = GROUP * g + a
            qt_ref[g, :, a * tq:(a + 1) * tq] = tpose(q_blk[:, hh * HEAD_DIM:(hh + 1) * HEAD_DIM]).astype(BF16)

    def score_tile(ki_tile, kpos0, w):
        acc = jnp.zeros((w, tq), F32)
        hpd = 2 if 2 * tq % LANES == 0 else 1
        for hp in range(IDX_HEADS // hpd):
            lg = _dot(ki_tile, qit_ref[:, hpd * hp * tq:(hpd * hp + hpd) * tq])
            for e in range(hpd):
                h = hpd * hp + e
                acc = acc + wt_ref[h:h + 1, :] * jnp.maximum(lg[:, e * tq:(e + 1) * tq], 0.0)
        kpos = kpos0 + lax.broadcasted_iota(jnp.int32, (w, 1), 0)
        vis = kpos < qend
        s = jnp.where(vis, acc, NEG_INF)
        smin = _col_reduce(jnp.min, jnp.where(vis, acc, POS_INF))
        smax = _col_reduce(jnp.max, s)
        return s, smax, smin

    def p1_cache(t, carry):
        mx, mn = carry
        r0 = pl.multiple_of(t * tk, tk)
        s, smax, smin = score_tile(cki_ref[0, pl.ds(r0, tk), :].astype(BF16), r0, tk)
        sc_ref[t] = s
        return jnp.maximum(mx, smax), jnp.minimum(mn, smin)

    def p1_new(j, carry):
        mx, mn = carry
        r0 = pl.multiple_of(j * wn, wn)
        s, smax, smin = score_tile(kin_ref[0, pl.ds(r0, wn), :], pos0 + r0, wn)
        if wn == tk:
            sc_ref[ntc + j] = s
        else:
            sc_ref[ntc + j] = jnp.full((tk, tq), NEG_INF, F32)
            sc_ref[ntc + j, 0:wn, :] = s
        return jnp.maximum(mx, smax), jnp.minimum(mn, smin)

    carry = (jnp.full((1, tq), NEG_INF, F32), jnp.full((1, tq), POS_INF, F32))
    if has_cache:
        carry = lax.fori_loop(0, ntc, p1_cache, carry)
    mx, mn = lax.fori_loop(0, ntn, p1_new, carry)

    nvis = jnp.clip(qend - pos0, 0, t_new)
    if has_cache:
        nvis = nvis + jnp.minimum(qend, l_cache)
    act0 = jnp.where(nvis > topk, 1.0, 0.0)
    kf = float(topk)
    cu = min(tk, COUNT_UNIT)
    upt = tk // cu
    n_units = ntc * upt + (nvis_new + cu - 1) // cu

    def bis_cond(carry):
        return jnp.logical_and(carry[0] < 400, carry[1] > 0)

    def unit(u):
        return sc_ref.at[u // upt, pl.ds(pl.multiple_of((u % upt) * cu, cu), cu), :]

    def count_where(pred):
        def body(u, cnt):
            ind = jnp.where(pred(unit(u)[...]), 1.0, 0.0)
            return cnt + jnp.sum(ind.reshape(cu // CNT_ROWS, CNT_ROWS, tq), axis=0)

        cnt = lax.fori_loop(0, n_units, body, jnp.zeros((CNT_ROWS, tq), F32))
        return jnp.sum(cnt, axis=0, keepdims=True)

    def bis_body(carry):
        it, _, lo, hi, tau, act, tied = carry
        mid = lo * 0.5 + hi * 0.5
        c = count_where(lambda s: s >= mid)
        found = jnp.where(c == kf, act, 0.0)
        stuck = jnp.where(jnp.logical_or(mid <= lo, mid >= hi), act, 0.0) * (1.0 - found)
        above = jnp.where(c > kf, act, 0.0) * (1.0 - stuck)
        below = act * (1.0 - above) * (1.0 - found) * (1.0 - stuck)
        tau = jnp.where(found > 0, mid, tau)
        tau = jnp.where(stuck > 0, lo, tau)
        lo = jnp.where(above > 0, mid, lo)
        hi = jnp.where(below > 0, mid, hi)
        nact = act * (1.0 - found) * (1.0 - stuck)
        return it + 1, (jnp.max(nact) > 0).astype(jnp.int32), lo, hi, tau, nact, jnp.maximum(tied, stuck)

    init = (jnp.int32(0), (jnp.max(act0) > 0).astype(jnp.int32), mn, mx, mn, act0, jnp.zeros((1, tq), F32))
    res = lax.while_loop(bis_cond, lambda carry: bis_body(bis_body(carry)), init)
    tau_ref[...] = res[4]
    tied = res[6]

    @pl.when(jnp.max(tied) > 0)
    def _():
        lo_t = tau_ref[...]

        def tmin_body(u, acc):
            s = unit(u)[...]
            return jnp.minimum(acc, _col_reduce(jnp.min, jnp.where(s >= lo_t, s, POS_INF)))

        tstar = lax.fori_loop(0, n_units, tmin_body, jnp.full((1, tq), POS_INF, F32))
        tstar = jnp.where(tied > 0, tstar, lo_t)
        need = kf - count_where(lambda s: s > tstar)
        ri = lax.broadcasted_iota(jnp.int32, (cu, cu), 0)
        ci = lax.broadcasted_iota(jnp.int32, (cu, cu), 1)
        tri = jnp.where(ri >= ci, 1.0, 0.0).astype(BF16)

        def strike_body(u, run):
            s = unit(u)[...]
            eq = jnp.where(jnp.logical_and(s == tstar, tied > 0), 1.0, 0.0)
            rank = run + _dot(tri, eq.astype(BF16))
            unit(u)[...] = jnp.where(jnp.logical_and(eq > 0, rank > need), NEG_INF, s)
            return rank[cu - 1:cu, :]

        lax.fori_loop(0, n_units, strike_body, jnp.zeros((1, tq), F32))
        tau_ref[...] = tstar

    tau = tau_ref[...]

    acc_ref[...] = jnp.zeros(acc_ref.shape, F32)

    def attend(t, k_of_g, vt_of_g, w, ms):
        s_idx = sc_ref[t]
        if w < tk:
            s_idx = s_idx[0:w]
        bias = jnp.where(s_idx >= tau, 0.0, NEG_INF)
        bias4 = jnp.concatenate([bias] * GROUP, axis=1)
        def qk(g):
            s = _dot(k_of_g(g), qt_ref[g]) + bias4
            s_ref[g % 2, 0:w] = s
            return _col_reduce(jnp.max, s)

        out = []
        tile_max = qk(0)
        for g in range(N_KV_HEADS):
            slot = g % 2
            next_max = qk(g + 1) if g + 1 < N_KV_HEADS else None
            m_old = ms[g]
            m_new = jnp.maximum(m_old, tile_max)
            m_safe = jnp.where(m_new == NEG_INF, 0.0, m_new)
            alpha = jnp.exp2(m_old - m_safe)
            p_ref[slot, 0:w] = jnp.exp2(s_ref[slot, 0:w] - m_safe).astype(BF16)
            acc_ref[g] = acc_ref[g] * alpha + _dot(vt_of_g(g), p_ref[slot, 0:w])
            out.append(m_new)
            tile_max = next_max
        return tuple(out)

    def p3_cache(t, ms):
        r0 = pl.multiple_of(t * tk, tk)

        def vt(g):
            vg = cv_ref[0, pl.ds(r0 * N_KV_HEADS + g, tk, stride=N_KV_HEADS), :]
            return jnp.concatenate([vg.T, jnp.ones((BF16_ROWS, tk), F32)], axis=0).astype(BF16)

        return attend(
            t, lambda g: ck_ref[0, pl.ds(r0 * N_KV_HEADS + g, tk, stride=N_KV_HEADS), :].astype(BF16), vt, tk, ms)

    def p3_new(j, ms):
        r0 = pl.multiple_of(j * wn, wn)
        return attend(ntc + j,
                      lambda g: kn_ref[0, pl.ds(r0, wn), g * HEAD_DIM:(g + 1) * HEAD_DIM],
                      lambda g: vtn_ref[0, j, g * VT_ROWS:(g + 1) * VT_ROWS, :], wn, ms)

    ms = tuple(jnp.full((1, nq), NEG_INF, F32) for _ in range(N_KV_HEADS))
    if has_cache:
        ms = lax.fori_loop(0, ntc, p3_cache, ms)
    lax.fori_loop(0, ntn, p3_new, ms)

    for g in range(N_KV_HEADS):
        acc = acc_ref[g]
        og = (acc[0:HEAD_DIM] / acc[HEAD_DIM:HEAD_DIM + 1]).T
        for a in range(GROUP):
            hh = GROUP * g + a
            o_ref[0, :, hh * HEAD_DIM:(hh + 1) * HEAD_DIM] = og[a * tq:(a + 1) * tq, :].astype(o_ref.dtype)


def _dsa(qi, wi, q, kin, kn, vtn, cache, pos0, tq, tk):
    b, t, _ = q.shape
    has_cache = cache is not None
    l_cache = cache[0].shape[1] if has_cache else 0
    topk = min(TOPK_MAX, (l_cache + t) // 4)
    wn = min(tk, t)
    ntn = t // wn
    nt = l_cache // tk + ntn
    kvd = N_KV_HEADS * HEAD_DIM
    nq = GROUP * tq
    res = pl.Buffered(1) if b == 1 else None
    in_specs = [pl.BlockSpec((1, tq, IDX_HEADS * IDX_DIM), lambda bb, i: (bb, i, 0)),
                pl.BlockSpec((1, tq, LANES), lambda bb, i: (bb, i, 0)),
                pl.BlockSpec((1, tq, N_HEADS * HEAD_DIM), lambda bb, i: (bb, i, 0)),
                pl.BlockSpec((1, t, IDX_DIM), lambda bb, i: (bb, 0, 0), pipeline_mode=res),
                pl.BlockSpec((1, t, kvd), lambda bb, i: (bb, 0, 0), pipeline_mode=res),
                pl.BlockSpec((1, ntn, N_KV_HEADS * VT_ROWS, wn), lambda bb, i: (bb, 0, 0, 0), pipeline_mode=res)]
    args = [qi, wi, q, kin, kn, vtn]
    if has_cache:
        in_specs += [pl.BlockSpec((1, l_cache, IDX_DIM), lambda bb, i: (bb, 0, 0)),
                     pl.BlockSpec((1, l_cache * N_KV_HEADS, HEAD_DIM), lambda bb, i: (bb, 0, 0)),
                     pl.BlockSpec((1, l_cache * N_KV_HEADS, HEAD_DIM), lambda bb, i: (bb, 0, 0))]
        args += list(cache)
    kern = functools.partial(_dsa_kernel, tq=tq, tk=tk, t_new=t, l_cache=l_cache, pos0=pos0, topk=topk,
                             has_cache=has_cache)
    return pl.pallas_call(
        kern,
        out_shape=jax.ShapeDtypeStruct((b, t, N_HEADS * HEAD_DIM), BF16),
        grid=(b, t // tq),
        in_specs=in_specs,
        out_specs=pl.BlockSpec((1, tq, N_HEADS * HEAD_DIM), lambda bb, i: (bb, i, 0)),
        scratch_shapes=[pltpu.VMEM((nt, tk, tq), F32),
                        pltpu.VMEM((IDX_DIM, IDX_HEADS * tq), BF16),
                        pltpu.VMEM((LANES, tq), F32),
                        pltpu.VMEM((N_KV_HEADS, HEAD_DIM, nq), BF16),
                        pltpu.VMEM((N_KV_HEADS, VT_ROWS, nq), F32),
                        pltpu.VMEM((2, tk, nq), F32),
                        pltpu.VMEM((2, tk, nq), BF16),
                        pltpu.VMEM((1, tq), F32)],
        compiler_params=_cparams(("arbitrary", "arbitrary")),
        name="dsa",
    )(*args)


def _hgrn_kernel(*refs, c, nchunk, nh, has_state):
    if has_state:
        q_ref, lf_ref, kk_ref, v_ref, hg_ref, nw_ref, s0_ref, o_ref, sout_ref, st_ref, g_ref = refs
    else:
        q_ref, lf_ref, kk_ref, v_ref, hg_ref, nw_ref, o_ref, sout_ref, st_ref, g_ref = refs
        s0_ref = None
    r = pl.program_id(2)

    @pl.when(r == 0)
    def _():
        for hh in range(nh):
            if has_state:
                st_ref[hh] = s0_ref[0, hh].T
            else:
                st_ref[hh] = jnp.zeros((HG_DV, HG_DK), F32)

    ri = lax.broadcasted_iota(jnp.int32, (c, c), 0)
    ci = lax.broadcasted_iota(jnp.int32, (c, c), 1)
    tri = jnp.where(ri >= ci, 1.0, 0.0).astype(BF16)
    nsub = c // SUB_BLOCK
    pair = lax.broadcasted_iota(jnp.int32, (SUB_BLOCK * SUB_BLOCK, HG_DK), 0)
    cap = jnp.where((pair % SUB_BLOCK) >= (pair // SUB_BLOCK), 0.0, NEG_INF)
    ones_w = jnp.ones((HG_DK, LANES), BF16)

    def rep_rows(row):
        return jnp.concatenate(
            [jnp.broadcast_to(row(sg), (SUB_BLOCK, HG_DK)) for sg in range(SUB_BLOCK)], axis=0)

    def cumulative(base):
        lf = lf_ref[pl.ds(base, c), :]
        l1 = lf.astype(BF16)
        r1 = lf - l1.astype(F32)
        l2 = r1.astype(BF16)
        l3 = (r1 - l2.astype(F32)).astype(BF16)
        return _dot(tri, l1) + _dot(tri, l2) + _dot(tri, l3)

    def stage_a(base, hh, g_all):
        cs = slice(hh * HG_DK, (hh + 1) * HG_DK)
        G = g_all[:, cs]
        kk = kk_ref[pl.ds(base, c), cs]
        v = v_ref[pl.ds(base, c), cs]
        g_ref[0, hh] = G
        g_ref[1, hh] = kk
        g_ref[2, hh] = v
        return dict(cs=cs, G=G, hh=hh, q=q_ref[pl.ds(base, c), cs], kk=kk, v=v)

    def stage_b(hh, d):
        q, kk, v, G = d["q"], d["kk"], d["v"], d["G"]
        st = st_ref[hh]
        vb = v.astype(BF16)
        d["vb"] = vb
        d["o_inter"] = _dot_nt((q * jnp.exp2(G)).astype(BF16), st.astype(BF16))
        d["A"], diag = [], []
        for i in range(nsub):
            lo_, hi_ = i * SUB_BLOCK, (i + 1) * SUB_BLOCK
            qi_ = q[lo_:hi_]
            Gi = G[lo_:hi_]
            if i > 0:
                Gb = G[lo_ - 1:lo_]
                qt = qi_ * jnp.exp2(Gi - Gb)
                kt = kk[:lo_] * jnp.exp2(Gb - G[:lo_])
                a_blk = _dot_nt(qt.astype(BF16), kt.astype(BF16))
                d["A"].append(jnp.concatenate([a_blk, jnp.zeros((SUB_BLOCK, c - lo_), F32)], axis=1))
            qrep = jnp.concatenate([qi_] * SUB_BLOCK, axis=0)
            grep = jnp.concatenate([Gi] * SUB_BLOCK, axis=0)
            kk_rep = rep_rows(lambda sg: g_ref[1, hh, lo_ + sg:lo_ + sg + 1, :])
            g_rep = rep_rows(lambda sg: g_ref[0, hh, lo_ + sg:lo_ + sg + 1, :])
            diag.append((qrep * kk_rep * jnp.exp2(jnp.minimum(grep - g_rep, cap))).astype(BF16))
        rs = _dot(jnp.concatenate(diag, axis=0) if nsub > 1 else diag[0], ones_w)
        pairs = SUB_BLOCK * SUB_BLOCK
        d["rs"] = [rs[i * pairs:(i + 1) * pairs] for i in range(nsub)]
        Gl = G[c - 1:c]
        kdec = kk * jnp.exp2(Gl - G)
        st_ref[hh] = st * jnp.exp2(Gl) + _dot_tn(vb, kdec.astype(BF16))

    def stage_c(base, d):
        parts = []
        o_base = d["o_inter"]
        if nsub > 1:
            a_low = jnp.concatenate([jnp.zeros((SUB_BLOCK, c), F32)] + d["A"], axis=0)
            o_base = o_base + _dot(a_low.astype(BF16), d["vb"])
        for i in range(nsub):
            lo_, hi_ = i * SUB_BLOCK, (i + 1) * SUB_BLOCK
            oi = o_base[lo_:hi_]
            v_rep = rep_rows(lambda sg: g_ref[2, d["hh"], lo_ + sg:lo_ + sg + 1, :])
            contrib = (d["rs"][i] * v_rep).reshape(SUB_BLOCK, SUB_BLOCK, HG_DV)
            parts.append(oi + jnp.sum(contrib, axis=0))
        o = jnp.concatenate(parts, axis=0) if nsub > 1 else parts[0]
        on = o * lax.rsqrt(jnp.mean(o * o, axis=1, keepdims=True) + EPS) * nw_ref[...]
        hg = hg_ref[pl.ds(base, c), d["cs"]]
        o_ref[pl.ds(base, c), d["cs"]] = (on * (hg * _sigmoid(hg))).astype(o_ref.dtype)

    def chunk(n, carry):
        base = pl.multiple_of(n * c, c)
        g_all = cumulative(base)
        heads = [stage_a(base, hh, g_all) for hh in range(nh)]
        for hh, d in enumerate(heads):
            stage_b(hh, d)
        for d in heads:
            stage_c(base, d)
        return carry

    lax.fori_loop(0, nchunk, chunk, 0)

    @pl.when(r == pl.num_programs(2) - 1)
    def _():
        for hh in range(nh):
            sout_ref[0, hh] = st_ref[hh].T


def _hgrn(hq, lf, kk, hv, hg, nw, s0, b, t, rb, nh):
    c = min(CHUNK, t)
    nchunk = rb // c
    nr = t // rb
    has_state = s0 is not None
    blk = pl.BlockSpec((rb, nh * HG_DK), lambda bb, h, r: (bb * nr + r, h))
    sblk = pl.BlockSpec((1, nh, HG_DK, HG_DV), lambda bb, h, r: (bb, h, 0, 0))
    in_specs = [blk, blk, blk, blk, blk, pl.BlockSpec((1, HG_DV), lambda bb, h, r: (0, 0))]
    args = [hq, lf, kk, hv, hg, nw]
    if has_state:
        in_specs.append(sblk)
        args.append(s0)
    kern = functools.partial(_hgrn_kernel, c=c, nchunk=nchunk, nh=nh, has_state=has_state)
    return pl.pallas_call(
        kern,
        out_shape=(jax.ShapeDtypeStruct((b * t, HG_HEADS * HG_DV), BF16),
                   jax.ShapeDtypeStruct((b, HG_HEADS, HG_DK, HG_DV), F32)),
        grid=(b, HG_HEADS // nh, nr),
        in_specs=in_specs,
        out_specs=(blk, sblk),
        scratch_shapes=[pltpu.VMEM((nh, HG_DV, HG_DK), F32), pltpu.VMEM((3, nh, c, HG_DK), F32)],
        compiler_params=_cparams(("arbitrary", "arbitrary", "arbitrary")),
        name="hgrn",
    )(*args)


def _merge_out_kernel(x_ref, oa_ref, oh_ref, ga_ref, gb_ref, w_ref, g1_ref, sc_ref, sh_ref, nw_ref,
                      x1_ref, h2_ref):
    bb, tb, d = x_ref.shape
    merged = (_sigmoid(ga_ref[...].astype(F32)) * oa_ref[...].astype(F32)
              + _sigmoid(gb_ref[...].astype(F32)) * oh_ref[...].astype(F32))
    y = _dot(merged.astype(BF16), w_ref[...]).reshape(bb, tb, d)
    x1 = x_ref[...] + g1_ref[...] * y
    x1_ref[...] = x1
    ms = jnp.mean(x1 * x1, axis=-1, keepdims=True)
    xn = x1 * lax.rsqrt(ms + EPS) * nw_ref[...]
    h2 = xn * (1.0 + sc_ref[...]) + sh_ref[...]
    h2_ref[...] = h2.reshape(bb * tb, d).astype(BF16)


def _merge_out(x, oa, oh, ga, gb, w_out, g1, sc2, sh2, nw2, bb, tb):
    b, t, d = x.shape
    tm = bb * tb
    nt = t // tb

    def row(i, j):
        return (i * nt + j, 0)

    def mod(i, j):
        return (i, 0, 0)

    x3 = pl.BlockSpec((bb, tb, d), lambda i, j: (i, j, 0))
    r2 = pl.BlockSpec((tm, d), row)
    return pl.pallas_call(
        _merge_out_kernel,
        out_shape=(jax.ShapeDtypeStruct((b, t, d), F32), jax.ShapeDtypeStruct((b * t, d), BF16)),
        grid=(b // bb, nt),
        in_specs=[x3, r2, r2, r2, r2,
                  pl.BlockSpec((d, d), lambda i, j: (0, 0)),
                  pl.BlockSpec((bb, 1, d), mod), pl.BlockSpec((bb, 1, d), mod), pl.BlockSpec((bb, 1, d), mod),
                  pl.BlockSpec((1, 1, d), lambda i, j: (0, 0, 0))],
        out_specs=(x3, r2),
        compiler_params=_cparams(("arbitrary", "arbitrary")),
        name="merge_out",
    )(x, oa, oh, ga, gb, w_out, g1, sc2, sh2, nw2)


def _mlp_kernel(h_ref, wu_ref, wd_ref, x1_ref, g2_ref, fw_ref, y_ref, acc_ref):
    f = pl.program_id(2)

    @pl.when(f == 0)
    def _():
        acc_ref[...] = jnp.zeros(acc_ref.shape, F32)

    u = jnp.maximum(_dot(h_ref[...], wu_ref[...]), 0.0)
    acc_ref[...] += _dot((u * u).astype(BF16), wd_ref[...])

    @pl.when(f == pl.num_programs(2) - 1)
    def _():
        bb, tb, d = x1_ref.shape
        x2 = x1_ref[...] + g2_ref[...] * acc_ref[...].reshape(bb, tb, d)
        ms = jnp.mean(x2 * x2, axis=-1, keepdims=True)
        y_ref[...] = x2 * lax.rsqrt(ms + EPS) * fw_ref[...]


def _mlp(h2, w_up, w_down, x1, g2, fw, bb, tb, tf):
    b, t, d = x1.shape
    dff = w_up.shape[1]
    tm = bb * tb
    nt = t // tb
    x3 = pl.BlockSpec((bb, tb, d), lambda i, j, f: (i, j, 0))
    return pl.pallas_call(
        _mlp_kernel,
        out_shape=jax.ShapeDtypeStruct((b, t, d), F32),
        grid=(b // bb, nt, dff // tf),
        in_specs=[pl.BlockSpec((tm, d), lambda i, j, f: (i * nt + j, 0)),
                  pl.BlockSpec((d, tf), lambda i, j, f: (0, f)),
                  pl.BlockSpec((tf, d), lambda i, j, f: (f, 0)),
                  x3,
                  pl.BlockSpec((bb, 1, d), lambda i, j, f: (i, 0, 0)),
                  pl.BlockSpec((1, 1, d), lambda i, j, f: (0, 0, 0))],
        out_specs=x3,
        scratch_shapes=[pltpu.VMEM((tm, d), F32)],
        compiler_params=_cparams(("arbitrary", "arbitrary", "arbitrary")),
        name="mlp",
    )(h2, w_up, w_down, x1, g2, fw)


def _rope_tables(pos):
    half = ROT_DIM // 2
    inv_freq = ROPE_THETA ** (-(jnp.arange(half, dtype=F32) * (2.0 / ROT_DIM)))
    ang = pos.astype(F32)[:, None] * inv_freq[None, :]
    cos, sin = jnp.cos(ang), jnp.sin(ang)
    n = pos.shape[0]
    ones = jnp.ones((n, LANES - ROT_DIM), F32)
    zeros = jnp.zeros((n, LANES - ROT_DIM), F32)
    zh = jnp.zeros((n, half), F32)
    c_t = jnp.concatenate([cos, cos, ones], axis=1)
    s_up = jnp.concatenate([-sin, zh, zeros], axis=1)
    s_dn = jnp.concatenate([zh, sin, zeros], axis=1)
    return c_t, s_up, s_dn


def _trunk(x, mod, pos0, past, wts, blocks):
    (norm1_w, w_parts, lb_logits, hg_norm_w, w_out, norm2_w, w_up, w_down, final_w) = wts
    b, t, d = x.shape
    bb, tb, tm_proj, tq, tk, rb, mlp_bb, mlp_tb = blocks
    m = [mod[:, i:i + 1, :] for i in range(6)]
    sh1, sc1, g1, sh2, sc2, g2 = m
    h = _normmod(x, sc1, sh1, norm1_w.reshape(1, 1, d), bb, tb).reshape(b * t, d)

    pos = pos0 + jnp.arange(t, dtype=jnp.int32)
    tabs = tuple(jnp.tile(tb_, (b, 1)) for tb_ in _rope_tables(pos))
    wq, wk, wv, wqi, wki, wwi, whq, whf, whi, whg, wga, wgb = w_parts
    tm = min(tm_proj, b * t)
    wn = min(tk, t)
    (q_bf,) = _proj("rope", h, wq, tm, PROJ_TN, tabs, (BF16,), scale=HEAD_DIM ** -0.5 * LOG2E)
    k_f, k_bf = _proj("rope", h, wk, tm, PROJ_TN, tabs, (F32, BF16))
    v_f, vt_bf = _proj_v(h, wv, tm, wn)
    (qi_bf,) = _proj("rope", h, wqi, tm, PROJ_TN, tabs, (BF16,))
    ki_f, ki_bf = _proj("rope", h, wki, tm, LANES, tabs, (F32, BF16))
    (wi_f,) = _proj("plain", h, wwi, tm, LANES, (), (F32,), scale=IDX_HEADS ** -0.5 * IDX_DIM ** -0.5)
    (hq,) = _proj("plain", h, whq, tm, PROJ_TN)
    lf, kk = _proj("forget", h, whf, tm, PROJ_TN, (lb_logits,), (F32, F32))
    (hi,) = _proj("plain", h, whi, tm, PROJ_TN)
    (hg,) = _proj("plain", h, whg, tm, PROJ_TN)
    (ga,) = _proj("plain", h, wga, tm, PROJ_TN, (), (BF16,))
    (gb,) = _proj("plain", h, wgb, tm, PROJ_TN, (), (BF16,))

    kvd = N_KV_HEADS * HEAD_DIM
    r3 = lambda a: a.reshape(b, t, a.shape[-1])
    if past is None:
        cache, s0 = None, None
    else:
        ck, cv, cki, s0 = past
        lc = ck.shape[1]
        cache = (cki, ck.reshape(b, lc * N_KV_HEADS, HEAD_DIM), cv.reshape(b, lc * N_KV_HEADS, HEAD_DIM))
    vtn = vt_bf.reshape(b, t // wn, N_KV_HEADS * VT_ROWS, wn)
    o_attn = _dsa(r3(qi_bf), r3(wi_f), r3(q_bf), r3(ki_bf), r3(k_bf), vtn, cache, pos0, tq, tk)
    o_hg, s_new = _hgrn(hq, lf, kk, hi, hg, hg_norm_w.reshape(1, HG_DV), s0, b, t, rb, 8)

    x1, h2 = _merge_out(x, o_attn.reshape(b * t, d), o_hg, ga, gb, w_out, g1, sc2, sh2,
                        norm2_w.reshape(1, 1, d), bb, tb)
    y = _mlp(h2, w_up, w_down, x1, g2, final_w.reshape(1, 1, d), mlp_bb, mlp_tb, 512)
    return (y, k_f.reshape(b, t, N_KV_HEADS, HEAD_DIM), v_f.reshape(b, t, N_KV_HEADS, HEAD_DIM),
            ki_f.reshape(b, t, IDX_DIM), s_new)


def kernel(x_prompt, x_sample, cache_k, cache_v, cache_ki, state_hgrn, c_prompt, c_sample, w_ada, b_ada, norm1_w,
           w_in, hg_lb_logits, hg_norm_w, w_out, norm2_w, w_up, w_down, final_norm_w):
    depth = w_in.shape[0]
    assert depth == 1
    d = x_prompt.shape[-1]
    bp, tp, _ = x_prompt.shape
    bs, ts, _ = x_sample.shape
    past_len = cache_k.shape[2]

    c_all = jnp.concatenate([c_prompt, c_sample], axis=0)
    nrow = c_all.shape[0]
    pad = (-nrow) % SUBLANES
    c_all = jnp.pad(c_all, ((0, pad), (0, 0)))
    mod = _ada(c_all, w_ada[0], b_ada[0].reshape(1, -1)).reshape(nrow + pad, 6, d)

    sizes = (N_HEADS * HEAD_DIM, N_KV_HEADS * HEAD_DIM, N_KV_HEADS * HEAD_DIM, IDX_HEADS * IDX_DIM, IDX_DIM,
             IDX_HEADS, HG_HEADS * HG_DK, HG_HEADS * HG_DK, HG_HEADS * HG_DV, HG_HEADS * HG_DV, d, d)
    offs = np.concatenate([[0], np.cumsum(sizes)])
    w_t = jnp.transpose(w_in[0])
    w_parts = [(w_t, int(offs[i]), sz + (-sz) % LANES) for i, sz in enumerate(sizes)]
    wts = (norm1_w[0], tuple(w_parts), hg_lb_logits.astype(F32), hg_norm_w[0], w_out[0].astype(BF16), norm2_w[0],
           w_up[0].astype(BF16), w_down[0].astype(BF16), final_norm_w)

    yp, kp, vp, kip, sp = _trunk(x_prompt, mod[:bp], 0, None, wts, (1, 256, 2048, 128, 1024, 512, 1, 512))
    past = (cache_k[0], cache_v[0], cache_ki[0], state_hgrn[0])
    ys, ks, vs, kis, ss = _trunk(x_sample, mod[bp:bp + bs], past_len, past, wts,
                                 (bs // 2, ts, bs * ts, ts, 512, ts, bs, ts))
    return (yp, ys, kp[None], vp[None], kip[None], sp[None], ks[None], vs[None], kis[None], ss[None])
```

```python
import functools
import math

import jax
import jax.numpy as jnp
import numpy as np
from jax import lax
from jax.experimental import pallas as pl
from jax.experimental.pallas import tpu as pltpu

CHUNK = 64
N_HEADS = 16
HEAD_DIM = 128
N_KV_HEADS = 4
GROUP = N_HEADS // N_KV_HEADS
ROT_DIM = HEAD_DIM // 4
ROPE_THETA = 500000.0
IDX_HEADS = 16
IDX_DIM = 128
TOPK_MAX = 256
HG_HEADS = 16
HG_DK = 128
HG_DV = 128
EPS = 1e-6
LANES = 128
SUBLANES = 8
BF16_ROWS = 16
SUB_BLOCK = 8
PROJ_TN = 512
COUNT_UNIT = 512
CNT_ROWS = 64
VT_ROWS = HEAD_DIM + BF16_ROWS
VMEM_LIMIT = 56 * 1024 * 1024
NEG_INF = float("-inf")
POS_INF = float("inf")
LOG2E = math.log2(math.e)

F32 = jnp.float32
BF16 = jnp.bfloat16


def _cparams(sem):
    return pltpu.CompilerParams(dimension_semantics=sem, vmem_limit_bytes=VMEM_LIMIT)


def _dot_nt(a, b):
    return lax.dot_general(a, b, (((1,), (1,)), ((), ())), preferred_element_type=F32)


def _dot_tn(a, b):
    return lax.dot_general(a, b, (((0,), (0,)), ((), ())), preferred_element_type=F32)


def _dot(a, b):
    return jnp.dot(a, b, preferred_element_type=F32)


def _sigmoid(x):
    return 1.0 / (1.0 + jnp.exp(-x))


def _col_reduce(op, x):
    rows, n = x.shape
    if rows > CNT_ROWS and rows % CNT_ROWS == 0:
        x = op(x.reshape(rows // CNT_ROWS, CNT_ROWS, n), axis=0)
    return op(x, axis=0, keepdims=True)


def _ada_kernel(c_ref, w_ref, b_ref, o_ref):
    o_ref[...] = _dot(c_ref[...].astype(BF16), w_ref[...].astype(BF16)) + b_ref[...]


def _ada(c_all, w_ada, b_ada, tn=1024):
    r, d = c_all.shape
    n = w_ada.shape[1]
    return pl.pallas_call(
        _ada_kernel,
        out_shape=jax.ShapeDtypeStruct((r, n), F32),
        grid=(n // tn,),
        in_specs=[pl.BlockSpec((r, d), lambda j: (0, 0)),
                  pl.BlockSpec((d, tn), lambda j: (0, j)),
                  pl.BlockSpec((1, tn), lambda j: (0, j))],
        out_specs=pl.BlockSpec((r, tn), lambda j: (0, j)),
        compiler_params=_cparams(("arbitrary",)),
        name="ada",
    )(c_all, w_ada, b_ada)


def _normmod_kernel(x_ref, sc_ref, sh_ref, g_ref, o_ref):
    x = x_ref[...]
    ms = jnp.mean(x * x, axis=-1, keepdims=True)
    xn = x * lax.rsqrt(ms + EPS) * g_ref[...]
    o_ref[...] = (xn * (1.0 + sc_ref[...]) + sh_ref[...]).astype(o_ref.dtype)


def _normmod(x, sc, sh, g, bb, tb):
    b, t, d = x.shape
    return pl.pallas_call(
        _normmod_kernel,
        out_shape=jax.ShapeDtypeStruct((b, t, d), BF16),
        grid=(b // bb, t // tb),
        in_specs=[pl.BlockSpec((bb, tb, d), lambda i, j: (i, j, 0)),
                  pl.BlockSpec((bb, 1, d), lambda i, j: (i, 0, 0)),
                  pl.BlockSpec((bb, 1, d), lambda i, j: (i, 0, 0)),
                  pl.BlockSpec((1, 1, d), lambda i, j: (0, 0, 0))],
        out_specs=pl.BlockSpec((bb, tb, d), lambda i, j: (i, j, 0)),
        compiler_params=_cparams(("arbitrary", "arbitrary")),
        name="normmod",
    )(x, sc, sh, g)


def _proj_plain_kernel(h_ref, w_ref, *o_refs, scale):
    z = _dot_nt(h_ref[...], w_ref[...].astype(BF16))
    if scale != 1.0:
        z = z * scale
    for o_ref in o_refs:
        o_ref[...] = z.astype(o_ref.dtype)


def _store_head_rows(o_ref, r):
    heads = r.shape[1] // LANES
    for g in range(heads):
        o_ref[pl.ds(g, r.shape[0], stride=heads), :] = r[:, g * LANES:(g + 1) * LANES].astype(o_ref.dtype)


def _proj_rope_kernel(h_ref, w_ref, cos_ref, sup_ref, sdn_ref, *o_refs, scale):
    z = _dot_nt(h_ref[...], w_ref[...].astype(BF16))
    tn = z.shape[1]
    reps = tn // LANES

    def wide(ref):
        t = ref[...]
        return t if reps == 1 else jnp.concatenate([t] * reps, axis=1)

    up = pltpu.roll(z, tn - ROT_DIM // 2, 1)
    dn = pltpu.roll(z, ROT_DIM // 2, 1)
    r = z * wide(cos_ref) + up * wide(sup_ref) + dn * wide(sdn_ref)
    for o_ref in o_refs:
        if o_ref.shape[0] != r.shape[0]:
            _store_head_rows(o_ref, r)
        elif o_ref.dtype == BF16 and scale != 1.0:
            o_ref[...] = (r * scale).astype(BF16)
        else:
            o_ref[...] = r.astype(o_ref.dtype)


def _proj_forget_kernel(h_ref, w_ref, lbl_ref, lf_ref, kk_ref):
    z = _dot_nt(h_ref[...], w_ref[...].astype(BF16))
    lbl = lbl_ref[...]
    mx = jnp.max(lbl, axis=0, keepdims=True)
    e = jnp.exp(lbl - mx)
    lb = e[0:1, :] / jnp.sum(e, axis=0, keepdims=True)
    f = lb + (1.0 - lb) * _sigmoid(z)
    lf_ref[...] = jnp.log(f) * LOG2E
    kk_ref[...] = 1.0 - f


def _proj(kind, h, wspec, tm, tn, extra=(), out_dtypes=(F32,), scale=1.0, head_rows=()):
    w, c0, n = wspec
    m, k = h.shape
    assert c0 % SUBLANES == 0 and n % tn == 0
    in_specs = [pl.BlockSpec((tm, k), lambda i, j: (i, 0)),
                pl.BlockSpec((pl.Element(tn), pl.Element(k)),
                             lambda i, j: (pl.multiple_of(c0 + j * tn, SUBLANES), 0))]
    if kind == "rope":
        kern = functools.partial(_proj_rope_kernel, scale=scale)
        in_specs += [pl.BlockSpec((tm, LANES), lambda i, j: (i, 0))] * 3
    elif kind == "forget":
        kern = _proj_forget_kernel
        in_specs += [pl.BlockSpec((extra[0].shape[0], tn), lambda i, j: (0, j))]
    else:
        kern = functools.partial(_proj_plain_kernel, scale=scale)
    hpt = tn // LANES
    outs, out_specs = [], []
    for o, dt in enumerate(out_dtypes):
        if o in head_rows:
            assert n == tn
            outs.append(jax.ShapeDtypeStruct((m * hpt, LANES), dt))
            out_specs.append(pl.BlockSpec((tm * hpt, LANES), lambda i, j: (i, 0)))
        else:
            outs.append(jax.ShapeDtypeStruct((m, n), dt))
            out_specs.append(pl.BlockSpec((tm, tn), lambda i, j: (i, j)))
    res = pl.pallas_call(
        kern,
        out_shape=tuple(outs),
        grid=(m // tm, n // tn),
        in_specs=in_specs,
        out_specs=tuple(out_specs),
        compiler_params=_cparams(("arbitrary", "arbitrary")),
        name="proj_" + kind,
    )(h, w, *extra)
    return res


def _proj_v_kernel(h_ref, w_ref, vf_ref, vt_ref, *, wn):
    z = _dot_nt(h_ref[...], w_ref[...].astype(BF16))
    _store_head_rows(vf_ref, z)
    zt = z.T
    ones = jnp.ones((BF16_ROWS, wn), BF16)
    for u in range(z.shape[0] // wn):
        blk = zt[:, u * wn:(u + 1) * wn].astype(BF16)
        for g in range(N_KV_HEADS):
            vt_ref[u, g * VT_ROWS:g * VT_ROWS + HEAD_DIM, :] = blk[g * HEAD_DIM:(g + 1) * HEAD_DIM]
            vt_ref[u, g * VT_ROWS + HEAD_DIM:(g + 1) * VT_ROWS, :] = ones


def _proj_v(h, wspec, tm, wn):
    w, c0, n = wspec
    m, k = h.shape
    assert c0 % SUBLANES == 0
    return pl.pallas_call(
        functools.partial(_proj_v_kernel, wn=wn),
        out_shape=(jax.ShapeDtypeStruct((m * N_KV_HEADS, HEAD_DIM), F32),
                   jax.ShapeDtypeStruct((m // wn, N_KV_HEADS * VT_ROWS, wn), BF16)),
        grid=(m // tm,),
        in_specs=[pl.BlockSpec((tm, k), lambda i: (i, 0)),
                  pl.BlockSpec((pl.Element(n), pl.Element(k)), lambda i: (c0, 0))],
        out_specs=(pl.BlockSpec((tm * N_KV_HEADS, HEAD_DIM), lambda i: (i, 0)),
                   pl.BlockSpec((tm // wn, N_KV_HEADS * VT_ROWS, wn), lambda i: (i, 0, 0))),
        compiler_params=_cparams(("arbitrary",)),
        name="proj_v",
    )(h, w)


def _dsa_kernel(*refs, tq, tk, t_new, l_cache, pos0, topk, has_cache):
    if has_cache:
        (qi_ref, wi_ref, q_ref, kin_ref, kn_ref, vtn_ref, cki_ref, ck_ref, cv_ref, o_ref,
         sc_ref, qit_ref, wt_ref, qt_ref, acc_ref, s_ref, p_ref, tau_ref) = refs
    else:
        (qi_ref, wi_ref, q_ref, kin_ref, kn_ref, vtn_ref, o_ref,
         sc_ref, qit_ref, wt_ref, qt_ref, acc_ref, s_ref, p_ref, tau_ref) = refs
        cki_ref = ck_ref = cv_ref = None
    i = pl.program_id(1)
    ntc = l_cache // tk
    wn = min(tk, t_new)
    nq = GROUP * tq
    q0 = pos0 + i * tq
    lane_q = lax.broadcasted_iota(jnp.int32, (1, tq), 1)
    qend = (((q0 + lane_q) >> 6) + 1) << 6
    last_end = (((q0 + tq - 1) >> 6) + 1) << 6
    nvis_new = jnp.minimum(last_end - pos0, t_new)
    ntn = (nvis_new + wn - 1) // wn
    ntiles = ntc + ntn

    def tpose(x):
        xf = x.astype(F32)
        if tq < LANES:
            xf = jnp.concatenate([xf, jnp.zeros((LANES - tq, LANES), F32)], axis=0)
        xt = xf.T
        return xt[:, :tq] if tq < LANES else xt

    qi_blk = qi_ref[0]
    for h in range(IDX_HEADS):
        qit_ref[:, h * tq:(h + 1) * tq] = tpose(qi_blk[:, h * IDX_DIM:(h + 1) * IDX_DIM]).astype(BF16)
    wt_ref[...] = tpose(wi_ref[0])
    q_blk = q_ref[0]
    for g in range(N_KV_HEADS):
        for a in range(GROUP):
            hh = GROUP * g + a
            qt_ref[g, :, a * tq:(a + 1) * tq] = tpose(q_blk[:, hh * HEAD_DIM:(hh + 1) * HEAD_DIM]).astype(BF16)

    def score_tile(ki_tile, kpos0, w):
        acc = jnp.zeros((w, tq), F32)
        hpd = 2 if 2 * tq % LANES == 0 else 1
        for hp in range(IDX_HEADS // hpd):
            lg = _dot(ki_tile, qit_ref[:, hpd * hp * tq:(hpd * hp + hpd) * tq])
            for e in range(hpd):
                h = hpd * hp + e
                acc = acc + wt_ref[h:h + 1, :] * jnp.maximum(lg[:, e * tq:(e + 1) * tq], 0.0)
        kpos = kpos0 + lax.broadcasted_iota(jnp.int32, (w, 1), 0)
        vis = kpos < qend
        s = jnp.where(vis, acc, NEG_INF)
        smin = _col_reduce(jnp.min, jnp.where(vis, acc, POS_INF))
        smax = _col_reduce(jnp.max, s)
        return s, smax, smin

    def p1_cache(t, carry):
        mx, mn = carry
        r0 = pl.multiple_of(t * tk, tk)
        s, smax, smin = score_tile(cki_ref[0, pl.ds(r0, tk), :].astype(BF16), r0, tk)
        sc_ref[t] = s
        return jnp.maximum(mx, smax), jnp.minimum(mn, smin)

    def p1_new(j, carry):
        mx, mn = carry
        r0 = pl.multiple_of(j * wn, wn)
        s, smax, smin = score_tile(kin_ref[0, pl.ds(r0, wn), :], pos0 + r0, wn)
        if wn == tk:
            sc_ref[ntc + j] = s
        else:
            sc_ref[ntc + j] = jnp.full((tk, tq), NEG_INF, F32)
            sc_ref[ntc + j, 0:wn, :] = s
        return jnp.maximum(mx, smax), jnp.minimum(mn, smin)

    carry = (jnp.full((1, tq), NEG_INF, F32), jnp.full((1, tq), POS_INF, F32))
    if has_cache:
        carry = lax.fori_loop(0, ntc, p1_cache, carry)
    mx, mn = lax.fori_loop(0, ntn, p1_new, carry)

    nvis = jnp.clip(qend - pos0, 0, t_new)
    if has_cache:
        nvis = nvis + jnp.minimum(qend, l_cache)
    act0 = jnp.where(nvis > topk, 1.0, 0.0)
    kf = float(topk)
    cu = min(tk, COUNT_UNIT)
    upt = tk // cu
    n_units = ntc * upt + (nvis_new + cu - 1) // cu

    def bis_cond(carry):
        return jnp.logical_and(carry[0] < 400, carry[1] > 0)

    def unit(u):
        return sc_ref.at[u // upt, pl.ds(pl.multiple_of((u % upt) * cu, cu), cu), :]

    def count_where(pred):
        def body(u, cnt):
            ind = jnp.where(pred(unit(u)[...]), 1.0, 0.0)
            return cnt + jnp.sum(ind.reshape(cu // CNT_ROWS, CNT_ROWS, tq), axis=0)

        cnt = lax.fori_loop(0, n_units, body, jnp.zeros((CNT_ROWS, tq), F32))
        return jnp.sum(cnt, axis=0, keepdims=True)

    def bis_body(carry):
        it, _, lo, hi, tau, act, tied = carry
        mid = lo * 0.5 + hi * 0.5
        c = count_where(lambda s: s >= mid)
        found = jnp.where(c == kf, act, 0.0)
        stuck = jnp.where(jnp.logical_or(mid <= lo, mid >= hi), act, 0.0) * (1.0 - found)
        above = jnp.where(c > kf, act, 0.0) * (1.0 - stuck)
        below = act * (1.0 - above) * (1.0 - found) * (1.0 - stuck)
        tau = jnp.where(found > 0, mid, tau)
        tau = jnp.where(stuck > 0, lo, tau)
        lo = jnp.where(above > 0, mid, lo)
        hi = jnp.where(below > 0, mid, hi)
        nact = act * (1.0 - found) * (1.0 - stuck)
        return it + 1, (jnp.max(nact) > 0).astype(jnp.int32), lo, hi, tau, nact, jnp.maximum(tied, stuck)

    init = (jnp.int32(0), (jnp.max(act0) > 0).astype(jnp.int32), mn, mx, mn, act0, jnp.zeros((1, tq), F32))
    res = lax.while_loop(bis_cond, lambda carry: bis_body(bis_body(carry)), init)
    tau_ref[...] = res[4]
    tied = res[6]

    @pl.when(jnp.max(tied) > 0)
    def _():
        lo_t = tau_ref[...]

        def tmin_body(u, acc):
            s = unit(u)[...]
            return jnp.minimum(acc, _col_reduce(jnp.min, jnp.where(s >= lo_t, s, POS_INF)))

        tstar = lax.fori_loop(0, n_units, tmin_body, jnp.full((1, tq), POS_INF, F32))
        tstar = jnp.where(tied > 0, tstar, lo_t)
        need = kf - count_where(lambda s: s > tstar)
        ri = lax.broadcasted_iota(jnp.int32, (cu, cu), 0)
        ci = lax.broadcasted_iota(jnp.int32, (cu, cu), 1)
        tri = jnp.where(ri >= ci, 1.0, 0.0).astype(BF16)

        def strike_body(u, run):
            s = unit(u)[...]
            eq = jnp.where(jnp.logical_and(s == tstar, tied > 0), 1.0, 0.0)
            rank = run + _dot(tri, eq.astype(BF16))
            unit(u)[...] = jnp.where(jnp.logical_and(eq > 0, rank > need), NEG_INF, s)
            return rank[cu - 1:cu, :]

        lax.fori_loop(0, n_units, strike_body, jnp.zeros((1, tq), F32))
        tau_ref[...] = tstar

    tau = tau_ref[...]

    acc_ref[...] = jnp.zeros(acc_ref.shape, F32)

    def attend(t, k_of_g, vt_of_g, w, ms):
        s_idx = sc_ref[t]
        if w < tk:
            s_idx = s_idx[0:w]
        bias = jnp.where(s_idx >= tau, 0.0, NEG_INF)
        bias4 = jnp.concatenate([bias] * GROUP, axis=1)
        def qk(g):
            s = _dot(k_of_g(g), qt_ref[g]) + bias4
            s_ref[g % 2, 0:w] = s
            return _col_reduce(jnp.max, s)

        out = []
        tile_max = qk(0)
        for g in range(N_KV_HEADS):
            slot = g % 2
            next_max = qk(g + 1) if g + 1 < N_KV_HEADS else None
            m_old = ms[g]
            m_new = jnp.maximum(m_old, tile_max)
            m_safe = jnp.where(m_new == NEG_INF, 0.0, m_new)
            alpha = jnp.exp2(m_old - m_safe)
            p_ref[slot, 0:w] = jnp.exp2(s_ref[slot, 0:w] - m_safe).astype(BF16)
            acc_ref[g] = acc_ref[g] * alpha + _dot(vt_of_g(g), p_ref[slot, 0:w])
            out.append(m_new)
            tile_max = next_max
        return tuple(out)

    def p3_cache(t, ms):
        r0 = pl.multiple_of(t * tk, tk)

        def vt(g):
            vg = cv_ref[0, pl.ds(r0 * N_KV_HEADS + g, tk, stride=N_KV_HEADS), :]
            return jnp.concatenate([vg.T, jnp.ones((BF16_ROWS, tk), F32)], axis=0).astype(BF16)

        return attend(
            t, lambda g: ck_ref[0, pl.ds(r0 * N_KV_HEADS + g, tk, stride=N_KV_HEADS), :].astype(BF16), vt, tk, ms)

    def p3_new(j, ms):
        r0 = pl.multiple_of(j * wn, wn)
        return attend(ntc + j,
                      lambda g: kn_ref[0, pl.ds(r0, wn), g * HEAD_DIM:(g + 1) * HEAD_DIM],
                      lambda g: vtn_ref[0, j, g * VT_ROWS:(g + 1) * VT_ROWS, :], wn, ms)

    ms = tuple(jnp.full((1, nq), NEG_INF, F32) for _ in range(N_KV_HEADS))
    if has_cache:
        ms = lax.fori_loop(0, ntc, p3_cache, ms)
    lax.fori_loop(0, ntn, p3_new, ms)

    for g in range(N_KV_HEADS):
        acc = acc_ref[g]
        og = (acc[0:HEAD_DIM] / acc[HEAD_DIM:HEAD_DIM + 1]).T
        for a in range(GROUP):
            hh = GROUP * g + a
            o_ref[0, :, hh * HEAD_DIM:(hh + 1) * HEAD_DIM] = og[a * tq:(a + 1) * tq, :].astype(o_ref.dtype)


def _dsa(qi, wi, q, kin, kn, vtn, cache, pos0, tq, tk):
    b, t, _ = q.shape
    has_cache = cache is not None
    l_cache = cache[0].shape[1] if has_cache else 0
    topk = min(TOPK_MAX, (l_cache + t) // 4)
    wn = min(tk, t)
    ntn = t // wn
    nt = l_cache // tk + ntn
    kvd = N_KV_HEADS * HEAD_DIM
    nq = GROUP * tq
    res = pl.Buffered(1) if b == 1 else None
    in_specs = [pl.BlockSpec((1, tq, IDX_HEADS * IDX_DIM), lambda bb, i: (bb, i, 0)),
                pl.BlockSpec((1, tq, LANES), lambda bb, i: (bb, i, 0)),
                pl.BlockSpec((1, tq, N_HEADS * HEAD_DIM), lambda bb, i: (bb, i, 0)),
                pl.BlockSpec((1, t, IDX_DIM), lambda bb, i: (bb, 0, 0), pipeline_mode=res),
                pl.BlockSpec((1, t, kvd), lambda bb, i: (bb, 0, 0), pipeline_mode=res),
                pl.BlockSpec((1, ntn, N_KV_HEADS * VT_ROWS, wn), lambda bb, i: (bb, 0, 0, 0), pipeline_mode=res)]
    args = [qi, wi, q, kin, kn, vtn]
    if has_cache:
        in_specs += [pl.BlockSpec((1, l_cache, IDX_DIM), lambda bb, i: (bb, 0, 0)),
                     pl.BlockSpec((1, l_cache * N_KV_HEADS, HEAD_DIM), lambda bb, i: (bb, 0, 0)),
                     pl.BlockSpec((1, l_cache * N_KV_HEADS, HEAD_DIM), lambda bb, i: (bb, 0, 0))]
        args += list(cache)
    kern = functools.partial(_dsa_kernel, tq=tq, tk=tk, t_new=t, l_cache=l_cache, pos0=pos0, topk=topk,
                             has_cache=has_cache)
    return pl.pallas_call(
        kern,
        out_shape=jax.ShapeDtypeStruct((b, t, N_HEADS * HEAD_DIM), BF16),
        grid=(b, t // tq),
        in_specs=in_specs,
        out_specs=pl.BlockSpec((1, tq, N_HEADS * HEAD_DIM), lambda bb, i: (bb, i, 0)),
        scratch_shapes=[pltpu.VMEM((nt, tk, tq), F32),
                        pltpu.VMEM((IDX_DIM, IDX_HEADS * tq), BF16),
                        pltpu.VMEM((LANES, tq), F32),
                        pltpu.VMEM((N_KV_HEADS, HEAD_DIM, nq), BF16),
                        pltpu.VMEM((N_KV_HEADS, VT_ROWS, nq), F32),
                        pltpu.VMEM((2, tk, nq), F32),
                        pltpu.VMEM((2, tk, nq), BF16),
                        pltpu.VMEM((1, tq), F32)],
        compiler_params=_cparams(("arbitrary", "arbitrary")),
        name="dsa",
    )(*args)


def _hgrn_kernel(*refs, c, nchunk, nh, has_state):
    if has_state:
        q_ref, lf_ref, kk_ref, v_ref, hg_ref, nw_ref, s0_ref, o_ref, sout_ref, st_ref, g_ref = refs
    else:
        q_ref, lf_ref, kk_ref, v_ref, hg_ref, nw_ref, o_ref, sout_ref, st_ref, g_ref = refs
        s0_ref = None
    r = pl.program_id(2)

    @pl.when(r == 0)
    def _():
        for hh in range(nh):
            if has_state:
                st_ref[hh] = s0_ref[0, hh].T
            else:
                st_ref[hh] = jnp.zeros((HG_DV, HG_DK), F32)

    ri = lax.broadcasted_iota(jnp.int32, (c, c), 0)
    ci = lax.broadcasted_iota(jnp.int32, (c, c), 1)
    tri = jnp.where(ri >= ci, 1.0, 0.0).astype(BF16)
    nsub = c // SUB_BLOCK
    pair = lax.broadcasted_iota(jnp.int32, (SUB_BLOCK * SUB_BLOCK, HG_DK), 0)
    cap = jnp.where((pair % SUB_BLOCK) >= (pair // SUB_BLOCK), 0.0, NEG_INF)
    ones_w = jnp.ones((HG_DK, LANES), BF16)

    def rep_rows(row):
        return jnp.concatenate(
            [jnp.broadcast_to(row(sg), (SUB_BLOCK, HG_DK)) for sg in range(SUB_BLOCK)], axis=0)

    def cumulative(base):
        lf = lf_ref[pl.ds(base, c), :]
        l1 = lf.astype(BF16)
        r1 = lf - l1.astype(F32)
        l2 = r1.astype(BF16)
        l3 = (r1 - l2.astype(F32)).astype(BF16)
        return _dot(tri, l1) + _dot(tri, l2) + _dot(tri, l3)

    def stage_a(base, hh, g_all):
        cs = slice(hh * HG_DK, (hh + 1) * HG_DK)
        G = g_all[:, cs]
        kk = kk_ref[pl.ds(base, c), cs]
        v = v_ref[pl.ds(base, c), cs].astype(F32)
        g_ref[0, hh] = G
        g_ref[1, hh] = kk
        g_ref[2, hh] = v
        return dict(cs=cs, G=G, hh=hh, q=q_ref[pl.ds(base, c), cs].astype(F32), kk=kk, v=v)

    def stage_b(hh, d):
        q, kk, v, G = d["q"], d["kk"], d["v"], d["G"]
        st = st_ref[hh]
        vb = v.astype(BF16)
        d["vb"] = vb
        d["o_inter"] = _dot_nt((q * jnp.exp2(G)).astype(BF16), st.astype(BF16))
        d["A"], diag = [], []
        for i in range(nsub):
            lo_, hi_ = i * SUB_BLOCK, (i + 1) * SUB_BLOCK
            qi_ = q[lo_:hi_]
            Gi = G[lo_:hi_]
            if i > 0:
                Gb = G[lo_ - 1:lo_]
                qt = qi_ * jnp.exp2(Gi - Gb)
                kt = kk[:lo_] * jnp.exp2(Gb - G[:lo_])
                a_blk = _dot_nt(qt.astype(BF16), kt.astype(BF16))
                d["A"].append(jnp.concatenate([a_blk, jnp.zeros((SUB_BLOCK, c - lo_), F32)], axis=1))
            qrep = jnp.concatenate([qi_] * SUB_BLOCK, axis=0)
            grep = jnp.concatenate([Gi] * SUB_BLOCK, axis=0)
            kk_rep = rep_rows(lambda sg: g_ref[1, hh, lo_ + sg:lo_ + sg + 1, :])
            g_rep = rep_rows(lambda sg: g_ref[0, hh, lo_ + sg:lo_ + sg + 1, :])
            diag.append((qrep * kk_rep * jnp.exp2(jnp.minimum(grep - g_rep, cap))).astype(BF16))
        rs = _dot(jnp.concatenate(diag, axis=0) if nsub > 1 else diag[0], ones_w)
        pairs = SUB_BLOCK * SUB_BLOCK
        d["rs"] = [rs[i * pairs:(i + 1) * pairs] for i in range(nsub)]
        Gl = G[c - 1:c]
        kdec = kk * jnp.exp2(Gl - G)
        st_ref[hh] = st * jnp.exp2(Gl) + _dot_tn(vb, kdec.astype(BF16))

    def stage_c(base, d):
        parts = []
        o_base = d["o_inter"]
        if nsub > 1:
            a_low = jnp.concatenate([jnp.zeros((SUB_BLOCK, c), F32)] + d["A"], axis=0)
            o_base = o_base + _dot(a_low.astype(BF16), d["vb"])
        for i in range(nsub):
            lo_, hi_ = i * SUB_BLOCK, (i + 1) * SUB_BLOCK
            oi = o_base[lo_:hi_]
            v_rep = rep_rows(lambda sg: g_ref[2, d["hh"], lo_ + sg:lo_ + sg + 1, :])
            contrib = (d["rs"][i] * v_rep).reshape(SUB_BLOCK, SUB_BLOCK, HG_DV)
            parts.append(oi + jnp.sum(contrib, axis=0))
        o = jnp.concatenate(parts, axis=0) if nsub > 1 else parts[0]
        on = o * lax.rsqrt(jnp.mean(o * o, axis=1, keepdims=True) + EPS) * nw_ref[...]
        hg = hg_ref[pl.ds(base, c), d["cs"]].astype(F32)
        o_ref[pl.ds(base, c), d["cs"]] = (on * (hg * _sigmoid(hg))).astype(o_ref.dtype)

    def chunk(n, carry):
        base = pl.multiple_of(n * c, c)
        g_all = cumulative(base)
        heads = [stage_a(base, hh, g_all) for hh in range(nh)]
        for hh, d in enumerate(heads):
            stage_b(hh, d)
        for d in heads:
            stage_c(base, d)
        return carry

    lax.fori_loop(0, nchunk, chunk, 0)

    @pl.when(r == pl.num_programs(2) - 1)
    def _():
        for hh in range(nh):
            sout_ref[0, hh] = st_ref[hh].T


def _hgrn(hq, lf, kk, hv, hg, nw, s0, b, t, rb, nh):
    c = min(CHUNK, t)
    nchunk = rb // c
    nr = t // rb
    has_state = s0 is not None
    blk = pl.BlockSpec((rb, nh * HG_DK), lambda bb, h, r: (bb * nr + r, h))
    sblk = pl.BlockSpec((1, nh, HG_DK, HG_DV), lambda bb, h, r: (bb, h, 0, 0))
    in_specs = [blk, blk, blk, blk, blk, pl.BlockSpec((1, HG_DV), lambda bb, h, r: (0, 0))]
    args = [hq, lf, kk, hv, hg, nw]
    if has_state:
        in_specs.append(sblk)
        args.append(s0)
    kern = functools.partial(_hgrn_kernel, c=c, nchunk=nchunk, nh=nh, has_state=has_state)
    return pl.pallas_call(
        kern,
        out_shape=(jax.ShapeDtypeStruct((b * t, HG_HEADS * HG_DV), BF16),
                   jax.ShapeDtypeStruct((b, HG_HEADS, HG_DK, HG_DV), F32)),
        grid=(b, HG_HEADS // nh, nr),
        in_specs=in_specs,
        out_specs=(blk, sblk),
        scratch_shapes=[pltpu.VMEM((nh, HG_DV, HG_DK), F32), pltpu.VMEM((3, nh, c, HG_DK), F32)],
        compiler_params=_cparams(("arbitrary", "arbitrary", "arbitrary")),
        name="hgrn",
    )(*args)


def _merge_out_kernel(x_ref, oa_ref, oh_ref, ga_ref, gb_ref, w_ref, g1_ref, sc_ref, sh_ref, nw_ref,
                      x1_ref, h2_ref):
    bb, tb, d = x_ref.shape
    merged = (_sigmoid(ga_ref[...].astype(F32)) * oa_ref[...].astype(F32)
              + _sigmoid(gb_ref[...].astype(F32)) * oh_ref[...].astype(F32))
    y = _dot(merged.astype(BF16), w_ref[...]).reshape(bb, tb, d)
    x1 = x_ref[...] + g1_ref[...] * y
    x1_ref[...] = x1
    ms = jnp.mean(x1 * x1, axis=-1, keepdims=True)
    xn = x1 * lax.rsqrt(ms + EPS) * nw_ref[...]
    h2 = xn * (1.0 + sc_ref[...]) + sh_ref[...]
    h2_ref[...] = h2.reshape(bb * tb, d).astype(BF16)


def _merge_out(x, oa, oh, ga, gb, w_out, g1, sc2, sh2, nw2, bb, tb):
    b, t, d = x.shape
    tm = bb * tb
    nt = t // tb

    def row(i, j):
        return (i * nt + j, 0)

    def mod(i, j):
        return (i, 0, 0)

    x3 = pl.BlockSpec((bb, tb, d), lambda i, j: (i, j, 0))
    r2 = pl.BlockSpec((tm, d), row)
    return pl.pallas_call(
        _merge_out_kernel,
        out_shape=(jax.ShapeDtypeStruct((b, t, d), F32), jax.ShapeDtypeStruct((b * t, d), BF16)),
        grid=(b // bb, nt),
        in_specs=[x3, r2, r2, r2, r2,
                  pl.BlockSpec((d, d), lambda i, j: (0, 0)),
                  pl.BlockSpec((bb, 1, d), mod), pl.BlockSpec((bb, 1, d), mod), pl.BlockSpec((bb, 1, d), mod),
                  pl.BlockSpec((1, 1, d), lambda i, j: (0, 0, 0))],
        out_specs=(x3, r2),
        compiler_params=_cparams(("arbitrary", "arbitrary")),
        name="merge_out",
    )(x, oa, oh, ga, gb, w_out, g1, sc2, sh2, nw2)


def _mlp_kernel(h_ref, wu_ref, wd_ref, x1_ref, g2_ref, fw_ref, y_ref, acc_ref):
    f = pl.program_id(2)

    @pl.when(f == 0)
    def _():
        acc_ref[...] = jnp.zeros(acc_ref.shape, F32)

    u = jnp.maximum(_dot(h_ref[...], wu_ref[...]), 0.0)
    acc_ref[...] += _dot((u * u).astype(BF16), wd_ref[...])

    @pl.when(f == pl.num_programs(2) - 1)
    def _():
        bb, tb, d = x1_ref.shape
        x2 = x1_ref[...] + g2_ref[...] * acc_ref[...].reshape(bb, tb, d)
        ms = jnp.mean(x2 * x2, axis=-1, keepdims=True)
        y_ref[...] = x2 * lax.rsqrt(ms + EPS) * fw_ref[...]


def _mlp(h2, w_up, w_down, x1, g2, fw, bb, tb, tf):
    b, t, d = x1.shape
    dff = w_up.shape[1]
    tm = bb * tb
    nt = t // tb
    x3 = pl.BlockSpec((bb, tb, d), lambda i, j, f: (i, j, 0))
    return pl.pallas_call(
        _mlp_kernel,
        out_shape=jax.ShapeDtypeStruct((b, t, d), F32),
        grid=(b // bb, nt, dff // tf),
        in_specs=[pl.BlockSpec((tm, d), lambda i, j, f: (i * nt + j, 0)),
                  pl.BlockSpec((d, tf), lambda i, j, f: (0, f)),
                  pl.BlockSpec((tf, d), lambda i, j, f: (f, 0)),
                  x3,
                  pl.BlockSpec((bb, 1, d), lambda i, j, f: (i, 0, 0)),
                  pl.BlockSpec((1, 1, d), lambda i, j, f: (0, 0, 0))],
        out_specs=x3,
        scratch_shapes=[pltpu.VMEM((tm, d), F32)],
        compiler_params=_cparams(("arbitrary", "arbitrary", "arbitrary")),
        name="mlp",
    )(h2, w_up, w_down, x1, g2, fw)


def _rope_tables(pos):
    half = ROT_DIM // 2
    inv_freq = ROPE_THETA ** (-(jnp.arange(half, dtype=F32) * (2.0 / ROT_DIM)))
    ang = pos.astype(F32)[:, None] * inv_freq[None, :]
    cos, sin = jnp.cos(ang), jnp.sin(ang)
    n = pos.shape[0]
    ones = jnp.ones((n, LANES - ROT_DIM), F32)
    zeros = jnp.zeros((n, LANES - ROT_DIM), F32)
    zh = jnp.zeros((n, half), F32)
    c_t = jnp.concatenate([cos, cos, ones], axis=1)
    s_up = jnp.concatenate([-sin, zh, zeros], axis=1)
    s_dn = jnp.concatenate([zh, sin, zeros], axis=1)
    return c_t, s_up, s_dn


def _trunk(x, mod, pos0, past, wts, blocks):
    (norm1_w, w_parts, lb_logits, hg_norm_w, w_out, norm2_w, w_up, w_down, final_w) = wts
    b, t, d = x.shape
    bb, tb, tm_proj, tq, tk, rb, mlp_bb, mlp_tb = blocks
    m = [mod[:, i:i + 1, :] for i in range(6)]
    sh1, sc1, g1, sh2, sc2, g2 = m
    h = _normmod(x, sc1, sh1, norm1_w.reshape(1, 1, d), bb, tb).reshape(b * t, d)

    pos = pos0 + jnp.arange(t, dtype=jnp.int32)
    tabs = tuple(jnp.tile(tb_, (b, 1)) for tb_ in _rope_tables(pos))
    wq, wk, wv, wqi, wki, wwi, whq, whf, whi, whg, wga, wgb = w_parts
    tm = min(tm_proj, b * t)
    wn = min(tk, t)
    (q_bf,) = _proj("rope", h, wq, tm, PROJ_TN, tabs, (BF16,), scale=HEAD_DIM ** -0.5 * LOG2E)
    k_f, k_bf = _proj("rope", h, wk, tm, PROJ_TN, tabs, (F32, BF16), head_rows=(0,))
    v_f, vt_bf = _proj_v(h, wv, tm, wn)
    (qi_bf,) = _proj("rope", h, wqi, tm, PROJ_TN, tabs, (BF16,))
    ki_f, ki_bf = _proj("rope", h, wki, tm, LANES, tabs, (F32, BF16))
    (wi_f,) = _proj("plain", h, wwi, tm, LANES, (), (F32,), scale=IDX_HEADS ** -0.5 * IDX_DIM ** -0.5)
    (hq,) = _proj("plain", h, whq, tm, PROJ_TN, (), (BF16,))
    lf, kk = _proj("forget", h, whf, tm, PROJ_TN, (lb_logits,), (F32, F32))
    (hi,) = _proj("plain", h, whi, tm, PROJ_TN, (), (BF16,))
    (hg,) = _proj("plain", h, whg, tm, PROJ_TN, (), (BF16,))
    (ga,) = _proj("plain", h, wga, tm, PROJ_TN, (), (BF16,))
    (gb,) = _proj("plain", h, wgb, tm, PROJ_TN, (), (BF16,))

    kvd = N_KV_HEADS * HEAD_DIM
    r3 = lambda a: a.reshape(b, t, a.shape[-1])
    if past is None:
        cache, s0 = None, None
    else:
        ck, cv, cki, s0 = past
        lc = ck.shape[1]
        cache = (cki, ck.reshape(b, lc * N_KV_HEADS, HEAD_DIM), cv.reshape(b, lc * N_KV_HEADS, HEAD_DIM))
    vtn = vt_bf.reshape(b, t // wn, N_KV_HEADS * VT_ROWS, wn)
    o_attn = _dsa(r3(qi_bf), r3(wi_f), r3(q_bf), r3(ki_bf), r3(k_bf), vtn, cache, pos0, tq, tk)
    o_hg, s_new = _hgrn(hq, lf, kk, hi, hg, hg_norm_w.reshape(1, HG_DV), s0, b, t, rb, 8)

    x1, h2 = _merge_out(x, o_attn.reshape(b * t, d), o_hg, ga, gb, w_out, g1, sc2, sh2,
                        norm2_w.reshape(1, 1, d), bb, tb)
    y = _mlp(h2, w_up, w_down, x1, g2, final_w.reshape(1, 1, d), mlp_bb, mlp_tb, 512)
    return (y, k_f.reshape(b, t, N_KV_HEADS, HEAD_DIM), v_f.reshape(b, t, N_KV_HEADS, HEAD_DIM),
            ki_f.reshape(b, t, IDX_DIM), s_new)


def kernel(x_prompt, x_sample, cache_k, cache_v, cache_ki, state_hgrn, c_prompt, c_sample, w_ada, b_ada, norm1_w,
           w_in, hg_lb_logits, hg_norm_w, w_out, norm2_w, w_up, w_down, final_norm_w):
    depth = w_in.shape[0]
    assert depth == 1
    d = x_prompt.shape[-1]
    bp, tp, _ = x_prompt.shape
    bs, ts, _ = x_sample.shape
    past_len = cache_k.shape[2]

    c_all = jnp.concatenate([c_prompt, c_sample], axis=0)
    nrow = c_all.shape[0]
    pad = (-nrow) % SUBLANES
    c_all = jnp.pad(c_all, ((0, pad), (0, 0)))
    mod = _ada(c_all, w_ada[0], b_ada[0].reshape(1, -1)).reshape(nrow + pad, 6, d)

    sizes = (N_HEADS * HEAD_DIM, N_KV_HEADS * HEAD_DIM, N_KV_HEADS * HEAD_DIM, IDX_HEADS * IDX_DIM, IDX_DIM,
             IDX_HEADS, HG_HEADS * HG_DK, HG_HEADS * HG_DK, HG_HEADS * HG_DV, HG_HEADS * HG_DV, d, d)
    offs = np.concatenate([[0], np.cumsum(sizes)])
    w_t = jnp.transpose(w_in[0])
    w_parts = [(w_t, int(offs[i]), sz + (-sz) % LANES) for i, sz in enumerate(sizes)]
    wts = (norm1_w[0], tuple(w_parts), hg_lb_logits.astype(F32), hg_norm_w[0], w_out[0].astype(BF16), norm2_w[0],
           w_up[0].astype(BF16), w_down[0].astype(BF16), final_norm_w)

    yp, kp, vp, kip, sp = _trunk(x_prompt, mod[:bp], 0, None, wts, (1, 256, 2048, 128, 1024, 512, 1, 512))
    past = (cache_k[0], cache_v[0], cache_ki[0], state_hgrn[0])
    ys, ks, vs, kis, ss = _trunk(x_sample, mod[bp:bp + bs], past_len, past, wts,
                                 (bs // 2, ts, bs * ts, ts, 512, ts, bs, ts))
    return (yp, ys, kp[None], vp[None], kip[None], sp[None], ks[None], vs[None], kis[None], ss[None])
```

```python
import functools
import math

import jax
import jax.numpy as jnp
import numpy as np
from jax import lax
from jax.experimental import pallas as pl
from jax.experimental.pallas import tpu as pltpu

CHUNK = 64
N_HEADS = 16
HEAD_DIM = 128
N_KV_HEADS = 4
GROUP = N_HEADS // N_KV_HEADS
ROT_DIM = HEAD_DIM // 4
ROPE_THETA = 500000.0
IDX_HEADS = 16
IDX_DIM = 128
TOPK_MAX = 256
HG_HEADS = 16
HG_DK = 128
HG_DV = 128
EPS = 1e-6
LANES = 128
SUBLANES = 8
BF16_ROWS = 16
SUB_BLOCK = 8
PROJ_TN = 512
COUNT_UNIT = 512
CNT_ROWS = 64
VT_ROWS = HEAD_DIM + BF16_ROWS
VMEM_LIMIT = 56 * 1024 * 1024
NEG_INF = float("-inf")
POS_INF = float("inf")
LOG2E = math.log2(math.e)

F32 = jnp.float32
BF16 = jnp.bfloat16


def _cparams(sem):
    return pltpu.CompilerParams(dimension_semantics=sem, vmem_limit_bytes=VMEM_LIMIT)


def _dot_nt(a, b):
    return lax.dot_general(a, b, (((1,), (1,)), ((), ())), preferred_element_type=F32)


def _dot_tn(a, b):
    return lax.dot_general(a, b, (((0,), (0,)), ((), ())), preferred_element_type=F32)


def _dot(a, b):
    return jnp.dot(a, b, preferred_element_type=F32)


def _sigmoid(x):
    return 1.0 / (1.0 + jnp.exp(-x))


def _col_reduce(op, x):
    rows, n = x.shape
    if rows > CNT_ROWS and rows % CNT_ROWS == 0:
        x = op(x.reshape(rows // CNT_ROWS, CNT_ROWS, n), axis=0)
    return op(x, axis=0, keepdims=True)


def _ada_kernel(c_ref, w_ref, b_ref, o_ref):
    o_ref[...] = _dot(c_ref[...].astype(BF16), w_ref[...].astype(BF16)) + b_ref[...]


def _ada(c_all, w_ada, b_ada, tn=1024):
    r, d = c_all.shape
    n = w_ada.shape[1]
    return pl.pallas_call(
        _ada_kernel,
        out_shape=jax.ShapeDtypeStruct((r, n), F32),
        grid=(n // tn,),
        in_specs=[pl.BlockSpec((r, d), lambda j: (0, 0)),
                  pl.BlockSpec((d, tn), lambda j: (0, j)),
                  pl.BlockSpec((1, tn), lambda j: (0, j))],
        out_specs=pl.BlockSpec((r, tn), lambda j: (0, j)),
        compiler_params=_cparams(("arbitrary",)),
        name="ada",
    )(c_all, w_ada, b_ada)


def _normmod_kernel(x_ref, sc_ref, sh_ref, g_ref, o_ref):
    x = x_ref[...]
    ms = jnp.mean(x * x, axis=-1, keepdims=True)
    xn = x * lax.rsqrt(ms + EPS) * g_ref[...]
    o_ref[...] = (xn * (1.0 + sc_ref[...]) + sh_ref[...]).astype(o_ref.dtype)


def _normmod(x, sc, sh, g, bb, tb):
    b, t, d = x.shape
    return pl.pallas_call(
        _normmod_kernel,
        out_shape=jax.ShapeDtypeStruct((b, t, d), BF16),
        grid=(b // bb, t // tb),
        in_specs=[pl.BlockSpec((bb, tb, d), lambda i, j: (i, j, 0)),
                  pl.BlockSpec((bb, 1, d), lambda i, j: (i, 0, 0)),
                  pl.BlockSpec((bb, 1, d), lambda i, j: (i, 0, 0)),
                  pl.BlockSpec((1, 1, d), lambda i, j: (0, 0, 0))],
        out_specs=pl.BlockSpec((bb, tb, d), lambda i, j: (i, j, 0)),
        compiler_params=_cparams(("arbitrary", "arbitrary")),
        name="normmod",
    )(x, sc, sh, g)


def _proj_plain_kernel(h_ref, w_ref, *o_refs, scale):
    z = _dot_nt(h_ref[...], w_ref[...].astype(BF16))
    if scale != 1.0:
        z = z * scale
    for o_ref in o_refs:
        o_ref[...] = z.astype(o_ref.dtype)


def _store_head_rows(o_ref, r):
    heads = r.shape[1] // LANES
    for g in range(heads):
        o_ref[pl.ds(g, r.shape[0], stride=heads), :] = r[:, g * LANES:(g + 1) * LANES].astype(o_ref.dtype)


def _proj_rope_kernel(h_ref, w_ref, cos_ref, sup_ref, sdn_ref, *o_refs, scale):
    z = _dot_nt(h_ref[...], w_ref[...].astype(BF16))
    tn = z.shape[1]
    reps = tn // LANES

    def wide(ref):
        t = ref[...]
        return t if reps == 1 else jnp.concatenate([t] * reps, axis=1)

    up = pltpu.roll(z, tn - ROT_DIM // 2, 1)
    dn = pltpu.roll(z, ROT_DIM // 2, 1)
    r = z * wide(cos_ref) + up * wide(sup_ref) + dn * wide(sdn_ref)
    for o_ref in o_refs:
        if o_ref.shape[0] != r.shape[0]:
            _store_head_rows(o_ref, r)
        elif o_ref.dtype == BF16 and scale != 1.0:
            o_ref[...] = (r * scale).astype(BF16)
        else:
            o_ref[...] = r.astype(o_ref.dtype)


def _proj_forget_kernel(h_ref, w_ref, lbl_ref, lf_ref, kk_ref):
    z = _dot_nt(h_ref[...], w_ref[...].astype(BF16))
    lbl = lbl_ref[...]
    mx = jnp.max(lbl, axis=0, keepdims=True)
    e = jnp.exp(lbl - mx)
    lb = e[0:1, :] / jnp.sum(e, axis=0, keepdims=True)
    f = lb + (1.0 - lb) * _sigmoid(z)
    lf_ref[...] = jnp.log(f) * LOG2E
    kk_ref[...] = 1.0 - f


def _proj(kind, h, wspec, tm, tn, extra=(), out_dtypes=(F32,), scale=1.0, head_rows=()):
    w, c0, n = wspec
    m, k = h.shape
    assert c0 % SUBLANES == 0 and n % tn == 0
    in_specs = [pl.BlockSpec((tm, k), lambda i, j: (i, 0)),
                pl.BlockSpec((pl.Element(tn), pl.Element(k)),
                             lambda i, j: (pl.multiple_of(c0 + j * tn, SUBLANES), 0))]
    if kind == "rope":
        kern = functools.partial(_proj_rope_kernel, scale=scale)
        in_specs += [pl.BlockSpec((tm, LANES), lambda i, j: (i, 0))] * 3
    elif kind == "forget":
        kern = _proj_forget_kernel
        in_specs += [pl.BlockSpec((extra[0].shape[0], tn), lambda i, j: (0, j))]
    else:
        kern = functools.partial(_proj_plain_kernel, scale=scale)
    hpt = tn // LANES
    outs, out_specs = [], []
    for o, dt in enumerate(out_dtypes):
        if o in head_rows:
            assert n == tn
            outs.append(jax.ShapeDtypeStruct((m * hpt, LANES), dt))
            out_specs.append(pl.BlockSpec((tm * hpt, LANES), lambda i, j: (i, 0)))
        else:
            outs.append(jax.ShapeDtypeStruct((m, n), dt))
            out_specs.append(pl.BlockSpec((tm, tn), lambda i, j: (i, j)))
    res = pl.pallas_call(
        kern,
        out_shape=tuple(outs),
        grid=(m // tm, n // tn),
        in_specs=in_specs,
        out_specs=tuple(out_specs),
        compiler_params=_cparams(("arbitrary", "arbitrary")),
        name="proj_" + kind,
    )(h, w, *extra)
    return res


def _proj_v_kernel(h_ref, w_ref, vf_ref, vt_ref, *, wn):
    z = _dot_nt(h_ref[...], w_ref[...].astype(BF16))
    _store_head_rows(vf_ref, z)
    zt = z.T
    ones = jnp.ones((BF16_ROWS, wn), BF16)
    for u in range(z.shape[0] // wn):
        blk = zt[:, u * wn:(u + 1) * wn].astype(BF16)
        for g in range(N_KV_HEADS):
            vt_ref[u, g * VT_ROWS:g * VT_ROWS + HEAD_DIM, :] = blk[g * HEAD_DIM:(g + 1) * HEAD_DIM]
            vt_ref[u, g * VT_ROWS + HEAD_DIM:(g + 1) * VT_ROWS, :] = ones


def _proj_v(h, wspec, tm, wn):
    w, c0, n = wspec
    m, k = h.shape
    assert c0 % SUBLANES == 0
    return pl.pallas_call(
        functools.partial(_proj_v_kernel, wn=wn),
        out_shape=(jax.ShapeDtypeStruct((m * N_KV_HEADS, HEAD_DIM), F32),
                   jax.ShapeDtypeStruct((m // wn, N_KV_HEADS * VT_ROWS, wn), BF16)),
        grid=(m // tm,),
        in_specs=[pl.BlockSpec((tm, k), lambda i: (i, 0)),
                  pl.BlockSpec((pl.Element(n), pl.Element(k)), lambda i: (c0, 0))],
        out_specs=(pl.BlockSpec((tm * N_KV_HEADS, HEAD_DIM), lambda i: (i, 0)),
                   pl.BlockSpec((tm // wn, N_KV_HEADS * VT_ROWS, wn), lambda i: (i, 0, 0))),
        compiler_params=_cparams(("arbitrary",)),
        name="proj_v",
    )(h, w)


def _dsa_kernel(*refs, tq, tk, t_new, l_cache, pos0, topk, has_cache):
    if has_cache:
        (qi_ref, wi_ref, q_ref, kin_ref, kn_ref, vtn_ref, cki_ref, ck_ref, cv_ref, o_ref,
         sc_ref, qit_ref, wt_ref, qt_ref, acc_ref, s_ref, p_ref, tau_ref, sct_ref) = refs
    else:
        (qi_ref, wi_ref, q_ref, kin_ref, kn_ref, vtn_ref, o_ref,
         sc_ref, qit_ref, wt_ref, qt_ref, acc_ref, s_ref, p_ref, tau_ref, sct_ref) = refs
        cki_ref = ck_ref = cv_ref = None
    i = pl.program_id(1)
    ntc = l_cache // tk
    wn = min(tk, t_new)
    nq = GROUP * tq
    q0 = pos0 + i * tq
    lane_q = lax.broadcasted_iota(jnp.int32, (1, tq), 1)
    qend = (((q0 + lane_q) >> 6) + 1) << 6
    last_end = (((q0 + tq - 1) >> 6) + 1) << 6
    nvis_new = jnp.minimum(last_end - pos0, t_new)
    ntn = (nvis_new + wn - 1) // wn
    ntiles = ntc + ntn

    def tpose(x):
        if tq == LANES:
            return x.T
        xf = x.astype(F32)
        if tq < LANES:
            xf = jnp.concatenate([xf, jnp.zeros((LANES - tq, LANES), F32)], axis=0)
        xt = xf.T
        return xt[:, :tq] if tq < LANES else xt

    qi_blk = qi_ref[0]
    for h in range(IDX_HEADS):
        qit_ref[:, h * tq:(h + 1) * tq] = tpose(qi_blk[:, h * IDX_DIM:(h + 1) * IDX_DIM]).astype(BF16)
    wt_ref[...] = tpose(wi_ref[0])
    q_blk = q_ref[0]
    for g in range(N_KV_HEADS):
        for a in range(GROUP):
            hh = GROUP * g + a
            qt_ref[g, :, a * tq:(a + 1) * tq] = tpose(q_blk[:, hh * HEAD_DIM:(hh + 1) * HEAD_DIM]).astype(BF16)

    def score_tile(ki_tile, kpos0, w):
        acc = jnp.zeros((w, tq), F32)
        hpd = 2 if 2 * tq % LANES == 0 else 1
        for hp in range(IDX_HEADS // hpd):
            lg = _dot(ki_tile, qit_ref[:, hpd * hp * tq:(hpd * hp + hpd) * tq])
            for e in range(hpd):
                h = hpd * hp + e
                acc = acc + wt_ref[h:h + 1, :] * jnp.maximum(lg[:, e * tq:(e + 1) * tq], 0.0)
        kpos = kpos0 + lax.broadcasted_iota(jnp.int32, (w, 1), 0)
        vis = kpos < qend
        s = jnp.where(vis, acc, NEG_INF)
        smin = _col_reduce(jnp.min, jnp.where(vis, acc, POS_INF))
        smax = _col_reduce(jnp.max, s)
        return s, smax, smin

    by_rows = tq < LANES and t_new <= tk

    def square_t(x, fill):
        r, c_ = x.shape
        if c_ < LANES:
            x = jnp.concatenate([x, jnp.full((r, LANES - c_), fill, F32)], axis=1)
        if r < LANES:
            x = jnp.concatenate([x, jnp.full((LANES - r, LANES), fill, F32)], axis=0)
        return x.T

    def stash_rows(t, s):
        st = square_t(s, NEG_INF)[:tq, :]
        if st.shape[1] < tk:
            sct_ref[t] = jnp.full((tq, tk), NEG_INF, F32)
            sct_ref[t, :, 0:st.shape[1]] = st
        else:
            sct_ref[t] = st

    def p1_cache(t, carry):
        mx, mn = carry
        r0 = pl.multiple_of(t * tk, tk)
        s, smax, smin = score_tile(cki_ref[0, pl.ds(r0, tk), :].astype(BF16), r0, tk)
        sc_ref[t] = s
        if by_rows:
            stash_rows(t, s)
        return jnp.maximum(mx, smax), jnp.minimum(mn, smin)

    def p1_new(j, carry):
        mx, mn = carry
        r0 = pl.multiple_of(j * wn, wn)
        s, smax, smin = score_tile(kin_ref[0, pl.ds(r0, wn), :], pos0 + r0, wn)
        if wn == tk:
            sc_ref[ntc + j] = s
        else:
            sc_ref[ntc + j] = jnp.full((tk, tq), NEG_INF, F32)
            sc_ref[ntc + j, 0:wn, :] = s
        if by_rows:
            stash_rows(ntc + j, s)
        return jnp.maximum(mx, smax), jnp.minimum(mn, smin)

    carry = (jnp.full((1, tq), NEG_INF, F32), jnp.full((1, tq), POS_INF, F32))
    if has_cache:
        carry = lax.fori_loop(0, ntc, p1_cache, carry)
    mx, mn = lax.fori_loop(0, ntn, p1_new, carry)

    nvis = jnp.clip(qend - pos0, 0, t_new)
    if has_cache:
        nvis = nvis + jnp.minimum(qend, l_cache)
    act0 = jnp.where(nvis > topk, 1.0, 0.0)
    kf = float(topk)
    cu = min(tk, COUNT_UNIT)
    upt = tk // cu
    n_units = ntc * upt + (nvis_new + cu - 1) // cu

    def bis_cond(carry):
        return jnp.logical_and(carry[0] < 400, carry[1] > 0)

    def unit(u):
        return sc_ref.at[u // upt, pl.ds(pl.multiple_of((u % upt) * cu, cu), cu), :]

    def count_where(pred):
        def body(u, cnt):
            ind = jnp.where(pred(unit(u)[...]), 1.0, 0.0)
            return cnt + jnp.sum(ind.reshape(cu // CNT_ROWS, CNT_ROWS, tq), axis=0)

        cnt = lax.fori_loop(0, n_units, body, jnp.zeros((CNT_ROWS, tq), F32))
        return jnp.sum(cnt, axis=0, keepdims=True)

    def count_ge_rows(mid):
        ind = jnp.where(sct_ref[...] >= mid[None], 1.0, 0.0)
        return jnp.sum(jnp.sum(ind, axis=0), axis=1, keepdims=True)

    def to_col(r):
        return square_t(jnp.broadcast_to(r, (LANES, tq)), 0.0)[:tq, 0:1]

    def to_row(col):
        return square_t(jnp.broadcast_to(col, (tq, LANES)), 0.0)[0:1, :tq]

    count_ge = count_ge_rows if by_rows else (lambda mid: count_where(lambda s: s >= mid))

    def bis_body(carry):
        it, _, lo, hi, tau, act, tied = carry
        mid = lo * 0.5 + hi * 0.5
        c = count_ge(mid)
        found = jnp.where(c == kf, act, 0.0)
        stuck = jnp.where(jnp.logical_or(mid <= lo, mid >= hi), act, 0.0) * (1.0 - found)
        above = jnp.where(c > kf, act, 0.0) * (1.0 - stuck)
        below = act * (1.0 - above) * (1.0 - found) * (1.0 - stuck)
        tau = jnp.where(found > 0, mid, tau)
        tau = jnp.where(stuck > 0, lo, tau)
        lo = jnp.where(above > 0, mid, lo)
        hi = jnp.where(below > 0, mid, hi)
        nact = act * (1.0 - found) * (1.0 - stuck)
        return it + 1, (jnp.max(nact) > 0).astype(jnp.int32), lo, hi, tau, nact, jnp.maximum(tied, stuck)

    flag0 = (jnp.max(act0) > 0).astype(jnp.int32)
    if by_rows:
        mn_c = to_col(mn)
        init = (jnp.int32(0), flag0, mn_c, to_col(mx), mn_c, to_col(act0), jnp.zeros((tq, 1), F32))
    else:
        init = (jnp.int32(0), flag0, mn, mx, mn, act0, jnp.zeros((1, tq), F32))
    res = lax.while_loop(bis_cond, lambda carry: bis_body(bis_body(carry)), init)
    tau_ref[...] = to_row(res[4]) if by_rows else res[4]
    tied = to_row(res[6]) if by_rows else res[6]

    @pl.when(jnp.max(tied) > 0)
    def _():
        lo_t = tau_ref[...]

        def tmin_body(u, acc):
            s = unit(u)[...]
            return jnp.minimum(acc, _col_reduce(jnp.min, jnp.where(s >= lo_t, s, POS_INF)))

        tstar = lax.fori_loop(0, n_units, tmin_body, jnp.full((1, tq), POS_INF, F32))
        tstar = jnp.where(tied > 0, tstar, lo_t)
        need = kf - count_where(lambda s: s > tstar)
        ri = lax.broadcasted_iota(jnp.int32, (cu, cu), 0)
        ci = lax.broadcasted_iota(jnp.int32, (cu, cu), 1)
        tri = jnp.where(ri >= ci, 1.0, 0.0).astype(BF16)

        def strike_body(u, run):
            s = unit(u)[...]
            eq = jnp.where(jnp.logical_and(s == tstar, tied > 0), 1.0, 0.0)
            rank = run + _dot(tri, eq.astype(BF16))
            unit(u)[...] = jnp.where(jnp.logical_and(eq > 0, rank > need), NEG_INF, s)
            return rank[cu - 1:cu, :]

        lax.fori_loop(0, n_units, strike_body, jnp.zeros((1, tq), F32))
        tau_ref[...] = tstar

    tau = tau_ref[...]

    acc_ref[...] = jnp.zeros(acc_ref.shape, F32)

    def attend(t, k_of_g, vt_of_g, w, ms):
        s_idx = sc_ref[t]
        if w < tk:
            s_idx = s_idx[0:w]
        bias = jnp.where(s_idx >= tau, 0.0, NEG_INF)
        bias4 = jnp.concatenate([bias] * GROUP, axis=1)
        def qk(g):
            s = _dot(k_of_g(g), qt_ref[g]) + bias4
            s_ref[g % 2, 0:w] = s
            return _col_reduce(jnp.max, s)

        out = []
        tile_max = qk(0)
        for g in range(N_KV_HEADS):
            slot = g % 2
            next_max = qk(g + 1) if g + 1 < N_KV_HEADS else None
            m_old = ms[g]
            m_new = jnp.maximum(m_old, tile_max)
            m_safe = jnp.where(m_new == NEG_INF, 0.0, m_new)
            alpha = jnp.exp2(m_old - m_safe)
            p_ref[slot, 0:w] = jnp.exp2(s_ref[slot, 0:w] - m_safe).astype(BF16)
            acc_ref[g] = acc_ref[g] * alpha + _dot(vt_of_g(g), p_ref[slot, 0:w])
            out.append(m_new)
            tile_max = next_max
        return tuple(out)

    def p3_cache(t, ms):
        r0 = pl.multiple_of(t * tk, tk)

        def vt(g):
            vg = cv_ref[0, pl.ds(r0 * N_KV_HEADS + g, tk, stride=N_KV_HEADS), :]
            return jnp.concatenate([vg.T, jnp.ones((BF16_ROWS, tk), F32)], axis=0).astype(BF16)

        return attend(
            t, lambda g: ck_ref[0, pl.ds(r0 * N_KV_HEADS + g, tk, stride=N_KV_HEADS), :].astype(BF16), vt, tk, ms)

    def p3_new(j, ms):
        r0 = pl.multiple_of(j * wn, wn)
        return attend(ntc + j,
                      lambda g: kn_ref[0, pl.ds(r0, wn), g * HEAD_DIM:(g + 1) * HEAD_DIM],
                      lambda g: vtn_ref[0, j, g * VT_ROWS:(g + 1) * VT_ROWS, :], wn, ms)

    ms = tuple(jnp.full((1, nq), NEG_INF, F32) for _ in range(N_KV_HEADS))
    if has_cache:
        ms = lax.fori_loop(0, ntc, p3_cache, ms)
    lax.fori_loop(0, ntn, p3_new, ms)

    for g in range(N_KV_HEADS):
        acc = acc_ref[g]
        og = (acc[0:HEAD_DIM] / acc[HEAD_DIM:HEAD_DIM + 1]).T
        for a in range(GROUP):
            hh = GROUP * g + a
            o_ref[0, :, hh * HEAD_DIM:(hh + 1) * HEAD_DIM] = og[a * tq:(a + 1) * tq, :].astype(o_ref.dtype)


def _dsa(qi, wi, q, kin, kn, vtn, cache, pos0, tq, tk):
    b, t, _ = q.shape
    has_cache = cache is not None
    l_cache = cache[0].shape[1] if has_cache else 0
    topk = min(TOPK_MAX, (l_cache + t) // 4)
    wn = min(tk, t)
    ntn = t // wn
    nt = l_cache // tk + ntn
    kvd = N_KV_HEADS * HEAD_DIM
    nq = GROUP * tq
    res = pl.Buffered(1) if b == 1 else None
    in_specs = [pl.BlockSpec((1, tq, IDX_HEADS * IDX_DIM), lambda bb, i: (bb, i, 0)),
                pl.BlockSpec((1, tq, LANES), lambda bb, i: (bb, i, 0)),
                pl.BlockSpec((1, tq, N_HEADS * HEAD_DIM), lambda bb, i: (bb, i, 0)),
                pl.BlockSpec((1, t, IDX_DIM), lambda bb, i: (bb, 0, 0), pipeline_mode=res),
                pl.BlockSpec((1, t, kvd), lambda bb, i: (bb, 0, 0), pipeline_mode=res),
                pl.BlockSpec((1, ntn, N_KV_HEADS * VT_ROWS, wn), lambda bb, i: (bb, 0, 0, 0), pipeline_mode=res)]
    args = [qi, wi, q, kin, kn, vtn]
    if has_cache:
        in_specs += [pl.BlockSpec((1, l_cache, IDX_DIM), lambda bb, i: (bb, 0, 0)),
                     pl.BlockSpec((1, l_cache * N_KV_HEADS, HEAD_DIM), lambda bb, i: (bb, 0, 0)),
                     pl.BlockSpec((1, l_cache * N_KV_HEADS, HEAD_DIM), lambda bb, i: (bb, 0, 0))]
        args += list(cache)
    kern = functools.partial(_dsa_kernel, tq=tq, tk=tk, t_new=t, l_cache=l_cache, pos0=pos0, topk=topk,
                             has_cache=has_cache)
    return pl.pallas_call(
        kern,
        out_shape=jax.ShapeDtypeStruct((b, t, N_HEADS * HEAD_DIM), BF16),
        grid=(b, t // tq),
        in_specs=in_specs,
        out_specs=pl.BlockSpec((1, tq, N_HEADS * HEAD_DIM), lambda bb, i: (bb, i, 0)),
        scratch_shapes=[pltpu.VMEM((nt, tk, tq), F32),
                        pltpu.VMEM((IDX_DIM, IDX_HEADS * tq), BF16),
                        pltpu.VMEM((LANES, tq), F32),
                        pltpu.VMEM((N_KV_HEADS, HEAD_DIM, nq), BF16),
                        pltpu.VMEM((N_KV_HEADS, VT_ROWS, nq), F32),
                        pltpu.VMEM((2, tk, nq), F32),
                        pltpu.VMEM((2, tk, nq), BF16),
                        pltpu.VMEM((1, tq), F32),
                        pltpu.VMEM((nt if (tq < LANES and t <= tk) else 1, tq, tk), F32)],
        compiler_params=_cparams(("arbitrary", "arbitrary")),
        name="dsa",
    )(*args)


def _hgrn_kernel(*refs, c, nchunk, nh, has_state):
    if has_state:
        q_ref, lf_ref, kk_ref, v_ref, hg_ref, nw_ref, s0_ref, o_ref, sout_ref, st_ref, g_ref = refs
    else:
        q_ref, lf_ref, kk_ref, v_ref, hg_ref, nw_ref, o_ref, sout_ref, st_ref, g_ref = refs
        s0_ref = None
    r = pl.program_id(2)

    @pl.when(r == 0)
    def _():
        for hh in range(nh):
            if has_state:
                st_ref[hh] = s0_ref[0, hh].T
            else:
                st_ref[hh] = jnp.zeros((HG_DV, HG_DK), F32)

    ri = lax.broadcasted_iota(jnp.int32, (c, c), 0)
    ci = lax.broadcasted_iota(jnp.int32, (c, c), 1)
    tri = jnp.where(ri >= ci, 1.0, 0.0).astype(BF16)
    nsub = c // SUB_BLOCK
    pair = lax.broadcasted_iota(jnp.int32, (SUB_BLOCK * SUB_BLOCK, HG_DK), 0)
    cap = jnp.where((pair % SUB_BLOCK) >= (pair // SUB_BLOCK), 0.0, NEG_INF)
    ones_w = jnp.ones((HG_DK, LANES), BF16)

    def rep_rows(row):
        return jnp.concatenate(
            [jnp.broadcast_to(row(sg), (SUB_BLOCK, HG_DK)) for sg in range(SUB_BLOCK)], axis=0)

    def cumulative(base):
        lf = lf_ref[pl.ds(base, c), :]
        l1 = lf.astype(BF16)
        r1 = lf - l1.astype(F32)
        l2 = r1.astype(BF16)
        l3 = (r1 - l2.astype(F32)).astype(BF16)
        return _dot(tri, l1) + _dot(tri, l2) + _dot(tri, l3)

    def stage_a(base, hh, g_all):
        cs = slice(hh * HG_DK, (hh + 1) * HG_DK)
        G = g_all[:, cs]
        kk = kk_ref[pl.ds(base, c), cs]
        v = v_ref[pl.ds(base, c), cs].astype(F32)
        g_ref[0, hh] = G
        g_ref[1, hh] = kk
        g_ref[2, hh] = v
        return dict(cs=cs, G=G, hh=hh, q=q_ref[pl.ds(base, c), cs].astype(F32), kk=kk, v=v)

    def stage_b(hh, d):
        q, kk, v, G = d["q"], d["kk"], d["v"], d["G"]
        st = st_ref[hh]
        vb = v.astype(BF16)
        d["vb"] = vb
        d["o_inter"] = _dot_nt((q * jnp.exp2(G)).astype(BF16), st.astype(BF16))
        d["A"], diag = [], []
        for i in range(nsub):
            lo_, hi_ = i * SUB_BLOCK, (i + 1) * SUB_BLOCK
            qi_ = q[lo_:hi_]
            Gi = G[lo_:hi_]
            if i > 0:
                Gb = G[lo_ - 1:lo_]
                qt = qi_ * jnp.exp2(Gi - Gb)
                kt = kk[:lo_] * jnp.exp2(Gb - G[:lo_])
                a_blk = _dot_nt(qt.astype(BF16), kt.astype(BF16))
                d["A"].append(jnp.concatenate([a_blk, jnp.zeros((SUB_BLOCK, c - lo_), F32)], axis=1))
            qrep = jnp.concatenate([qi_] * SUB_BLOCK, axis=0)
            grep = jnp.concatenate([Gi] * SUB_BLOCK, axis=0)
            kk_rep = rep_rows(lambda sg: g_ref[1, hh, lo_ + sg:lo_ + sg + 1, :])
            g_rep = rep_rows(lambda sg: g_ref[0, hh, lo_ + sg:lo_ + sg + 1, :])
            diag.append((qrep * kk_rep * jnp.exp2(jnp.minimum(grep - g_rep, cap))).astype(BF16))
        rs = _dot(jnp.concatenate(diag, axis=0) if nsub > 1 else diag[0], ones_w)
        pairs = SUB_BLOCK * SUB_BLOCK
        d["rs"] = [rs[i * pairs:(i + 1) * pairs] for i in range(nsub)]
        Gl = G[c - 1:c]
        kdec = kk * jnp.exp2(Gl - G)
        st_ref[hh] = st * jnp.exp2(Gl) + _dot_tn(vb, kdec.astype(BF16))

    def stage_c(base, d):
        parts = []
        o_base = d["o_inter"]
        if nsub > 1:
            a_low = jnp.concatenate([jnp.zeros((SUB_BLOCK, c), F32)] + d["A"], axis=0)
            o_base = o_base + _dot(a_low.astype(BF16), d["vb"])
        for i in range(nsub):
            lo_, hi_ = i * SUB_BLOCK, (i + 1) * SUB_BLOCK
            oi = o_base[lo_:hi_]
            v_rep = rep_rows(lambda sg: g_ref[2, d["hh"], lo_ + sg:lo_ + sg + 1, :])
            contrib = (d["rs"][i] * v_rep).reshape(SUB_BLOCK, SUB_BLOCK, HG_DV)
            parts.append(oi + jnp.sum(contrib, axis=0))
        o = jnp.concatenate(parts, axis=0) if nsub > 1 else parts[0]
        on = o * lax.rsqrt(jnp.mean(o * o, axis=1, keepdims=True) + EPS) * nw_ref[...]
        hg = hg_ref[pl.ds(base, c), d["cs"]].astype(F32)
        o_ref[pl.ds(base, c), d["cs"]] = (on * (hg * _sigmoid(hg))).astype(o_ref.dtype)

    def chunk(n, carry):
        base = pl.multiple_of(n * c, c)
        g_all = cumulative(base)
        heads = [stage_a(base, hh, g_all) for hh in range(nh)]
        for hh, d in enumerate(heads):
            stage_b(hh, d)
        for d in heads:
            stage_c(base, d)
        return carry

    lax.fori_loop(0, nchunk, chunk, 0)

    @pl.when(r == pl.num_programs(2) - 1)
    def _():
        for hh in range(nh):
            sout_ref[0, hh] = st_ref[hh].T


def _hgrn(hq, lf, kk, hv, hg, nw, s0, b, t, rb, nh):
    c = min(CHUNK, t)
    nchunk = rb // c
    nr = t // rb
    has_state = s0 is not None
    blk = pl.BlockSpec((rb, nh * HG_DK), lambda bb, h, r: (bb * nr + r, h))
    sblk = pl.BlockSpec((1, nh, HG_DK, HG_DV), lambda bb, h, r: (bb, h, 0, 0))
    in_specs = [blk, blk, blk, blk, blk, pl.BlockSpec((1, HG_DV), lambda bb, h, r: (0, 0))]
    args = [hq, lf, kk, hv, hg, nw]
    if has_state:
        in_specs.append(sblk)
        args.append(s0)
    kern = functools.partial(_hgrn_kernel, c=c, nchunk=nchunk, nh=nh, has_state=has_state)
    return pl.pallas_call(
        kern,
        out_shape=(jax.ShapeDtypeStruct((b * t, HG_HEADS * HG_DV), BF16),
                   jax.ShapeDtypeStruct((b, HG_HEADS, HG_DK, HG_DV), F32)),
        grid=(b, HG_HEADS // nh, nr),
        in_specs=in_specs,
        out_specs=(blk, sblk),
        scratch_shapes=[pltpu.VMEM((nh, HG_DV, HG_DK), F32), pltpu.VMEM((3, nh, c, HG_DK), F32)],
        compiler_params=_cparams(("arbitrary", "arbitrary", "arbitrary")),
        name="hgrn",
    )(*args)


def _merge_out_kernel(x_ref, oa_ref, oh_ref, ga_ref, gb_ref, w_ref, g1_ref, sc_ref, sh_ref, nw_ref,
                      x1_ref, h2_ref):
    bb, tb, d = x_ref.shape
    merged = (_sigmoid(ga_ref[...].astype(F32)) * oa_ref[...].astype(F32)
              + _sigmoid(gb_ref[...].astype(F32)) * oh_ref[...].astype(F32))
    y = _dot(merged.astype(BF16), w_ref[...]).reshape(bb, tb, d)
    x1 = x_ref[...] + g1_ref[...] * y
    x1_ref[...] = x1
    ms = jnp.mean(x1 * x1, axis=-1, keepdims=True)
    xn = x1 * lax.rsqrt(ms + EPS) * nw_ref[...]
    h2 = xn * (1.0 + sc_ref[...]) + sh_ref[...]
    h2_ref[...] = h2.reshape(bb * tb, d).astype(BF16)


def _merge_out(x, oa, oh, ga, gb, w_out, g1, sc2, sh2, nw2, bb, tb):
    b, t, d = x.shape
    tm = bb * tb
    nt = t // tb

    def row(i, j):
        return (i * nt + j, 0)

    def mod(i, j):
        return (i, 0, 0)

    x3 = pl.BlockSpec((bb, tb, d), lambda i, j: (i, j, 0))
    r2 = pl.BlockSpec((tm, d), row)
    return pl.pallas_call(
        _merge_out_kernel,
        out_shape=(jax.ShapeDtypeStruct((b, t, d), F32), jax.ShapeDtypeStruct((b * t, d), BF16)),
        grid=(b // bb, nt),
        in_specs=[x3, r2, r2, r2, r2,
                  pl.BlockSpec((d, d), lambda i, j: (0, 0)),
                  pl.BlockSpec((bb, 1, d), mod), pl.BlockSpec((bb, 1, d), mod), pl.BlockSpec((bb, 1, d), mod),
                  pl.BlockSpec((1, 1, d), lambda i, j: (0, 0, 0))],
        out_specs=(x3, r2),
        compiler_params=_cparams(("arbitrary", "arbitrary")),
        name="merge_out",
    )(x, oa, oh, ga, gb, w_out, g1, sc2, sh2, nw2)


def _mlp_kernel(h_ref, wu_ref, wd_ref, x1_ref, g2_ref, fw_ref, y_ref, acc_ref):
    f = pl.program_id(2)

    @pl.when(f == 0)
    def _():
        acc_ref[...] = jnp.zeros(acc_ref.shape, F32)

    u = jnp.maximum(_dot(h_ref[...], wu_ref[...]), 0.0)
    acc_ref[...] += _dot((u * u).astype(BF16), wd_ref[...])

    @pl.when(f == pl.num_programs(2) - 1)
    def _():
        bb, tb, d = x1_ref.shape
        x2 = x1_ref[...] + g2_ref[...] * acc_ref[...].reshape(bb, tb, d)
        ms = jnp.mean(x2 * x2, axis=-1, keepdims=True)
        y_ref[...] = x2 * lax.rsqrt(ms + EPS) * fw_ref[...]


def _mlp(h2, w_up, w_down, x1, g2, fw, bb, tb, tf):
    b, t, d = x1.shape
    dff = w_up.shape[1]
    tm = bb * tb
    nt = t // tb
    x3 = pl.BlockSpec((bb, tb, d), lambda i, j, f: (i, j, 0))
    return pl.pallas_call(
        _mlp_kernel,
        out_shape=jax.ShapeDtypeStruct((b, t, d), F32),
        grid=(b // bb, nt, dff // tf),
        in_specs=[pl.BlockSpec((tm, d), lambda i, j, f: (i * nt + j, 0)),
                  pl.BlockSpec((d, tf), lambda i, j, f: (0, f)),
                  pl.BlockSpec((tf, d), lambda i, j, f: (f, 0)),
                  x3,
                  pl.BlockSpec((bb, 1, d), lambda i, j, f: (i, 0, 0)),
                  pl.BlockSpec((1, 1, d), lambda i, j, f: (0, 0, 0))],
        out_specs=x3,
        scratch_shapes=[pltpu.VMEM((tm, d), F32)],
        compiler_params=_cparams(("arbitrary", "arbitrary", "arbitrary")),
        name="mlp",
    )(h2, w_up, w_down, x1, g2, fw)


def _rope_tables(pos):
    half = ROT_DIM // 2
    inv_freq = ROPE_THETA ** (-(jnp.arange(half, dtype=F32) * (2.0 / ROT_DIM)))
    ang = pos.astype(F32)[:, None] * inv_freq[None, :]
    cos, sin = jnp.cos(ang), jnp.sin(ang)
    n = pos.shape[0]
    ones = jnp.ones((n, LANES - ROT_DIM), F32)
    zeros = jnp.zeros((n, LANES - ROT_DIM), F32)
    zh = jnp.zeros((n, half), F32)
    c_t = jnp.concatenate([cos, cos, ones], axis=1)
    s_up = jnp.concatenate([-sin, zh, zeros], axis=1)
    s_dn = jnp.concatenate([zh, sin, zeros], axis=1)
    return c_t, s_up, s_dn


def _trunk(x, mod, pos0, past, wts, blocks):
    (norm1_w, w_parts, lb_logits, hg_norm_w, w_out, norm2_w, w_up, w_down, final_w) = wts
    b, t, d = x.shape
    bb, tb, tm_proj, tq, tk, rb, mlp_bb, mlp_tb = blocks
    m = [mod[:, i:i + 1, :] for i in range(6)]
    sh1, sc1, g1, sh2, sc2, g2 = m
    h = _normmod(x, sc1, sh1, norm1_w.reshape(1, 1, d), bb, tb).reshape(b * t, d)

    pos = pos0 + jnp.arange(t, dtype=jnp.int32)
    tabs = tuple(jnp.tile(tb_, (b, 1)) for tb_ in _rope_tables(pos))
    wq, wk, wv, wqi, wki, wwi, whq, whf, whi, whg, wga, wgb = w_parts
    tm = min(tm_proj, b * t)
    wn = min(tk, t)
    (q_bf,) = _proj("rope", h, wq, tm, PROJ_TN, tabs, (BF16,), scale=HEAD_DIM ** -0.5 * LOG2E)
    k_f, k_bf = _proj("rope", h, wk, tm, PROJ_TN, tabs, (F32, BF16), head_rows=(0,))
    v_f, vt_bf = _proj_v(h, wv, tm, wn)
    (qi_bf,) = _proj("rope", h, wqi, tm, PROJ_TN, tabs, (BF16,))
    ki_f, ki_bf = _proj("rope", h, wki, tm, LANES, tabs, (F32, BF16))
    (wi_f,) = _proj("plain", h, wwi, tm, LANES, (), (F32,), scale=IDX_HEADS ** -0.5 * IDX_DIM ** -0.5)
    (hq,) = _proj("plain", h, whq, tm, PROJ_TN, (), (BF16,))
    lf, kk = _proj("forget", h, whf, tm, PROJ_TN, (lb_logits,), (F32, F32))
    (hi,) = _proj("plain", h, whi, tm, PROJ_TN, (), (BF16,))
    (hg,) = _proj("plain", h, whg, tm, PROJ_TN, (), (BF16,))
    (ga,) = _proj("plain", h, wga, tm, PROJ_TN, (), (BF16,))
    (gb,) = _proj("plain", h, wgb, tm, PROJ_TN, (), (BF16,))

    kvd = N_KV_HEADS * HEAD_DIM
    r3 = lambda a: a.reshape(b, t, a.shape[-1])
    if past is None:
        cache, s0 = None, None
    else:
        ck, cv, cki, s0 = past
        lc = ck.shape[1]
        cache = (cki, ck.reshape(b, lc * N_KV_HEADS, HEAD_DIM), cv.reshape(b, lc * N_KV_HEADS, HEAD_DIM))
    vtn = vt_bf.reshape(b, t // wn, N_KV_HEADS * VT_ROWS, wn)
    o_attn = _dsa(r3(qi_bf), r3(wi_f), r3(q_bf), r3(ki_bf), r3(k_bf), vtn, cache, pos0, tq, tk)
    o_hg, s_new = _hgrn(hq, lf, kk, hi, hg, hg_norm_w.reshape(1, HG_DV), s0, b, t, rb, 8)

    x1, h2 = _merge_out(x, o_attn.reshape(b * t, d), o_hg, ga, gb, w_out, g1, sc2, sh2,
                        norm2_w.reshape(1, 1, d), bb, tb)
    y = _mlp(h2, w_up, w_down, x1, g2, final_w.reshape(1, 1, d), mlp_bb, mlp_tb, 512)
    return (y, k_f.reshape(b, t, N_KV_HEADS, HEAD_DIM), v_f.reshape(b, t, N_KV_HEADS, HEAD_DIM),
            ki_f.reshape(b, t, IDX_DIM), s_new)


def kernel(x_prompt, x_sample, cache_k, cache_v, cache_ki, state_hgrn, c_prompt, c_sample, w_ada, b_ada, norm1_w,
           w_in, hg_lb_logits, hg_norm_w, w_out, norm2_w, w_up, w_down, final_norm_w):
    depth = w_in.shape[0]
    assert depth == 1
    d = x_prompt.shape[-1]
    bp, tp, _ = x_prompt.shape
    bs, ts, _ = x_sample.shape
    past_len = cache_k.shape[2]

    c_all = jnp.concatenate([c_prompt, c_sample], axis=0)
    nrow = c_all.shape[0]
    pad = (-nrow) % SUBLANES
    c_all = jnp.pad(c_all, ((0, pad), (0, 0)))
    mod = _ada(c_all, w_ada[0], b_ada[0].reshape(1, -1)).reshape(nrow + pad, 6, d)

    sizes = (N_HEADS * HEAD_DIM, N_KV_HEADS * HEAD_DIM, N_KV_HEADS * HEAD_DIM, IDX_HEADS * IDX_DIM, IDX_DIM,
             IDX_HEADS, HG_HEADS * HG_DK, HG_HEADS * HG_DK, HG_HEADS * HG_DV, HG_HEADS * HG_DV, d, d)
    offs = np.concatenate([[0], np.cumsum(sizes)])
    w_t = jnp.transpose(w_in[0])
    w_parts = [(w_t, int(offs[i]), sz + (-sz) % LANES) for i, sz in enumerate(sizes)]
    wts = (norm1_w[0], tuple(w_parts), hg_lb_logits.astype(F32), hg_norm_w[0], w_out[0].astype(BF16), norm2_w[0],
           w_up[0].astype(BF16), w_down[0].astype(BF16), final_norm_w)

    yp, kp, vp, kip, sp = _trunk(x_prompt, mod[:bp], 0, None, wts, (1, 256, 2048, 128, 1024, 512, 1, 512))
    past = (cache_k[0], cache_v[0], cache_ki[0], state_hgrn[0])
    ys, ks, vs, kis, ss = _trunk(x_sample, mod[bp:bp + bs], past_len, past, wts,
                                 (bs // 2, ts, bs * ts, ts, 512, ts, bs, ts))
    return (yp, ys, kp[None], vp[None], kip[None], sp[None], ks[None], vs[None], kis[None], ss[None])
```

```python
import functools
import math

import jax
import jax.numpy as jnp
import numpy as np
from jax import lax
from jax.experimental import pallas as pl
from jax.experimental.pallas import tpu as pltpu

CHUNK = 64
N_HEADS = 16
HEAD_DIM = 128
N_KV_HEADS = 4
GROUP = N_HEADS // N_KV_HEADS
ROT_DIM = HEAD_DIM // 4
ROPE_THETA = 500000.0
IDX_HEADS = 16
IDX_DIM = 128
TOPK_MAX = 256
HG_HEADS = 16
HG_DK = 128
HG_DV = 128
EPS = 1e-6
LANES = 128
SUBLANES = 8
BF16_ROWS = 16
SUB_BLOCK = 8
PROJ_TN = 512
COUNT_UNIT = 512
STAGE_SLOTS = 3
CNT_ROWS = 64
VT_ROWS = HEAD_DIM + BF16_ROWS
VMEM_LIMIT = 56 * 1024 * 1024
NEG_INF = float("-inf")
POS_INF = float("inf")
LOG2E = math.log2(math.e)

F32 = jnp.float32
BF16 = jnp.bfloat16


def _cparams(sem):
    return pltpu.CompilerParams(dimension_semantics=sem, vmem_limit_bytes=VMEM_LIMIT)


def _dot_nt(a, b):
    return lax.dot_general(a, b, (((1,), (1,)), ((), ())), preferred_element_type=F32)


def _dot_tn(a, b):
    return lax.dot_general(a, b, (((0,), (0,)), ((), ())), preferred_element_type=F32)


def _dot(a, b):
    return jnp.dot(a, b, preferred_element_type=F32)


def _sigmoid(x):
    return 1.0 / (1.0 + jnp.exp(-x))


def _col_reduce(op, x):
    rows, n = x.shape
    if rows > CNT_ROWS and rows % CNT_ROWS == 0:
        x = op(x.reshape(rows // CNT_ROWS, CNT_ROWS, n), axis=0)
    return op(x, axis=0, keepdims=True)


def _ada_kernel(c_ref, w_ref, b_ref, o_ref):
    o_ref[...] = _dot(c_ref[...].astype(BF16), w_ref[...].astype(BF16)) + b_ref[...]


def _ada(c_all, w_ada, b_ada, tn=1024):
    r, d = c_all.shape
    n = w_ada.shape[1]
    return pl.pallas_call(
        _ada_kernel,
        out_shape=jax.ShapeDtypeStruct((r, n), F32),
        grid=(n // tn,),
        in_specs=[pl.BlockSpec((r, d), lambda j: (0, 0)),
                  pl.BlockSpec((d, tn), lambda j: (0, j)),
                  pl.BlockSpec((1, tn), lambda j: (0, j))],
        out_specs=pl.BlockSpec((r, tn), lambda j: (0, j)),
        compiler_params=_cparams(("arbitrary",)),
        name="ada",
    )(c_all, w_ada, b_ada)


def _normmod_kernel(x_ref, sc_ref, sh_ref, g_ref, o_ref):
    x = x_ref[...]
    ms = jnp.mean(x * x, axis=-1, keepdims=True)
    xn = x * lax.rsqrt(ms + EPS) * g_ref[...]
    o_ref[...] = (xn * (1.0 + sc_ref[...]) + sh_ref[...]).astype(o_ref.dtype)


def _normmod(x, sc, sh, g, bb, tb):
    b, t, d = x.shape
    return pl.pallas_call(
        _normmod_kernel,
        out_shape=jax.ShapeDtypeStruct((b, t, d), BF16),
        grid=(b // bb, t // tb),
        in_specs=[pl.BlockSpec((bb, tb, d), lambda i, j: (i, j, 0)),
                  pl.BlockSpec((bb, 1, d), lambda i, j: (i, 0, 0)),
                  pl.BlockSpec((bb, 1, d), lambda i, j: (i, 0, 0)),
                  pl.BlockSpec((1, 1, d), lambda i, j: (0, 0, 0))],
        out_specs=pl.BlockSpec((bb, tb, d), lambda i, j: (i, j, 0)),
        compiler_params=_cparams(("arbitrary", "arbitrary")),
        name="normmod",
    )(x, sc, sh, g)


def _proj_plain_kernel(h_ref, w_ref, *o_refs, scale):
    z = _dot_nt(h_ref[...], w_ref[...].astype(BF16))
    if scale != 1.0:
        z = z * scale
    for o_ref in o_refs:
        o_ref[...] = z.astype(o_ref.dtype)


def _store_head_rows(o_ref, r):
    heads = r.shape[1] // LANES
    for g in range(heads):
        o_ref[pl.ds(g, r.shape[0], stride=heads), :] = r[:, g * LANES:(g + 1) * LANES].astype(o_ref.dtype)


def _proj_rope_kernel(h_ref, w_ref, cos_ref, sup_ref, sdn_ref, *o_refs, scale):
    z = _dot_nt(h_ref[...], w_ref[...].astype(BF16))
    tn = z.shape[1]
    reps = tn // LANES

    def wide(ref):
        t = ref[...]
        return t if reps == 1 else jnp.concatenate([t] * reps, axis=1)

    up = pltpu.roll(z, tn - ROT_DIM // 2, 1)
    dn = pltpu.roll(z, ROT_DIM // 2, 1)
    r = z * wide(cos_ref) + up * wide(sup_ref) + dn * wide(sdn_ref)
    for o_ref in o_refs:
        if o_ref.shape[0] != r.shape[0]:
            _store_head_rows(o_ref, r)
        elif o_ref.dtype == BF16 and scale != 1.0:
            o_ref[...] = (r * scale).astype(BF16)
        else:
            o_ref[...] = r.astype(o_ref.dtype)


def _proj_forget_kernel(h_ref, w_ref, lbl_ref, lf_ref, kk_ref):
    z = _dot_nt(h_ref[...], w_ref[...].astype(BF16))
    lbl = lbl_ref[...]
    mx = jnp.max(lbl, axis=0, keepdims=True)
    e = jnp.exp(lbl - mx)
    lb = e[0:1, :] / jnp.sum(e, axis=0, keepdims=True)
    f = lb + (1.0 - lb) * _sigmoid(z)
    lf_ref[...] = jnp.log(f) * LOG2E
    kk_ref[...] = 1.0 - f


def _proj(kind, h, wspec, tm, tn, extra=(), out_dtypes=(F32,), scale=1.0, head_rows=()):
    w, c0, n = wspec
    m, k = h.shape
    assert c0 % SUBLANES == 0 and n % tn == 0
    in_specs = [pl.BlockSpec((tm, k), lambda i, j: (i, 0)),
                pl.BlockSpec((pl.Element(tn), pl.Element(k)),
                             lambda i, j: (pl.multiple_of(c0 + j * tn, SUBLANES), 0))]
    if kind == "rope":
        kern = functools.partial(_proj_rope_kernel, scale=scale)
        in_specs += [pl.BlockSpec((tm, LANES), lambda i, j: (i, 0))] * 3
    elif kind == "forget":
        kern = _proj_forget_kernel
        in_specs += [pl.BlockSpec((extra[0].shape[0], tn), lambda i, j: (0, j))]
    else:
        kern = functools.partial(_proj_plain_kernel, scale=scale)
    hpt = tn // LANES
    outs, out_specs = [], []
    for o, dt in enumerate(out_dtypes):
        if o in head_rows:
            assert n == tn
            outs.append(jax.ShapeDtypeStruct((m * hpt, LANES), dt))
            out_specs.append(pl.BlockSpec((tm * hpt, LANES), lambda i, j: (i, 0)))
        else:
            outs.append(jax.ShapeDtypeStruct((m, n), dt))
            out_specs.append(pl.BlockSpec((tm, tn), lambda i, j: (i, j)))
    res = pl.pallas_call(
        kern,
        out_shape=tuple(outs),
        grid=(m // tm, n // tn),
        in_specs=in_specs,
        out_specs=tuple(out_specs),
        compiler_params=_cparams(("arbitrary", "arbitrary")),
        name="proj_" + kind,
    )(h, w, *extra)
    return res


def _proj_v_kernel(h_ref, w_ref, vf_ref, vt_ref, *, wn):
    z = _dot_nt(h_ref[...], w_ref[...].astype(BF16))
    _store_head_rows(vf_ref, z)
    zt = z.T
    ones = jnp.ones((BF16_ROWS, wn), BF16)
    for u in range(z.shape[0] // wn):
        blk = zt[:, u * wn:(u + 1) * wn].astype(BF16)
        for g in range(N_KV_HEADS):
            vt_ref[u, g * VT_ROWS:g * VT_ROWS + HEAD_DIM, :] = blk[g * HEAD_DIM:(g + 1) * HEAD_DIM]
            vt_ref[u, g * VT_ROWS + HEAD_DIM:(g + 1) * VT_ROWS, :] = ones


def _proj_v(h, wspec, tm, wn):
    w, c0, n = wspec
    m, k = h.shape
    assert c0 % SUBLANES == 0
    return pl.pallas_call(
        functools.partial(_proj_v_kernel, wn=wn),
        out_shape=(jax.ShapeDtypeStruct((m * N_KV_HEADS, HEAD_DIM), F32),
                   jax.ShapeDtypeStruct((m // wn, N_KV_HEADS * VT_ROWS, wn), BF16)),
        grid=(m // tm,),
        in_specs=[pl.BlockSpec((tm, k), lambda i: (i, 0)),
                  pl.BlockSpec((pl.Element(n), pl.Element(k)), lambda i: (c0, 0))],
        out_specs=(pl.BlockSpec((tm * N_KV_HEADS, HEAD_DIM), lambda i: (i, 0)),
                   pl.BlockSpec((tm // wn, N_KV_HEADS * VT_ROWS, wn), lambda i: (i, 0, 0))),
        compiler_params=_cparams(("arbitrary",)),
        name="proj_v",
    )(h, w)


def _dsa_kernel(*refs, tq, tk, t_new, l_cache, pos0, topk, has_cache):
    if has_cache:
        (qi_ref, wi_ref, q_ref, kin_ref, kn_ref, vtn_ref, cki_ref, ck_ref, cv_ref, o_ref,
         sc_ref, qit_ref, wt_ref, qt_ref, acc_ref, s_ref, p_ref, tau_ref, sct_ref) = refs
    else:
        (qi_ref, wi_ref, q_ref, kin_ref, kn_ref, vtn_ref, o_ref,
         sc_ref, qit_ref, wt_ref, qt_ref, acc_ref, s_ref, p_ref, tau_ref, sct_ref) = refs
        cki_ref = ck_ref = cv_ref = None
    i = pl.program_id(1)
    ntc = l_cache // tk
    wn = min(tk, t_new)
    nq = GROUP * tq
    q0 = pos0 + i * tq
    lane_q = lax.broadcasted_iota(jnp.int32, (1, tq), 1)
    qend = (((q0 + lane_q) >> 6) + 1) << 6
    last_end = (((q0 + tq - 1) >> 6) + 1) << 6
    nvis_new = jnp.minimum(last_end - pos0, t_new)
    ntn = (nvis_new + wn - 1) // wn
    ntiles = ntc + ntn

    def tpose(x):
        if tq == LANES:
            return x.T
        xf = x.astype(F32)
        if tq < LANES:
            xf = jnp.concatenate([xf, jnp.zeros((LANES - tq, LANES), F32)], axis=0)
        xt = xf.T
        return xt[:, :tq] if tq < LANES else xt

    qi_blk = qi_ref[0]
    for h in range(IDX_HEADS):
        qit_ref[:, h * tq:(h + 1) * tq] = tpose(qi_blk[:, h * IDX_DIM:(h + 1) * IDX_DIM]).astype(BF16)
    wt_ref[...] = tpose(wi_ref[0])
    q_blk = q_ref[0]
    for g in range(N_KV_HEADS):
        for a in range(GROUP):
            hh = GROUP * g + a
            qt_ref[g, :, a * tq:(a + 1) * tq] = tpose(q_blk[:, hh * HEAD_DIM:(hh + 1) * HEAD_DIM]).astype(BF16)

    def score_tile(ki_tile, kpos0, w):
        acc = jnp.zeros((w, tq), F32)
        hpd = 2 if 2 * tq % LANES == 0 else 1
        for hp in range(IDX_HEADS // hpd):
            lg = _dot(ki_tile, qit_ref[:, hpd * hp * tq:(hpd * hp + hpd) * tq])
            for e in range(hpd):
                h = hpd * hp + e
                acc = acc + wt_ref[h:h + 1, :] * jnp.maximum(lg[:, e * tq:(e + 1) * tq], 0.0)
        kpos = kpos0 + lax.broadcasted_iota(jnp.int32, (w, 1), 0)
        vis = kpos < qend
        s = jnp.where(vis, acc, NEG_INF)
        smin = _col_reduce(jnp.min, jnp.where(vis, acc, POS_INF))
        smax = _col_reduce(jnp.max, s)
        return s, smax, smin

    by_rows = tq < LANES and t_new <= tk

    def square_t(x, fill):
        r, c_ = x.shape
        if c_ < LANES:
            x = jnp.concatenate([x, jnp.full((r, LANES - c_), fill, F32)], axis=1)
        if r < LANES:
            x = jnp.concatenate([x, jnp.full((LANES - r, LANES), fill, F32)], axis=0)
        return x.T

    def stash_rows(t, s):
        st = square_t(s, NEG_INF)[:tq, :]
        if st.shape[1] < tk:
            sct_ref[t] = jnp.full((tq, tk), NEG_INF, F32)
            sct_ref[t, :, 0:st.shape[1]] = st
        else:
            sct_ref[t] = st

    def p1_cache(t, carry):
        mx, mn = carry
        r0 = pl.multiple_of(t * tk, tk)
        s, smax, smin = score_tile(cki_ref[0, pl.ds(r0, tk), :].astype(BF16), r0, tk)
        sc_ref[t] = s
        if by_rows:
            stash_rows(t, s)
        return jnp.maximum(mx, smax), jnp.minimum(mn, smin)

    def p1_new(j, carry):
        mx, mn = carry
        r0 = pl.multiple_of(j * wn, wn)
        s, smax, smin = score_tile(kin_ref[0, pl.ds(r0, wn), :], pos0 + r0, wn)
        if wn == tk:
            sc_ref[ntc + j] = s
        else:
            sc_ref[ntc + j] = jnp.full((tk, tq), NEG_INF, F32)
            sc_ref[ntc + j, 0:wn, :] = s
        if by_rows:
            stash_rows(ntc + j, s)
        return jnp.maximum(mx, smax), jnp.minimum(mn, smin)

    carry = (jnp.full((1, tq), NEG_INF, F32), jnp.full((1, tq), POS_INF, F32))
    if has_cache:
        carry = lax.fori_loop(0, ntc, p1_cache, carry)
    mx, mn = lax.fori_loop(0, ntn, p1_new, carry)

    nvis = jnp.clip(qend - pos0, 0, t_new)
    if has_cache:
        nvis = nvis + jnp.minimum(qend, l_cache)
    act0 = jnp.where(nvis > topk, 1.0, 0.0)
    kf = float(topk)
    cu = min(tk, COUNT_UNIT)
    upt = tk // cu
    n_units = ntc * upt + (nvis_new + cu - 1) // cu

    def bis_cond(carry):
        return jnp.logical_and(carry[0] < 400, carry[1] > 0)

    def unit(u):
        return sc_ref.at[u // upt, pl.ds(pl.multiple_of((u % upt) * cu, cu), cu), :]

    def count_where(pred):
        def body(u, cnt):
            ind = jnp.where(pred(unit(u)[...]), 1.0, 0.0)
            return cnt + jnp.sum(ind.reshape(cu // CNT_ROWS, CNT_ROWS, tq), axis=0)

        cnt = lax.fori_loop(0, n_units, body, jnp.zeros((CNT_ROWS, tq), F32))
        return jnp.sum(cnt, axis=0, keepdims=True)

    def count_ge_rows(mid):
        ind = jnp.where(sct_ref[...] >= mid[None], 1.0, 0.0)
        return jnp.sum(jnp.sum(ind, axis=0), axis=1, keepdims=True)

    def to_col(r):
        return square_t(jnp.broadcast_to(r, (LANES, tq)), 0.0)[:tq, 0:1]

    def to_row(col):
        return square_t(jnp.broadcast_to(col, (tq, LANES)), 0.0)[0:1, :tq]

    count_ge = count_ge_rows if by_rows else (lambda mid: count_where(lambda s: s >= mid))

    def bis_body(carry):
        it, _, lo, hi, tau, act, tied = carry
        mid = lo * 0.5 + hi * 0.5
        c = count_ge(mid)
        found = jnp.where(c == kf, act, 0.0)
        stuck = jnp.where(jnp.logical_or(mid <= lo, mid >= hi), act, 0.0) * (1.0 - found)
        above = jnp.where(c > kf, act, 0.0) * (1.0 - stuck)
        below = act * (1.0 - above) * (1.0 - found) * (1.0 - stuck)
        tau = jnp.where(found > 0, mid, tau)
        tau = jnp.where(stuck > 0, lo, tau)
        lo = jnp.where(above > 0, mid, lo)
        hi = jnp.where(below > 0, mid, hi)
        nact = act * (1.0 - found) * (1.0 - stuck)
        return it + 1, (jnp.max(nact) > 0).astype(jnp.int32), lo, hi, tau, nact, jnp.maximum(tied, stuck)

    flag0 = (jnp.max(act0) > 0).astype(jnp.int32)
    if by_rows:
        mn_c = to_col(mn)
        init = (jnp.int32(0), flag0, mn_c, to_col(mx), mn_c, to_col(act0), jnp.zeros((tq, 1), F32))
    else:
        init = (jnp.int32(0), flag0, mn, mx, mn, act0, jnp.zeros((1, tq), F32))
    res = lax.while_loop(bis_cond, lambda carry: bis_body(bis_body(carry)), init)
    tau_ref[...] = to_row(res[4]) if by_rows else res[4]
    tied = to_row(res[6]) if by_rows else res[6]

    @pl.when(jnp.max(tied) > 0)
    def _():
        lo_t = tau_ref[...]

        def tmin_body(u, acc):
            s = unit(u)[...]
            return jnp.minimum(acc, _col_reduce(jnp.min, jnp.where(s >= lo_t, s, POS_INF)))

        tstar = lax.fori_loop(0, n_units, tmin_body, jnp.full((1, tq), POS_INF, F32))
        tstar = jnp.where(tied > 0, tstar, lo_t)
        need = kf - count_where(lambda s: s > tstar)
        ri = lax.broadcasted_iota(jnp.int32, (cu, cu), 0)
        ci = lax.broadcasted_iota(jnp.int32, (cu, cu), 1)
        tri = jnp.where(ri >= ci, 1.0, 0.0).astype(BF16)

        def strike_body(u, run):
            s = unit(u)[...]
            eq = jnp.where(jnp.logical_and(s == tstar, tied > 0), 1.0, 0.0)
            rank = run + _dot(tri, eq.astype(BF16))
            unit(u)[...] = jnp.where(jnp.logical_and(eq > 0, rank > need), NEG_INF, s)
            return rank[cu - 1:cu, :]

        lax.fori_loop(0, n_units, strike_body, jnp.zeros((1, tq), F32))
        tau_ref[...] = tstar

    tau = tau_ref[...]

    acc_ref[...] = jnp.zeros(acc_ref.shape, F32)

    def attend(t, k_of_g, vt_of_g, w, ms):
        s_idx = sc_ref[t]
        if w < tk:
            s_idx = s_idx[0:w]
        bias = jnp.where(s_idx >= tau, 0.0, NEG_INF)
        bias4 = jnp.concatenate([bias] * GROUP, axis=1)
        def qk(g):
            s = _dot(k_of_g(g), qt_ref[g]) + bias4
            s_ref[g % STAGE_SLOTS, 0:w] = s
            return _col_reduce(jnp.max, s)

        out = []
        maxes = [qk(g) for g in range(STAGE_SLOTS - 1)]
        for g in range(N_KV_HEADS):
            slot = g % STAGE_SLOTS
            if g + STAGE_SLOTS - 1 < N_KV_HEADS:
                maxes.append(qk(g + STAGE_SLOTS - 1))
            m_old = ms[g]
            m_new = jnp.maximum(m_old, maxes[g])
            m_safe = jnp.where(m_new == NEG_INF, 0.0, m_new)
            alpha = jnp.exp2(m_old - m_safe)
            p_ref[slot, 0:w] = jnp.exp2(s_ref[slot, 0:w] - m_safe).astype(BF16)
            acc_ref[g] = acc_ref[g] * alpha + _dot(vt_of_g(g), p_ref[slot, 0:w])
            out.append(m_new)
        return tuple(out)

    def p3_cache(t, ms):
        r0 = pl.multiple_of(t * tk, tk)

        def vt(g):
            vg = cv_ref[0, pl.ds(r0 * N_KV_HEADS + g, tk, stride=N_KV_HEADS), :]
            return jnp.concatenate([vg.T, jnp.ones((BF16_ROWS, tk), F32)], axis=0).astype(BF16)

        return attend(
            t, lambda g: ck_ref[0, pl.ds(r0 * N_KV_HEADS + g, tk, stride=N_KV_HEADS), :].astype(BF16), vt, tk, ms)

    def p3_new(j, ms):
        r0 = pl.multiple_of(j * wn, wn)
        return attend(ntc + j,
                      lambda g: kn_ref[0, pl.ds(r0, wn), g * HEAD_DIM:(g + 1) * HEAD_DIM],
                      lambda g: vtn_ref[0, j, g * VT_ROWS:(g + 1) * VT_ROWS, :], wn, ms)

    ms = tuple(jnp.full((1, nq), NEG_INF, F32) for _ in range(N_KV_HEADS))
    if has_cache:
        ms = lax.fori_loop(0, ntc, p3_cache, ms)
    lax.fori_loop(0, ntn, p3_new, ms)

    for g in range(N_KV_HEADS):
        acc = acc_ref[g]
        og = (acc[0:HEAD_DIM] / acc[HEAD_DIM:HEAD_DIM + 1]).T
        for a in range(GROUP):
            hh = GROUP * g + a
            o_ref[0, :, hh * HEAD_DIM:(hh + 1) * HEAD_DIM] = og[a * tq:(a + 1) * tq, :].astype(o_ref.dtype)


def _dsa(qi, wi, q, kin, kn, vtn, cache, pos0, tq, tk):
    b, t, _ = q.shape
    has_cache = cache is not None
    l_cache = cache[0].shape[1] if has_cache else 0
    topk = min(TOPK_MAX, (l_cache + t) // 4)
    wn = min(tk, t)
    ntn = t // wn
    nt = l_cache // tk + ntn
    kvd = N_KV_HEADS * HEAD_DIM
    nq = GROUP * tq
    res = pl.Buffered(1) if b == 1 else None
    in_specs = [pl.BlockSpec((1, tq, IDX_HEADS * IDX_DIM), lambda bb, i: (bb, i, 0)),
                pl.BlockSpec((1, tq, LANES), lambda bb, i: (bb, i, 0)),
                pl.BlockSpec((1, tq, N_HEADS * HEAD_DIM), lambda bb, i: (bb, i, 0)),
                pl.BlockSpec((1, t, IDX_DIM), lambda bb, i: (bb, 0, 0), pipeline_mode=res),
                pl.BlockSpec((1, t, kvd), lambda bb, i: (bb, 0, 0), pipeline_mode=res),
                pl.BlockSpec((1, ntn, N_KV_HEADS * VT_ROWS, wn), lambda bb, i: (bb, 0, 0, 0), pipeline_mode=res)]
    args = [qi, wi, q, kin, kn, vtn]
    if has_cache:
        in_specs += [pl.BlockSpec((1, l_cache, IDX_DIM), lambda bb, i: (bb, 0, 0)),
                     pl.BlockSpec((1, l_cache * N_KV_HEADS, HEAD_DIM), lambda bb, i: (bb, 0, 0)),
                     pl.BlockSpec((1, l_cache * N_KV_HEADS, HEAD_DIM), lambda bb, i: (bb, 0, 0))]
        args += list(cache)
    kern = functools.partial(_dsa_kernel, tq=tq, tk=tk, t_new=t, l_cache=l_cache, pos0=pos0, topk=topk,
                             has_cache=has_cache)
    return pl.pallas_call(
        kern,
        out_shape=jax.ShapeDtypeStruct((b, t, N_HEADS * HEAD_DIM), BF16),
        grid=(b, t // tq),
        in_specs=in_specs,
        out_specs=pl.BlockSpec((1, tq, N_HEADS * HEAD_DIM), lambda bb, i: (bb, i, 0)),
        scratch_shapes=[pltpu.VMEM((nt, tk, tq), F32),
                        pltpu.VMEM((IDX_DIM, IDX_HEADS * tq), BF16),
                        pltpu.VMEM((LANES, tq), F32),
                        pltpu.VMEM((N_KV_HEADS, HEAD_DIM, nq), BF16),
                        pltpu.VMEM((N_KV_HEADS, VT_ROWS, nq), F32),
                        pltpu.VMEM((STAGE_SLOTS, tk, nq), F32),
                        pltpu.VMEM((STAGE_SLOTS, tk, nq), BF16),
                        pltpu.VMEM((1, tq), F32),
                        pltpu.VMEM((nt if (tq < LANES and t <= tk) else 1, tq, tk), F32)],
        compiler_params=_cparams(("arbitrary", "arbitrary")),
        name="dsa",
    )(*args)


def _hgrn_kernel(*refs, c, nchunk, nh, has_state):
    if has_state:
        q_ref, lf_ref, kk_ref, v_ref, hg_ref, nw_ref, s0_ref, o_ref, sout_ref, st_ref, g_ref = refs
    else:
        q_ref, lf_ref, kk_ref, v_ref, hg_ref, nw_ref, o_ref, sout_ref, st_ref, g_ref = refs
        s0_ref = None
    r = pl.program_id(2)

    @pl.when(r == 0)
    def _():
        for hh in range(nh):
            if has_state:
                st_ref[hh] = s0_ref[0, hh].T
            else:
                st_ref[hh] = jnp.zeros((HG_DV, HG_DK), F32)

    ri = lax.broadcasted_iota(jnp.int32, (c, c), 0)
    ci = lax.broadcasted_iota(jnp.int32, (c, c), 1)
    tri = jnp.where(ri >= ci, 1.0, 0.0).astype(BF16)
    nsub = c // SUB_BLOCK
    pair = lax.broadcasted_iota(jnp.int32, (SUB_BLOCK * SUB_BLOCK, HG_DK), 0)
    cap = jnp.where((pair % SUB_BLOCK) >= (pair // SUB_BLOCK), 0.0, NEG_INF)
    ones_w = jnp.ones((HG_DK, LANES), BF16)

    def rep_rows(row):
        return jnp.concatenate(
            [jnp.broadcast_to(row(sg), (SUB_BLOCK, HG_DK)) for sg in range(SUB_BLOCK)], axis=0)

    def cumulative(base):
        lf = lf_ref[pl.ds(base, c), :]
        l1 = lf.astype(BF16)
        r1 = lf - l1.astype(F32)
        l2 = r1.astype(BF16)
        l3 = (r1 - l2.astype(F32)).astype(BF16)
        return _dot(tri, l1) + _dot(tri, l2) + _dot(tri, l3)

    def stage_a(base, hh, g_all):
        cs = slice(hh * HG_DK, (hh + 1) * HG_DK)
        G = g_all[:, cs]
        kk = kk_ref[pl.ds(base, c), cs]
        v = v_ref[pl.ds(base, c), cs].astype(F32)
        g_ref[0, hh] = G
        g_ref[1, hh] = kk
        g_ref[2, hh] = v
        return dict(cs=cs, G=G, hh=hh, q=q_ref[pl.ds(base, c), cs].astype(F32), kk=kk, v=v)

    def stage_b(hh, d):
        q, kk, v, G = d["q"], d["kk"], d["v"], d["G"]
        st = st_ref[hh]
        vb = v.astype(BF16)
        d["vb"] = vb
        d["o_inter"] = _dot_nt((q * jnp.exp2(G)).astype(BF16), st.astype(BF16))
        d["A"], diag = [], []
        for i in range(nsub):
            lo_, hi_ = i * SUB_BLOCK, (i + 1) * SUB_BLOCK
            qi_ = q[lo_:hi_]
            Gi = G[lo_:hi_]
            if i > 0:
                Gb = G[lo_ - 1:lo_]
                qt = qi_ * jnp.exp2(Gi - Gb)
                kt = kk[:lo_] * jnp.exp2(Gb - G[:lo_])
                a_blk = _dot_nt(qt.astype(BF16), kt.astype(BF16))
                d["A"].append(jnp.concatenate([a_blk, jnp.zeros((SUB_BLOCK, c - lo_), F32)], axis=1))
            qrep = jnp.concatenate([qi_] * SUB_BLOCK, axis=0)
            grep = jnp.concatenate([Gi] * SUB_BLOCK, axis=0)
            kk_rep = rep_rows(lambda sg: g_ref[1, hh, lo_ + sg:lo_ + sg + 1, :])
            g_rep = rep_rows(lambda sg: g_ref[0, hh, lo_ + sg:lo_ + sg + 1, :])
            diag.append((qrep * kk_rep * jnp.exp2(jnp.minimum(grep - g_rep, cap))).astype(BF16))
        rs = _dot(jnp.concatenate(diag, axis=0) if nsub > 1 else diag[0], ones_w)
        pairs = SUB_BLOCK * SUB_BLOCK
        d["rs"] = [rs[i * pairs:(i + 1) * pairs] for i in range(nsub)]
        Gl = G[c - 1:c]
        kdec = kk * jnp.exp2(Gl - G)
        st_ref[hh] = st * jnp.exp2(Gl) + _dot_tn(vb, kdec.astype(BF16))

    def stage_c(base, d):
        parts = []
        o_base = d["o_inter"]
        if nsub > 1:
            a_low = jnp.concatenate([jnp.zeros((SUB_BLOCK, c), F32)] + d["A"], axis=0)
            o_base = o_base + _dot(a_low.astype(BF16), d["vb"])
        for i in range(nsub):
            lo_, hi_ = i * SUB_BLOCK, (i + 1) * SUB_BLOCK
            oi = o_base[lo_:hi_]
            v_rep = rep_rows(lambda sg: g_ref[2, d["hh"], lo_ + sg:lo_ + sg + 1, :])
            contrib = (d["rs"][i] * v_rep).reshape(SUB_BLOCK, SUB_BLOCK, HG_DV)
            parts.append(oi + jnp.sum(contrib, axis=0))
        o = jnp.concatenate(parts, axis=0) if nsub > 1 else parts[0]
        on = o * lax.rsqrt(jnp.mean(o * o, axis=1, keepdims=True) + EPS) * nw_ref[...]
        hg = hg_ref[pl.ds(base, c), d["cs"]].astype(F32)
        o_ref[pl.ds(base, c), d["cs"]] = (on * (hg * _sigmoid(hg))).astype(o_ref.dtype)

    def chunk(n, carry):
        base = pl.multiple_of(n * c, c)
        g_all = cumulative(base)
        heads = [stage_a(base, hh, g_all) for hh in range(nh)]
        for hh, d in enumerate(heads):
            stage_b(hh, d)
        for d in heads:
            stage_c(base, d)
        return carry

    lax.fori_loop(0, nchunk, chunk, 0)

    @pl.when(r == pl.num_programs(2) - 1)
    def _():
        for hh in range(nh):
            sout_ref[0, hh] = st_ref[hh].T


def _hgrn(hq, lf, kk, hv, hg, nw, s0, b, t, rb, nh):
    c = min(CHUNK, t)
    nchunk = rb // c
    nr = t // rb
    has_state = s0 is not None
    blk = pl.BlockSpec((rb, nh * HG_DK), lambda bb, h, r: (bb * nr + r, h))
    sblk = pl.BlockSpec((1, nh, HG_DK, HG_DV), lambda bb, h, r: (bb, h, 0, 0))
    in_specs = [blk, blk, blk, blk, blk, pl.BlockSpec((1, HG_DV), lambda bb, h, r: (0, 0))]
    args = [hq, lf, kk, hv, hg, nw]
    if has_state:
        in_specs.append(sblk)
        args.append(s0)
    kern = functools.partial(_hgrn_kernel, c=c, nchunk=nchunk, nh=nh, has_state=has_state)
    return pl.pallas_call(
        kern,
        out_shape=(jax.ShapeDtypeStruct((b * t, HG_HEADS * HG_DV), BF16),
                   jax.ShapeDtypeStruct((b, HG_HEADS, HG_DK, HG_DV), F32)),
        grid=(b, HG_HEADS // nh, nr),
        in_specs=in_specs,
        out_specs=(blk, sblk),
        scratch_shapes=[pltpu.VMEM((nh, HG_DV, HG_DK), F32), pltpu.VMEM((3, nh, c, HG_DK), F32)],
        compiler_params=_cparams(("arbitrary", "arbitrary", "arbitrary")),
        name="hgrn",
    )(*args)


def _merge_out_kernel(x_ref, oa_ref, oh_ref, ga_ref, gb_ref, w_ref, g1_ref, sc_ref, sh_ref, nw_ref,
                      x1_ref, h2_ref):
    bb, tb, d = x_ref.shape
    merged = (_sigmoid(ga_ref[...].astype(F32)) * oa_ref[...].astype(F32)
              + _sigmoid(gb_ref[...].astype(F32)) * oh_ref[...].astype(F32))
    y = _dot(merged.astype(BF16), w_ref[...]).reshape(bb, tb, d)
    x1 = x_ref[...] + g1_ref[...] * y
    x1_ref[...] = x1
    ms = jnp.mean(x1 * x1, axis=-1, keepdims=True)
    xn = x1 * lax.rsqrt(ms + EPS) * nw_ref[...]
    h2 = xn * (1.0 + sc_ref[...]) + sh_ref[...]
    h2_ref[...] = h2.reshape(bb * tb, d).astype(BF16)


def _merge_out(x, oa, oh, ga, gb, w_out, g1, sc2, sh2, nw2, bb, tb):
    b, t, d = x.shape
    tm = bb * tb
    nt = t // tb

    def row(i, j):
        return (i * nt + j, 0)

    def mod(i, j):
        return (i, 0, 0)

    x3 = pl.BlockSpec((bb, tb, d), lambda i, j: (i, j, 0))
    r2 = pl.BlockSpec((tm, d), row)
    return pl.pallas_call(
        _merge_out_kernel,
        out_shape=(jax.ShapeDtypeStruct((b, t, d), F32), jax.ShapeDtypeStruct((b * t, d), BF16)),
        grid=(b // bb, nt),
        in_specs=[x3, r2, r2, r2, r2,
                  pl.BlockSpec((d, d), lambda i, j: (0, 0)),
                  pl.BlockSpec((bb, 1, d), mod), pl.BlockSpec((bb, 1, d), mod), pl.BlockSpec((bb, 1, d), mod),
                  pl.BlockSpec((1, 1, d), lambda i, j: (0, 0, 0))],
        out_specs=(x3, r2),
        compiler_params=_cparams(("arbitrary", "arbitrary")),
        name="merge_out",
    )(x, oa, oh, ga, gb, w_out, g1, sc2, sh2, nw2)


def _mlp_kernel(h_ref, wu_ref, wd_ref, x1_ref, g2_ref, fw_ref, y_ref, acc_ref):
    f = pl.program_id(2)

    @pl.when(f == 0)
    def _():
        acc_ref[...] = jnp.zeros(acc_ref.shape, F32)

    u = jnp.maximum(_dot(h_ref[...], wu_ref[...]), 0.0)
    acc_ref[...] += _dot((u * u).astype(BF16), wd_ref[...])

    @pl.when(f == pl.num_programs(2) - 1)
    def _():
        bb, tb, d = x1_ref.shape
        x2 = x1_ref[...] + g2_ref[...] * acc_ref[...].reshape(bb, tb, d)
        ms = jnp.mean(x2 * x2, axis=-1, keepdims=True)
        y_ref[...] = x2 * lax.rsqrt(ms + EPS) * fw_ref[...]


def _mlp(h2, w_up, w_down, x1, g2, fw, bb, tb, tf):
    b, t, d = x1.shape
    dff = w_up.shape[1]
    tm = bb * tb
    nt = t // tb
    x3 = pl.BlockSpec((bb, tb, d), lambda i, j, f: (i, j, 0))
    return pl.pallas_call(
        _mlp_kernel,
        out_shape=jax.ShapeDtypeStruct((b, t, d), F32),
        grid=(b // bb, nt, dff // tf),
        in_specs=[pl.BlockSpec((tm, d), lambda i, j, f: (i * nt + j, 0)),
                  pl.BlockSpec((d, tf), lambda i, j, f: (0, f)),
                  pl.BlockSpec((tf, d), lambda i, j, f: (f, 0)),
                  x3,
                  pl.BlockSpec((bb, 1, d), lambda i, j, f: (i, 0, 0)),
                  pl.BlockSpec((1, 1, d), lambda i, j, f: (0, 0, 0))],
        out_specs=x3,
        scratch_shapes=[pltpu.VMEM((tm, d), F32)],
        compiler_params=_cparams(("arbitrary", "arbitrary", "arbitrary")),
        name="mlp",
    )(h2, w_up, w_down, x1, g2, fw)


def _rope_tables(pos):
    half = ROT_DIM // 2
    inv_freq = ROPE_THETA ** (-(jnp.arange(half, dtype=F32) * (2.0 / ROT_DIM)))
    ang = pos.astype(F32)[:, None] * inv_freq[None, :]
    cos, sin = jnp.cos(ang), jnp.sin(ang)
    n = pos.shape[0]
    ones = jnp.ones((n, LANES - ROT_DIM), F32)
    zeros = jnp.zeros((n, LANES - ROT_DIM), F32)
    zh = jnp.zeros((n, half), F32)
    c_t = jnp.concatenate([cos, cos, ones], axis=1)
    s_up = jnp.concatenate([-sin, zh, zeros], axis=1)
    s_dn = jnp.concatenate([zh, sin, zeros], axis=1)
    return c_t, s_up, s_dn


def _trunk(x, mod, pos0, past, wts, blocks):
    (norm1_w, w_parts, lb_logits, hg_norm_w, w_out, norm2_w, w_up, w_down, final_w) = wts
    b, t, d = x.shape
    bb, tb, tm_proj, tq, tk, rb, mlp_bb, mlp_tb = blocks
    m = [mod[:, i:i + 1, :] for i in range(6)]
    sh1, sc1, g1, sh2, sc2, g2 = m
    h = _normmod(x, sc1, sh1, norm1_w.reshape(1, 1, d), bb, tb).reshape(b * t, d)

    pos = pos0 + jnp.arange(t, dtype=jnp.int32)
    tabs = tuple(jnp.tile(tb_, (b, 1)) for tb_ in _rope_tables(pos))
    wq, wk, wv, wqi, wki, wwi, whq, whf, whi, whg, wga, wgb = w_parts
    tm = min(tm_proj, b * t)
    wn = min(tk, t)
    (q_bf,) = _proj("rope", h, wq, tm, PROJ_TN, tabs, (BF16,), scale=HEAD_DIM ** -0.5 * LOG2E)
    k_f, k_bf = _proj("rope", h, wk, tm, PROJ_TN, tabs, (F32, BF16), head_rows=(0,))
    v_f, vt_bf = _proj_v(h, wv, tm, wn)
    (qi_bf,) = _proj("rope", h, wqi, tm, PROJ_TN, tabs, (BF16,))
    ki_f, ki_bf = _proj("rope", h, wki, tm, LANES, tabs, (F32, BF16))
    (wi_f,) = _proj("plain", h, wwi, tm, LANES, (), (F32,), scale=IDX_HEADS ** -0.5 * IDX_DIM ** -0.5)
    (hq,) = _proj("plain", h, whq, tm, PROJ_TN, (), (BF16,))
    lf, kk = _proj("forget", h, whf, tm, PROJ_TN, (lb_logits,), (F32, F32))
    (hi,) = _proj("plain", h, whi, tm, PROJ_TN, (), (BF16,))
    (hg,) = _proj("plain", h, whg, tm, PROJ_TN, (), (BF16,))
    (ga,) = _proj("plain", h, wga, tm, PROJ_TN, (), (BF16,))
    (gb,) = _proj("plain", h, wgb, tm, PROJ_TN, (), (BF16,))

    kvd = N_KV_HEADS * HEAD_DIM
    r3 = lambda a: a.reshape(b, t, a.shape[-1])
    if past is None:
        cache, s0 = None, None
    else:
        ck, cv, cki, s0 = past
        lc = ck.shape[1]
        cache = (cki, ck.reshape(b, lc * N_KV_HEADS, HEAD_DIM), cv.reshape(b, lc * N_KV_HEADS, HEAD_DIM))
    vtn = vt_bf.reshape(b, t // wn, N_KV_HEADS * VT_ROWS, wn)
    o_attn = _dsa(r3(qi_bf), r3(wi_f), r3(q_bf), r3(ki_bf), r3(k_bf), vtn, cache, pos0, tq, tk)
    o_hg, s_new = _hgrn(hq, lf, kk, hi, hg, hg_norm_w.reshape(1, HG_DV), s0, b, t, rb, 16)

    x1, h2 = _merge_out(x, o_attn.reshape(b * t, d), o_hg, ga, gb, w_out, g1, sc2, sh2,
                        norm2_w.reshape(1, 1, d), bb, tb)
    y = _mlp(h2, w_up, w_down, x1, g2, final_w.reshape(1, 1, d), mlp_bb, mlp_tb, 1024)
    return (y, k_f.reshape(b, t, N_KV_HEADS, HEAD_DIM), v_f.reshape(b, t, N_KV_HEADS, HEAD_DIM),
            ki_f.reshape(b, t, IDX_DIM), s_new)


def kernel(x_prompt, x_sample, cache_k, cache_v, cache_ki, state_hgrn, c_prompt, c_sample, w_ada, b_ada, norm1_w,
           w_in, hg_lb_logits, hg_norm_w, w_out, norm2_w, w_up, w_down, final_norm_w):
    depth = w_in.shape[0]
    assert depth == 1
    d = x_prompt.shape[-1]
    bp, tp, _ = x_prompt.shape
    bs, ts, _ = x_sample.shape
    past_len = cache_k.shape[2]

    c_all = jnp.concatenate([c_prompt, c_sample], axis=0)
    nrow = c_all.shape[0]
    pad = (-nrow) % SUBLANES
    c_all = jnp.pad(c_all, ((0, pad), (0, 0)))
    mod = _ada(c_all, w_ada[0], b_ada[0].reshape(1, -1)).reshape(nrow + pad, 6, d)

    sizes = (N_HEADS * HEAD_DIM, N_KV_HEADS * HEAD_DIM, N_KV_HEADS * HEAD_DIM, IDX_HEADS * IDX_DIM, IDX_DIM,
             IDX_HEADS, HG_HEADS * HG_DK, HG_HEADS * HG_DK, HG_HEADS * HG_DV, HG_HEADS * HG_DV, d, d)
    offs = np.concatenate([[0], np.cumsum(sizes)])
    w_t = jnp.transpose(w_in[0])
    w_parts = [(w_t, int(offs[i]), sz + (-sz) % LANES) for i, sz in enumerate(sizes)]
    wts = (norm1_w[0], tuple(w_parts), hg_lb_logits.astype(F32), hg_norm_w[0], w_out[0].astype(BF16), norm2_w[0],
           w_up[0].astype(BF16), w_down[0].astype(BF16), final_norm_w)

    yp, kp, vp, kip, sp = _trunk(x_prompt, mod[:bp], 0, None, wts, (1, 256, 2048, 128, 1024, 512, 1, 512))
    past = (cache_k[0], cache_v[0], cache_ki[0], state_hgrn[0])
    ys, ks, vs, kis, ss = _trunk(x_sample, mod[bp:bp + bs], past_len, past, wts,
                                 (bs // 2, ts, bs * ts, ts, 512, ts, bs, ts))
    return (yp, ys, kp[None], vp[None], kip[None], sp[None], ks[None], vs[None], kis[None], ss[None])
```

```python
import functools
import math

import jax
import jax.numpy as jnp
import numpy as np
from jax import lax
from jax.experimental import pallas as pl
from jax.experimental.pallas import tpu as pltpu

CHUNK = 64
N_HEADS = 16
HEAD_DIM = 128
N_KV_HEADS = 4
GROUP = N_HEADS // N_KV_HEADS
ROT_DIM = HEAD_DIM // 4
ROPE_THETA = 500000.0
IDX_HEADS = 16
IDX_DIM = 128
TOPK_MAX = 256
HG_HEADS = 16
HG_DK = 128
HG_DV = 128
EPS = 1e-6
LANES = 128
SUBLANES = 8
BF16_ROWS = 16
SUB_BLOCK = 8
PROJ_TN = 512
COUNT_UNIT = 512
STAGE_SLOTS = 3
CNT_ROWS = 64
VT_ROWS = HEAD_DIM + BF16_ROWS
VMEM_LIMIT = 56 * 1024 * 1024
NEG_INF = float("-inf")
POS_INF = float("inf")
LOG2E = math.log2(math.e)

F32 = jnp.float32
BF16 = jnp.bfloat16


def _cparams(sem):
    return pltpu.CompilerParams(dimension_semantics=sem, vmem_limit_bytes=VMEM_LIMIT)


def _dot_nt(a, b):
    return lax.dot_general(a, b, (((1,), (1,)), ((), ())), preferred_element_type=F32)


def _dot_tn(a, b):
    return lax.dot_general(a, b, (((0,), (0,)), ((), ())), preferred_element_type=F32)


def _dot(a, b):
    return jnp.dot(a, b, preferred_element_type=F32)


def _sigmoid(x):
    return 1.0 / (1.0 + jnp.exp(-x))


def _col_reduce(op, x):
    rows, n = x.shape
    if rows > CNT_ROWS and rows % CNT_ROWS == 0:
        x = op(x.reshape(rows // CNT_ROWS, CNT_ROWS, n), axis=0)
    return op(x, axis=0, keepdims=True)


def _ada_kernel(c_ref, w_ref, b_ref, o_ref):
    o_ref[...] = _dot(c_ref[...].astype(BF16), w_ref[...].astype(BF16)) + b_ref[...]


def _ada(c_all, w_ada, b_ada, tn=1024):
    r, d = c_all.shape
    n = w_ada.shape[1]
    return pl.pallas_call(
        _ada_kernel,
        out_shape=jax.ShapeDtypeStruct((r, n), F32),
        grid=(n // tn,),
        in_specs=[pl.BlockSpec((r, d), lambda j: (0, 0)),
                  pl.BlockSpec((d, tn), lambda j: (0, j)),
                  pl.BlockSpec((1, tn), lambda j: (0, j))],
        out_specs=pl.BlockSpec((r, tn), lambda j: (0, j)),
        compiler_params=_cparams(("arbitrary",)),
        name="ada",
    )(c_all, w_ada, b_ada)


def _normmod_kernel(x_ref, sc_ref, sh_ref, g_ref, o_ref):
    x = x_ref[...]
    ms = jnp.mean(x * x, axis=-1, keepdims=True)
    xn = x * lax.rsqrt(ms + EPS) * g_ref[...]
    o_ref[...] = (xn * (1.0 + sc_ref[...]) + sh_ref[...]).astype(o_ref.dtype)


def _normmod(x, sc, sh, g, bb, tb):
    b, t, d = x.shape
    return pl.pallas_call(
        _normmod_kernel,
        out_shape=jax.ShapeDtypeStruct((b, t, d), BF16),
        grid=(b // bb, t // tb),
        in_specs=[pl.BlockSpec((bb, tb, d), lambda i, j: (i, j, 0)),
                  pl.BlockSpec((bb, 1, d), lambda i, j: (i, 0, 0)),
                  pl.BlockSpec((bb, 1, d), lambda i, j: (i, 0, 0)),
                  pl.BlockSpec((1, 1, d), lambda i, j: (0, 0, 0))],
        out_specs=pl.BlockSpec((bb, tb, d), lambda i, j: (i, j, 0)),
        compiler_params=_cparams(("arbitrary", "arbitrary")),
        name="normmod",
    )(x, sc, sh, g)


def _proj_plain_kernel(h_ref, w_ref, *o_refs, scale):
    z = _dot_nt(h_ref[...], w_ref[...].astype(BF16))
    if scale != 1.0:
        z = z * scale
    for o_ref in o_refs:
        o_ref[...] = z.astype(o_ref.dtype)


def _store_head_rows(o_ref, r):
    heads = r.shape[1] // LANES
    for g in range(heads):
        o_ref[pl.ds(g, r.shape[0], stride=heads), :] = r[:, g * LANES:(g + 1) * LANES].astype(o_ref.dtype)


def _proj_rope_kernel(h_ref, w_ref, cos_ref, sup_ref, sdn_ref, *o_refs, scale):
    z = _dot_nt(h_ref[...], w_ref[...].astype(BF16))
    tn = z.shape[1]
    reps = tn // LANES

    def wide(ref):
        t = ref[...]
        return t if reps == 1 else jnp.concatenate([t] * reps, axis=1)

    up = pltpu.roll(z, tn - ROT_DIM // 2, 1)
    dn = pltpu.roll(z, ROT_DIM // 2, 1)
    r = z * wide(cos_ref) + up * wide(sup_ref) + dn * wide(sdn_ref)
    for o_ref in o_refs:
        if o_ref.shape[0] != r.shape[0]:
            _store_head_rows(o_ref, r)
        elif o_ref.dtype == BF16 and scale != 1.0:
            o_ref[...] = (r * scale).astype(BF16)
        else:
            o_ref[...] = r.astype(o_ref.dtype)


def _proj_forget_kernel(h_ref, w_ref, lbl_ref, lf_ref, kk_ref):
    z = _dot_nt(h_ref[...], w_ref[...].astype(BF16))
    lbl = lbl_ref[...]
    mx = jnp.max(lbl, axis=0, keepdims=True)
    e = jnp.exp(lbl - mx)
    lb = e[0:1, :] / jnp.sum(e, axis=0, keepdims=True)
    f = lb + (1.0 - lb) * _sigmoid(z)
    lf_ref[...] = jnp.log(f) * LOG2E
    kk_ref[...] = 1.0 - f


def _proj(kind, h, wspec, tm, tn, extra=(), out_dtypes=(F32,), scale=1.0, head_rows=()):
    w, c0, n = wspec
    m, k = h.shape
    assert c0 % SUBLANES == 0 and n % tn == 0
    in_specs = [pl.BlockSpec((tm, k), lambda i, j: (i, 0)),
                pl.BlockSpec((pl.Element(tn), pl.Element(k)),
                             lambda i, j: (pl.multiple_of(c0 + j * tn, SUBLANES), 0))]
    if kind == "rope":
        kern = functools.partial(_proj_rope_kernel, scale=scale)
        in_specs += [pl.BlockSpec((tm, LANES), lambda i, j: (i, 0))] * 3
    elif kind == "forget":
        kern = _proj_forget_kernel
        in_specs += [pl.BlockSpec((extra[0].shape[0], tn), lambda i, j: (0, j))]
    else:
        kern = functools.partial(_proj_plain_kernel, scale=scale)
    hpt = tn // LANES
    outs, out_specs = [], []
    for o, dt in enumerate(out_dtypes):
        if o in head_rows:
            assert n == tn
            outs.append(jax.ShapeDtypeStruct((m * hpt, LANES), dt))
            out_specs.append(pl.BlockSpec((tm * hpt, LANES), lambda i, j: (i, 0)))
        else:
            outs.append(jax.ShapeDtypeStruct((m, n), dt))
            out_specs.append(pl.BlockSpec((tm, tn), lambda i, j: (i, j)))
    res = pl.pallas_call(
        kern,
        out_shape=tuple(outs),
        grid=(m // tm, n // tn),
        in_specs=in_specs,
        out_specs=tuple(out_specs),
        compiler_params=_cparams(("arbitrary", "arbitrary")),
        name="proj_" + kind,
    )(h, w, *extra)
    return res


def _proj_v_kernel(h_ref, w_ref, vf_ref, vt_ref, *, wn):
    z = _dot_nt(h_ref[...], w_ref[...].astype(BF16))
    _store_head_rows(vf_ref, z)
    zt = z.T
    ones = jnp.ones((BF16_ROWS, wn), BF16)
    for u in range(z.shape[0] // wn):
        blk = zt[:, u * wn:(u + 1) * wn].astype(BF16)
        for g in range(N_KV_HEADS):
            vt_ref[u, g * VT_ROWS:g * VT_ROWS + HEAD_DIM, :] = blk[g * HEAD_DIM:(g + 1) * HEAD_DIM]
            vt_ref[u, g * VT_ROWS + HEAD_DIM:(g + 1) * VT_ROWS, :] = ones


def _proj_v(h, wspec, tm, wn):
    w, c0, n = wspec
    m, k = h.shape
    assert c0 % SUBLANES == 0
    return pl.pallas_call(
        functools.partial(_proj_v_kernel, wn=wn),
        out_shape=(jax.ShapeDtypeStruct((m * N_KV_HEADS, HEAD_DIM), F32),
                   jax.ShapeDtypeStruct((m // wn, N_KV_HEADS * VT_ROWS, wn), BF16)),
        grid=(m // tm,),
        in_specs=[pl.BlockSpec((tm, k), lambda i: (i, 0)),
                  pl.BlockSpec((pl.Element(n), pl.Element(k)), lambda i: (c0, 0))],
        out_specs=(pl.BlockSpec((tm * N_KV_HEADS, HEAD_DIM), lambda i: (i, 0)),
                   pl.BlockSpec((tm // wn, N_KV_HEADS * VT_ROWS, wn), lambda i: (i, 0, 0))),
        compiler_params=_cparams(("arbitrary",)),
        name="proj_v",
    )(h, w)


def _dsa_kernel(*refs, tq, tk, t_new, l_cache, pos0, topk, has_cache):
    if has_cache:
        (qi_ref, wi_ref, q_ref, kin_ref, kn_ref, vtn_ref, cki_ref, ck_ref, cv_ref, o_ref,
         sc_ref, qit_ref, wt_ref, qt_ref, acc_ref, s_ref, p_ref, tau_ref, sct_ref) = refs
    else:
        (qi_ref, wi_ref, q_ref, kin_ref, kn_ref, vtn_ref, o_ref,
         sc_ref, qit_ref, wt_ref, qt_ref, acc_ref, s_ref, p_ref, tau_ref, sct_ref) = refs
        cki_ref = ck_ref = cv_ref = None
    i = pl.program_id(1)
    ntc = l_cache // tk
    wn = min(tk, t_new)
    nq = GROUP * tq
    q0 = pos0 + i * tq
    lane_q = lax.broadcasted_iota(jnp.int32, (1, tq), 1)
    qend = (((q0 + lane_q) >> 6) + 1) << 6
    last_end = (((q0 + tq - 1) >> 6) + 1) << 6
    nvis_new = jnp.minimum(last_end - pos0, t_new)
    ntn = (nvis_new + wn - 1) // wn
    ntiles = ntc + ntn

    def tpose(x):
        if tq == LANES:
            return x.T
        xf = x.astype(F32)
        if tq < LANES:
            xf = jnp.concatenate([xf, jnp.zeros((LANES - tq, LANES), F32)], axis=0)
        xt = xf.T
        return xt[:, :tq] if tq < LANES else xt

    qi_blk = qi_ref[0]
    for h in range(IDX_HEADS):
        qit_ref[:, h * tq:(h + 1) * tq] = tpose(qi_blk[:, h * IDX_DIM:(h + 1) * IDX_DIM]).astype(BF16)
    wt_ref[...] = tpose(wi_ref[0])
    q_blk = q_ref[0]
    for g in range(N_KV_HEADS):
        for a in range(GROUP):
            hh = GROUP * g + a
            qt_ref[g, :, a * tq:(a + 1) * tq] = tpose(q_blk[:, hh * HEAD_DIM:(hh + 1) * HEAD_DIM]).astype(BF16)

    def score_tile(ki_tile, kpos0, w):
        acc = jnp.zeros((w, tq), F32)
        hpd = 2 if 2 * tq % LANES == 0 else 1
        for hp in range(IDX_HEADS // hpd):
            lg = _dot(ki_tile, qit_ref[:, hpd * hp * tq:(hpd * hp + hpd) * tq])
            for e in range(hpd):
                h = hpd * hp + e
                acc = acc + wt_ref[h:h + 1, :] * jnp.maximum(lg[:, e * tq:(e + 1) * tq], 0.0)
        kpos = kpos0 + lax.broadcasted_iota(jnp.int32, (w, 1), 0)
        vis = kpos < qend
        s = jnp.where(vis, acc, NEG_INF)
        smin = _col_reduce(jnp.min, jnp.where(vis, acc, POS_INF))
        smax = _col_reduce(jnp.max, s)
        return s, smax, smin

    by_rows = tq < LANES and t_new <= tk

    def square_t(x, fill):
        r, c_ = x.shape
        if c_ < LANES:
            x = jnp.concatenate([x, jnp.full((r, LANES - c_), fill, F32)], axis=1)
        if r < LANES:
            x = jnp.concatenate([x, jnp.full((LANES - r, LANES), fill, F32)], axis=0)
        return x.T

    def stash_rows(t, s):
        st = square_t(s, NEG_INF)[:tq, :]
        if st.shape[1] < tk:
            sct_ref[t] = jnp.full((tq, tk), NEG_INF, F32)
            sct_ref[t, :, 0:st.shape[1]] = st
        else:
            sct_ref[t] = st

    def p1_cache(t, carry):
        mx, mn = carry
        r0 = pl.multiple_of(t * tk, tk)
        s, smax, smin = score_tile(cki_ref[0, pl.ds(r0, tk), :].astype(BF16), r0, tk)
        sc_ref[t] = s
        if by_rows:
            stash_rows(t, s)
        return jnp.maximum(mx, smax), jnp.minimum(mn, smin)

    def p1_new(j, carry):
        mx, mn = carry
        r0 = pl.multiple_of(j * wn, wn)
        s, smax, smin = score_tile(kin_ref[0, pl.ds(r0, wn), :], pos0 + r0, wn)
        if wn == tk:
            sc_ref[ntc + j] = s
        else:
            sc_ref[ntc + j] = jnp.full((tk, tq), NEG_INF, F32)
            sc_ref[ntc + j, 0:wn, :] = s
        if by_rows:
            stash_rows(ntc + j, s)
        return jnp.maximum(mx, smax), jnp.minimum(mn, smin)

    carry = (jnp.full((1, tq), NEG_INF, F32), jnp.full((1, tq), POS_INF, F32))
    if has_cache:
        carry = lax.fori_loop(0, ntc, p1_cache, carry)
    mx, mn = lax.fori_loop(0, ntn, p1_new, carry)

    nvis = jnp.clip(qend - pos0, 0, t_new)
    if has_cache:
        nvis = nvis + jnp.minimum(qend, l_cache)
    act0 = jnp.where(nvis > topk, 1.0, 0.0)
    kf = float(topk)
    cu = min(tk, COUNT_UNIT)
    upt = tk // cu
    n_units = ntc * upt + (nvis_new + cu - 1) // cu

    def bis_cond(carry):
        return jnp.logical_and(carry[0] < 400, carry[1] > 0)

    def unit(u):
        return sc_ref.at[u // upt, pl.ds(pl.multiple_of((u % upt) * cu, cu), cu), :]

    def count_where(pred):
        def body(u, cnt):
            ind = jnp.where(pred(unit(u)[...]), 1.0, 0.0)
            return cnt + jnp.sum(ind.reshape(cu // CNT_ROWS, CNT_ROWS, tq), axis=0)

        cnt = lax.fori_loop(0, n_units, body, jnp.zeros((CNT_ROWS, tq), F32))
        return jnp.sum(cnt, axis=0, keepdims=True)

    def count_ge_rows(mid):
        ind = jnp.where(sct_ref[...] >= mid[None], 1.0, 0.0)
        return jnp.sum(jnp.sum(ind, axis=0), axis=1, keepdims=True)

    def to_col(r):
        return square_t(jnp.broadcast_to(r, (LANES, tq)), 0.0)[:tq, 0:1]

    def to_row(col):
        return square_t(jnp.broadcast_to(col, (tq, LANES)), 0.0)[0:1, :tq]

    count_ge = count_ge_rows if by_rows else (lambda mid: count_where(lambda s: s >= mid))

    def bis_body(carry):
        it, _, lo, hi, tau, act, tied = carry
        mid = lo * 0.5 + hi * 0.5
        c = count_ge(mid)
        found = jnp.where(c == kf, act, 0.0)
        stuck = jnp.where(jnp.logical_or(mid <= lo, mid >= hi), act, 0.0) * (1.0 - found)
        above = jnp.where(c > kf, act, 0.0) * (1.0 - stuck)
        below = act * (1.0 - above) * (1.0 - found) * (1.0 - stuck)
        tau = jnp.where(found > 0, mid, tau)
        tau = jnp.where(stuck > 0, lo, tau)
        lo = jnp.where(above > 0, mid, lo)
        hi = jnp.where(below > 0, mid, hi)
        nact = act * (1.0 - found) * (1.0 - stuck)
        return it + 1, (jnp.max(nact) > 0).astype(jnp.int32), lo, hi, tau, nact, jnp.maximum(tied, stuck)

    flag0 = (jnp.max(act0) > 0).astype(jnp.int32)
    if by_rows:
        mn_c = to_col(mn)
        init = (jnp.int32(0), flag0, mn_c, to_col(mx), mn_c, to_col(act0), jnp.zeros((tq, 1), F32))
    else:
        init = (jnp.int32(0), flag0, mn, mx, mn, act0, jnp.zeros((1, tq), F32))
    res = lax.while_loop(bis_cond, lambda carry: bis_body(bis_body(carry)), init)
    tau_ref[...] = to_row(res[4]) if by_rows else res[4]
    tied = to_row(res[6]) if by_rows else res[6]

    @pl.when(jnp.max(tied) > 0)
    def _():
        lo_t = tau_ref[...]

        def tmin_body(u, acc):
            s = unit(u)[...]
            return jnp.minimum(acc, _col_reduce(jnp.min, jnp.where(s >= lo_t, s, POS_INF)))

        tstar = lax.fori_loop(0, n_units, tmin_body, jnp.full((1, tq), POS_INF, F32))
        tstar = jnp.where(tied > 0, tstar, lo_t)
        need = kf - count_where(lambda s: s > tstar)
        ri = lax.broadcasted_iota(jnp.int32, (cu, cu), 0)
        ci = lax.broadcasted_iota(jnp.int32, (cu, cu), 1)
        tri = jnp.where(ri >= ci, 1.0, 0.0).astype(BF16)

        def strike_body(u, run):
            s = unit(u)[...]
            eq = jnp.where(jnp.logical_and(s == tstar, tied > 0), 1.0, 0.0)
            rank = run + _dot(tri, eq.astype(BF16))
            unit(u)[...] = jnp.where(jnp.logical_and(eq > 0, rank > need), NEG_INF, s)
            return rank[cu - 1:cu, :]

        lax.fori_loop(0, n_units, strike_body, jnp.zeros((1, tq), F32))
        tau_ref[...] = tstar

    tau = tau_ref[...]

    acc_ref[...] = jnp.zeros(acc_ref.shape, F32)

    def attend(t, k_of_g, vt_of_g, w, ms):
        s_idx = sc_ref[t]
        if w < tk:
            s_idx = s_idx[0:w]
        bias = jnp.where(s_idx >= tau, 0.0, NEG_INF)
        bias4 = jnp.concatenate([bias] * GROUP, axis=1)
        def qk(g):
            s = _dot(k_of_g(g), qt_ref[g]) + bias4
            s_ref[g % STAGE_SLOTS, 0:w] = s
            return _col_reduce(jnp.max, s)

        out = []
        maxes = [qk(g) for g in range(STAGE_SLOTS - 1)]
        for g in range(N_KV_HEADS):
            slot = g % STAGE_SLOTS
            if g + STAGE_SLOTS - 1 < N_KV_HEADS:
                maxes.append(qk(g + STAGE_SLOTS - 1))
            m_old = ms[g]
            m_new = jnp.maximum(m_old, maxes[g])
            m_safe = jnp.where(m_new == NEG_INF, 0.0, m_new)
            alpha = jnp.exp2(m_old - m_safe)
            p_ref[slot, 0:w] = jnp.exp2(s_ref[slot, 0:w] - m_safe).astype(BF16)
            acc_ref[g] = acc_ref[g] * alpha + _dot(vt_of_g(g), p_ref[slot, 0:w])
            out.append(m_new)
        return tuple(out)

    def p3_cache(t, ms):
        r0 = pl.multiple_of(t * tk, tk)

        def vt(g):
            vg = cv_ref[0, pl.ds(r0 * N_KV_HEADS + g, tk, stride=N_KV_HEADS), :]
            return jnp.concatenate([vg.T, jnp.ones((BF16_ROWS, tk), F32)], axis=0).astype(BF16)

        return attend(
            t, lambda g: ck_ref[0, pl.ds(r0 * N_KV_HEADS + g, tk, stride=N_KV_HEADS), :].astype(BF16), vt, tk, ms)

    def p3_new(j, ms):
        r0 = pl.multiple_of(j * wn, wn)
        return attend(ntc + j,
                      lambda g: kn_ref[0, pl.ds(r0, wn), g * HEAD_DIM:(g + 1) * HEAD_DIM],
                      lambda g: vtn_ref[0, j, g * VT_ROWS:(g + 1) * VT_ROWS, :], wn, ms)

    ms = tuple(jnp.full((1, nq), NEG_INF, F32) for _ in range(N_KV_HEADS))
    if has_cache:
        ms = lax.fori_loop(0, ntc, p3_cache, ms)
    lax.fori_loop(0, ntn, p3_new, ms)

    for g in range(N_KV_HEADS):
        acc = acc_ref[g]
        og = (acc[0:HEAD_DIM] / acc[HEAD_DIM:HEAD_DIM + 1]).T
        for a in range(GROUP):
            hh = GROUP * g + a
            o_ref[0, :, hh * HEAD_DIM:(hh + 1) * HEAD_DIM] = og[a * tq:(a + 1) * tq, :].astype(o_ref.dtype)


def _dsa(qi, wi, q, kin, kn, vtn, cache, pos0, tq, tk):
    b, t, _ = q.shape
    has_cache = cache is not None
    l_cache = cache[0].shape[1] if has_cache else 0
    topk = min(TOPK_MAX, (l_cache + t) // 4)
    wn = min(tk, t)
    ntn = t // wn
    nt = l_cache // tk + ntn
    kvd = N_KV_HEADS * HEAD_DIM
    nq = GROUP * tq
    res = pl.Buffered(1) if b == 1 else None
    in_specs = [pl.BlockSpec((1, tq, IDX_HEADS * IDX_DIM), lambda bb, i: (bb, i, 0)),
                pl.BlockSpec((1, tq, LANES), lambda bb, i: (bb, i, 0)),
                pl.BlockSpec((1, tq, N_HEADS * HEAD_DIM), lambda bb, i: (bb, i, 0)),
                pl.BlockSpec((1, t, IDX_DIM), lambda bb, i: (bb, 0, 0), pipeline_mode=res),
                pl.BlockSpec((1, t, kvd), lambda bb, i: (bb, 0, 0), pipeline_mode=res),
                pl.BlockSpec((1, ntn, N_KV_HEADS * VT_ROWS, wn), lambda bb, i: (bb, 0, 0, 0), pipeline_mode=res)]
    args = [qi, wi, q, kin, kn, vtn]
    if has_cache:
        in_specs += [pl.BlockSpec((1, l_cache, IDX_DIM), lambda bb, i: (bb, 0, 0)),
                     pl.BlockSpec((1, l_cache * N_KV_HEADS, HEAD_DIM), lambda bb, i: (bb, 0, 0)),
                     pl.BlockSpec((1, l_cache * N_KV_HEADS, HEAD_DIM), lambda bb, i: (bb, 0, 0))]
        args += list(cache)
    kern = functools.partial(_dsa_kernel, tq=tq, tk=tk, t_new=t, l_cache=l_cache, pos0=pos0, topk=topk,
                             has_cache=has_cache)
    return pl.pallas_call(
        kern,
        out_shape=jax.ShapeDtypeStruct((b, t, N_HEADS * HEAD_DIM), BF16),
        grid=(b, t // tq),
        in_specs=in_specs,
        out_specs=pl.BlockSpec((1, tq, N_HEADS * HEAD_DIM), lambda bb, i: (bb, i, 0)),
        scratch_shapes=[pltpu.VMEM((nt, tk, tq), F32),
                        pltpu.VMEM((IDX_DIM, IDX_HEADS * tq), BF16),
                        pltpu.VMEM((LANES, tq), F32),
                        pltpu.VMEM((N_KV_HEADS, HEAD_DIM, nq), BF16),
                        pltpu.VMEM((N_KV_HEADS, VT_ROWS, nq), F32),
                        pltpu.VMEM((STAGE_SLOTS, tk, nq), F32),
                        pltpu.VMEM((STAGE_SLOTS, tk, nq), BF16),
                        pltpu.VMEM((1, tq), F32),
                        pltpu.VMEM((nt if (tq < LANES and t <= tk) else 1, tq, tk), F32)],
        compiler_params=_cparams(("arbitrary", "arbitrary")),
        name="dsa",
    )(*args)


def _hgrn_kernel(*refs, c, nchunk, nh, has_state):
    if has_state:
        q_ref, lf_ref, kk_ref, v_ref, hg_ref, nw_ref, s0_ref, o_ref, sout_ref, st_ref, g_ref = refs
    else:
        q_ref, lf_ref, kk_ref, v_ref, hg_ref, nw_ref, o_ref, sout_ref, st_ref, g_ref = refs
        s0_ref = None
    r = pl.program_id(2)

    @pl.when(r == 0)
    def _():
        for hh in range(nh):
            if has_state:
                st_ref[hh] = s0_ref[0, hh].T
            else:
                st_ref[hh] = jnp.zeros((HG_DV, HG_DK), F32)

    ri = lax.broadcasted_iota(jnp.int32, (c, c), 0)
    ci = lax.broadcasted_iota(jnp.int32, (c, c), 1)
    tri = jnp.where(ri >= ci, 1.0, 0.0).astype(BF16)
    nsub = c // SUB_BLOCK
    pair = lax.broadcasted_iota(jnp.int32, (SUB_BLOCK * SUB_BLOCK, HG_DK), 0)
    cap = jnp.where((pair % SUB_BLOCK) >= (pair // SUB_BLOCK), 0.0, NEG_INF)
    ones_w = jnp.ones((HG_DK, LANES), BF16)

    def rep_rows(row):
        return jnp.concatenate(
            [jnp.broadcast_to(row(sg), (SUB_BLOCK, HG_DK)) for sg in range(SUB_BLOCK)], axis=0)

    def cumulative(base):
        lf = lf_ref[pl.ds(base, c), :]
        l1 = lf.astype(BF16)
        r1 = lf - l1.astype(F32)
        l2 = r1.astype(BF16)
        l3 = (r1 - l2.astype(F32)).astype(BF16)
        return _dot(tri, l1) + _dot(tri, l2) + _dot(tri, l3)

    def stage_a(base, hh, g_all):
        cs = slice(hh * HG_DK, (hh + 1) * HG_DK)
        G = g_all[:, cs]
        kk = kk_ref[pl.ds(base, c), cs]
        v = v_ref[pl.ds(base, c), cs].astype(F32)
        g_ref[0, hh] = G
        g_ref[1, hh] = kk
        g_ref[2, hh] = v
        return dict(cs=cs, G=G, hh=hh, q=q_ref[pl.ds(base, c), cs].astype(F32), kk=kk, v=v)

    def stage_b(hh, d):
        q, kk, v, G = d["q"], d["kk"], d["v"], d["G"]
        st = st_ref[hh]
        vb = v.astype(BF16)
        d["vb"] = vb
        d["o_inter"] = _dot_nt((q * jnp.exp2(G)).astype(BF16), st.astype(BF16))
        d["A"], diag = [], []
        for i in range(nsub):
            lo_, hi_ = i * SUB_BLOCK, (i + 1) * SUB_BLOCK
            qi_ = q[lo_:hi_]
            Gi = G[lo_:hi_]
            if i > 0:
                Gb = G[lo_ - 1:lo_]
                qt = qi_ * jnp.exp2(Gi - Gb)
                kt = kk[:lo_] * jnp.exp2(Gb - G[:lo_])
                a_blk = _dot_nt(qt.astype(BF16), kt.astype(BF16))
                d["A"].append(jnp.concatenate([a_blk, jnp.zeros((SUB_BLOCK, c - lo_), F32)], axis=1))
            qrep = jnp.concatenate([qi_] * SUB_BLOCK, axis=0)
            grep = jnp.concatenate([Gi] * SUB_BLOCK, axis=0)
            kk_rep = rep_rows(lambda sg: g_ref[1, hh, lo_ + sg:lo_ + sg + 1, :])
            g_rep = rep_rows(lambda sg: g_ref[0, hh, lo_ + sg:lo_ + sg + 1, :])
            diag.append((qrep * kk_rep * jnp.exp2(jnp.minimum(grep - g_rep, cap))).astype(BF16))
        rs = _dot(jnp.concatenate(diag, axis=0) if nsub > 1 else diag[0], ones_w)
        pairs = SUB_BLOCK * SUB_BLOCK
        d["rs"] = [rs[i * pairs:(i + 1) * pairs] for i in range(nsub)]
        Gl = G[c - 1:c]
        kdec = kk * jnp.exp2(Gl - G)
        st_ref[hh] = st * jnp.exp2(Gl) + _dot_tn(vb, kdec.astype(BF16))

    def stage_c(base, d):
        parts = []
        o_base = d["o_inter"]
        if nsub > 1:
            a_low = jnp.concatenate([jnp.zeros((SUB_BLOCK, c), F32)] + d["A"], axis=0)
            o_base = o_base + _dot(a_low.astype(BF16), d["vb"])
        for i in range(nsub):
            lo_, hi_ = i * SUB_BLOCK, (i + 1) * SUB_BLOCK
            oi = o_base[lo_:hi_]
            v_rep = rep_rows(lambda sg: g_ref[2, d["hh"], lo_ + sg:lo_ + sg + 1, :])
            contrib = (d["rs"][i] * v_rep).reshape(SUB_BLOCK, SUB_BLOCK, HG_DV)
            parts.append(oi + jnp.sum(contrib, axis=0))
        o = jnp.concatenate(parts, axis=0) if nsub > 1 else parts[0]
        on = o * lax.rsqrt(jnp.mean(o * o, axis=1, keepdims=True) + EPS) * nw_ref[...]
        hg = hg_ref[pl.ds(base, c), d["cs"]].astype(F32)
        o_ref[pl.ds(base, c), d["cs"]] = (on * (hg * _sigmoid(hg))).astype(o_ref.dtype)

    def chunk(n, carry):
        base = pl.multiple_of(n * c, c)
        g_all = cumulative(base)
        heads = [stage_a(base, hh, g_all) for hh in range(nh)]
        for hh, d in enumerate(heads):
            stage_b(hh, d)
        for d in heads:
            stage_c(base, d)
        return carry

    lax.fori_loop(0, nchunk, chunk, 0)

    @pl.when(r == pl.num_programs(2) - 1)
    def _():
        for hh in range(nh):
            sout_ref[0, hh] = st_ref[hh].T


def _hgrn(hq, lf, kk, hv, hg, nw, s0, b, t, rb, nh):
    c = min(CHUNK, t)
    nchunk = rb // c
    nr = t // rb
    has_state = s0 is not None
    blk = pl.BlockSpec((rb, nh * HG_DK), lambda bb, h, r: (bb * nr + r, h))
    sblk = pl.BlockSpec((1, nh, HG_DK, HG_DV), lambda bb, h, r: (bb, h, 0, 0))
    in_specs = [blk, blk, blk, blk, blk, pl.BlockSpec((1, HG_DV), lambda bb, h, r: (0, 0))]
    args = [hq, lf, kk, hv, hg, nw]
    if has_state:
        in_specs.append(sblk)
        args.append(s0)
    kern = functools.partial(_hgrn_kernel, c=c, nchunk=nchunk, nh=nh, has_state=has_state)
    return pl.pallas_call(
        kern,
        out_shape=(jax.ShapeDtypeStruct((b * t, HG_HEADS * HG_DV), BF16),
                   jax.ShapeDtypeStruct((b, HG_HEADS, HG_DK, HG_DV), F32)),
        grid=(b, HG_HEADS // nh, nr),
        in_specs=in_specs,
        out_specs=(blk, sblk),
        scratch_shapes=[pltpu.VMEM((nh, HG_DV, HG_DK), F32), pltpu.VMEM((3, nh, c, HG_DK), F32)],
        compiler_params=_cparams(("arbitrary", "arbitrary", "arbitrary")),
        name="hgrn",
    )(*args)


def _merge_out_kernel(x_ref, oa_ref, oh_ref, ga_ref, gb_ref, w_ref, g1_ref, sc_ref, sh_ref, nw_ref,
                      x1_ref, h2_ref):
    bb, tb, d = x_ref.shape
    merged = (_sigmoid(ga_ref[...].astype(F32)) * oa_ref[...].astype(F32)
              + _sigmoid(gb_ref[...].astype(F32)) * oh_ref[...].astype(F32))
    y = _dot(merged.astype(BF16), w_ref[...]).reshape(bb, tb, d)
    x1 = x_ref[...] + g1_ref[...] * y
    x1_ref[...] = x1
    ms = jnp.mean(x1 * x1, axis=-1, keepdims=True)
    xn = x1 * lax.rsqrt(ms + EPS) * nw_ref[...]
    h2 = xn * (1.0 + sc_ref[...]) + sh_ref[...]
    h2_ref[...] = h2.reshape(bb * tb, d).astype(BF16)


def _merge_out(x, oa, oh, ga, gb, w_out, g1, sc2, sh2, nw2, bb, tb):
    b, t, d = x.shape
    tm = bb * tb
    nt = t // tb

    def row(i, j):
        return (i * nt + j, 0)

    def mod(i, j):
        return (i, 0, 0)

    x3 = pl.BlockSpec((bb, tb, d), lambda i, j: (i, j, 0))
    r2 = pl.BlockSpec((tm, d), row)
    return pl.pallas_call(
        _merge_out_kernel,
        out_shape=(jax.ShapeDtypeStruct((b, t, d), F32), jax.ShapeDtypeStruct((b * t, d), BF16)),
        grid=(b // bb, nt),
        in_specs=[x3, r2, r2, r2, r2,
                  pl.BlockSpec((d, d), lambda i, j: (0, 0)),
                  pl.BlockSpec((bb, 1, d), mod), pl.BlockSpec((bb, 1, d), mod), pl.BlockSpec((bb, 1, d), mod),
                  pl.BlockSpec((1, 1, d), lambda i, j: (0, 0, 0))],
        out_specs=(x3, r2),
        compiler_params=_cparams(("arbitrary", "arbitrary")),
        name="merge_out",
    )(x, oa, oh, ga, gb, w_out, g1, sc2, sh2, nw2)


def _mlp_kernel(h_ref, wu_ref, wd_ref, x1_ref, g2_ref, fw_ref, y_ref, acc_ref):
    f = pl.program_id(2)

    @pl.when(f == 0)
    def _():
        acc_ref[...] = jnp.zeros(acc_ref.shape, F32)

    u = jnp.maximum(_dot(h_ref[...], wu_ref[...]), 0.0)
    acc_ref[...] += _dot((u * u).astype(BF16), wd_ref[...])

    @pl.when(f == pl.num_programs(2) - 1)
    def _():
        bb, tb, d = x1_ref.shape
        x2 = x1_ref[...] + g2_ref[...] * acc_ref[...].reshape(bb, tb, d)
        ms = jnp.mean(x2 * x2, axis=-1, keepdims=True)
        y_ref[...] = x2 * lax.rsqrt(ms + EPS) * fw_ref[...]


def _mlp(h2, w_up, w_down, x1, g2, fw, bb, tb, tf):
    b, t, d = x1.shape
    dff = w_up.shape[1]
    tm = bb * tb
    nt = t // tb
    x3 = pl.BlockSpec((bb, tb, d), lambda i, j, f: (i, j, 0))
    return pl.pallas_call(
        _mlp_kernel,
        out_shape=jax.ShapeDtypeStruct((b, t, d), F32),
        grid=(b // bb, nt, dff // tf),
        in_specs=[pl.BlockSpec((tm, d), lambda i, j, f: (i * nt + j, 0)),
                  pl.BlockSpec((d, tf), lambda i, j, f: (0, f)),
                  pl.BlockSpec((tf, d), lambda i, j, f: (f, 0)),
                  x3,
                  pl.BlockSpec((bb, 1, d), lambda i, j, f: (i, 0, 0)),
                  pl.BlockSpec((1, 1, d), lambda i, j, f: (0, 0, 0))],
        out_specs=x3,
        scratch_shapes=[pltpu.VMEM((tm, d), F32)],
        compiler_params=_cparams(("arbitrary", "arbitrary", "arbitrary")),
        name="mlp",
    )(h2, w_up, w_down, x1, g2, fw)


def _rope_tables(pos):
    half = ROT_DIM // 2
    inv_freq = ROPE_THETA ** (-(jnp.arange(half, dtype=F32) * (2.0 / ROT_DIM)))
    ang = pos.astype(F32)[:, None] * inv_freq[None, :]
    cos, sin = jnp.cos(ang), jnp.sin(ang)
    n = pos.shape[0]
    ones = jnp.ones((n, LANES - ROT_DIM), F32)
    zeros = jnp.zeros((n, LANES - ROT_DIM), F32)
    zh = jnp.zeros((n, half), F32)
    c_t = jnp.concatenate([cos, cos, ones], axis=1)
    s_up = jnp.concatenate([-sin, zh, zeros], axis=1)
    s_dn = jnp.concatenate([zh, sin, zeros], axis=1)
    return c_t, s_up, s_dn


def _trunk(x, mod, pos0, past, wts, blocks):
    (norm1_w, w_parts, lb_logits, hg_norm_w, w_out, norm2_w, w_up, w_down, final_w) = wts
    b, t, d = x.shape
    bb, tb, tm_proj, tq, tk, rb, mlp_bb, mlp_tb = blocks
    m = [mod[:, i:i + 1, :] for i in range(6)]
    sh1, sc1, g1, sh2, sc2, g2 = m
    h = _normmod(x, sc1, sh1, norm1_w.reshape(1, 1, d), bb, tb).reshape(b * t, d)

    pos = pos0 + jnp.arange(t, dtype=jnp.int32)
    tabs = tuple(jnp.tile(tb_, (b, 1)) for tb_ in _rope_tables(pos))
    wq, wk, wv, wqi, wki, wwi, whq, whf, whi, whg, wga, wgb = w_parts
    tm = min(tm_proj, b * t)
    wn = min(tk, t)
    (q_bf,) = _proj("rope", h, wq, tm, PROJ_TN, tabs, (BF16,), scale=HEAD_DIM ** -0.5 * LOG2E)
    k_f, k_bf = _proj("rope", h, wk, tm, PROJ_TN, tabs, (F32, BF16), head_rows=(0,))
    v_f, vt_bf = _proj_v(h, wv, tm, wn)
    (qi_bf,) = _proj("rope", h, wqi, tm, PROJ_TN, tabs, (BF16,))
    ki_f, ki_bf = _proj("rope", h, wki, tm, LANES, tabs, (F32, BF16))
    (wi_f,) = _proj("plain", h, wwi, tm, LANES, (), (F32,), scale=IDX_HEADS ** -0.5 * IDX_DIM ** -0.5)
    (hq,) = _proj("plain", h, whq, tm, PROJ_TN, (), (BF16,))
    lf, kk = _proj("forget", h, whf, tm, PROJ_TN, (lb_logits,), (F32, F32))
    (hi,) = _proj("plain", h, whi, tm, PROJ_TN, (), (BF16,))
    (hg,) = _proj("plain", h, whg, tm, PROJ_TN, (), (BF16,))
    (ga,) = _proj("plain", h, wga, tm, PROJ_TN, (), (BF16,))
    (gb,) = _proj("plain", h, wgb, tm, PROJ_TN, (), (BF16,))

    r3 = lambda a: a.reshape(b, t, a.shape[-1])
    if past is None:
        cache, s0 = None, None
    else:
        ck, cv, cki, s0 = past
        lc = ck.shape[1]
        cache = (cki, ck.reshape(b, lc * N_KV_HEADS, HEAD_DIM), cv.reshape(b, lc * N_KV_HEADS, HEAD_DIM))
    vtn = vt_bf.reshape(b, t // wn, N_KV_HEADS * VT_ROWS, wn)
    o_attn = _dsa(r3(qi_bf), r3(wi_f), r3(q_bf), r3(ki_bf), r3(k_bf), vtn, cache, pos0, tq, tk)
    o_hg, s_new = _hgrn(hq, lf, kk, hi, hg, hg_norm_w.reshape(1, HG_DV), s0, b, t, rb, 16)

    x1, h2 = _merge_out(x, o_attn.reshape(b * t, d), o_hg, ga, gb, w_out, g1, sc2, sh2,
                        norm2_w.reshape(1, 1, d), bb, tb)
    y = _mlp(h2, w_up, w_down, x1, g2, final_w.reshape(1, 1, d), mlp_bb, mlp_tb, 1024)
    return (y, k_f.reshape(b, t, N_KV_HEADS, HEAD_DIM), v_f.reshape(b, t, N_KV_HEADS, HEAD_DIM),
            ki_f.reshape(b, t, IDX_DIM), s_new)


def kernel(x_prompt, x_sample, cache_k, cache_v, cache_ki, state_hgrn, c_prompt, c_sample, w_ada, b_ada, norm1_w,
           w_in, hg_lb_logits, hg_norm_w, w_out, norm2_w, w_up, w_down, final_norm_w):
    depth = w_in.shape[0]
    assert depth == 1
    d = x_prompt.shape[-1]
    bp, tp, _ = x_prompt.shape
    bs, ts, _ = x_sample.shape
    past_len = cache_k.shape[2]

    c_all = jnp.concatenate([c_prompt, c_sample], axis=0)
    nrow = c_all.shape[0]
    pad = (-nrow) % SUBLANES
    c_all = jnp.pad(c_all, ((0, pad), (0, 0)))
    mod = _ada(c_all, w_ada[0], b_ada[0].reshape(1, -1)).reshape(nrow + pad, 6, d)

    sizes = (N_HEADS * HEAD_DIM, N_KV_HEADS * HEAD_DIM, N_KV_HEADS * HEAD_DIM, IDX_HEADS * IDX_DIM, IDX_DIM,
             IDX_HEADS, HG_HEADS * HG_DK, HG_HEADS * HG_DK, HG_HEADS * HG_DV, HG_HEADS * HG_DV, d, d)
    offs = np.concatenate([[0], np.cumsum(sizes)])
    w_t = jnp.transpose(w_in[0])
    w_parts = [(w_t, int(offs[i]), sz + (-sz) % LANES) for i, sz in enumerate(sizes)]
    wts = (norm1_w[0], tuple(w_parts), hg_lb_logits.astype(F32), hg_norm_w[0], w_out[0].astype(BF16), norm2_w[0],
           w_up[0].astype(BF16), w_down[0].astype(BF16), final_norm_w)

    yp, kp, vp, kip, sp = _trunk(x_prompt, mod[:bp], 0, None, wts, (1, 256, 2048, 128, 1024, 512, 1, 512))
    past = (cache_k[0], cache_v[0], cache_ki[0], state_hgrn[0])
    ys, ks, vs, kis, ss = _trunk(x_sample, mod[bp:bp + bs], past_len, past, wts,
                                 (bs // 2, ts, bs * ts, ts, 512, ts, bs, ts))
    return (yp, ys, kp[None], vp[None], kip[None], sp[None], ks[None], vs[None], kis[None], ss[None])
```

```python
import functools
import math

import jax
import jax.numpy as jnp
import numpy as np
from jax import lax
from jax.experimental import pallas as pl
from jax.experimental.pallas import tpu as pltpu

CHUNK = 64
N_HEADS = 16
HEAD_DIM = 128
N_KV_HEADS = 4
GROUP = N_HEADS // N_KV_HEADS
ROT_DIM = HEAD_DIM // 4
ROPE_THETA = 500000.0
IDX_HEADS = 16
IDX_DIM = 128
TOPK_MAX = 256
HG_HEADS = 16
HG_DK = 128
HG_DV = 128
EPS = 1e-6
LANES = 128
SUBLANES = 8
BF16_ROWS = 16
SUB_BLOCK = 8
PROJ_TN = 512
COUNT_UNIT = 512
STAGE_SLOTS = 3
CNT_ROWS = 64
VT_ROWS = HEAD_DIM + BF16_ROWS
VMEM_LIMIT = 56 * 1024 * 1024
NEG_INF = float("-inf")
POS_INF = float("inf")
LOG2E = math.log2(math.e)

F32 = jnp.float32
BF16 = jnp.bfloat16


def _cparams(sem):
    return pltpu.CompilerParams(dimension_semantics=sem, vmem_limit_bytes=VMEM_LIMIT)


def _dot_nt(a, b):
    return lax.dot_general(a, b, (((1,), (1,)), ((), ())), preferred_element_type=F32)


def _dot_tn(a, b):
    return lax.dot_general(a, b, (((0,), (0,)), ((), ())), preferred_element_type=F32)


def _dot(a, b):
    return jnp.dot(a, b, preferred_element_type=F32)


def _sigmoid(x):
    return 1.0 / (1.0 + jnp.exp(-x))


def _col_reduce(op, x):
    rows, n = x.shape
    if rows > CNT_ROWS and rows % CNT_ROWS == 0:
        x = op(x.reshape(rows // CNT_ROWS, CNT_ROWS, n), axis=0)
    return op(x, axis=0, keepdims=True)


def _ada_kernel(c_ref, w_ref, b_ref, o_ref):
    o_ref[...] = _dot(c_ref[...].astype(BF16), w_ref[...].astype(BF16)) + b_ref[...]


def _ada(c_all, w_ada, b_ada, tn=1024):
    r, d = c_all.shape
    n = w_ada.shape[1]
    return pl.pallas_call(
        _ada_kernel,
        out_shape=jax.ShapeDtypeStruct((r, n), F32),
        grid=(n // tn,),
        in_specs=[pl.BlockSpec((r, d), lambda j: (0, 0)),
                  pl.BlockSpec((d, tn), lambda j: (0, j)),
                  pl.BlockSpec((1, tn), lambda j: (0, j))],
        out_specs=pl.BlockSpec((r, tn), lambda j: (0, j)),
        compiler_params=_cparams(("arbitrary",)),
        name="ada",
    )(c_all, w_ada, b_ada)


def _normmod_kernel(x_ref, sc_ref, sh_ref, g_ref, o_ref):
    x = x_ref[...]
    ms = jnp.mean(x * x, axis=-1, keepdims=True)
    xn = x * lax.rsqrt(ms + EPS) * g_ref[...]
    o_ref[...] = (xn * (1.0 + sc_ref[...]) + sh_ref[...]).astype(o_ref.dtype)


def _normmod(x, sc, sh, g, bb, tb):
    b, t, d = x.shape
    return pl.pallas_call(
        _normmod_kernel,
        out_shape=jax.ShapeDtypeStruct((b, t, d), BF16),
        grid=(b // bb, t // tb),
        in_specs=[pl.BlockSpec((bb, tb, d), lambda i, j: (i, j, 0)),
                  pl.BlockSpec((bb, 1, d), lambda i, j: (i, 0, 0)),
                  pl.BlockSpec((bb, 1, d), lambda i, j: (i, 0, 0)),
                  pl.BlockSpec((1, 1, d), lambda i, j: (0, 0, 0))],
        out_specs=pl.BlockSpec((bb, tb, d), lambda i, j: (i, j, 0)),
        compiler_params=_cparams(("arbitrary", "arbitrary")),
        name="normmod",
    )(x, sc, sh, g)


def _proj_plain_kernel(h_ref, w_ref, *o_refs, scale):
    z = _dot_nt(h_ref[...], w_ref[...].astype(BF16))
    if scale != 1.0:
        z = z * scale
    for o_ref in o_refs:
        o_ref[...] = z.astype(o_ref.dtype)


def _store_head_rows(o_ref, r):
    heads = r.shape[1] // LANES
    for g in range(heads):
        o_ref[pl.ds(g, r.shape[0], stride=heads), :] = r[:, g * LANES:(g + 1) * LANES].astype(o_ref.dtype)


def _proj_rope_kernel(h_ref, w_ref, cos_ref, sup_ref, sdn_ref, *o_refs, scale):
    z = _dot_nt(h_ref[...], w_ref[...].astype(BF16))
    tn = z.shape[1]
    reps = tn // LANES

    def wide(ref):
        t = ref[...]
        return t if reps == 1 else jnp.concatenate([t] * reps, axis=1)

    up = pltpu.roll(z, tn - ROT_DIM // 2, 1)
    dn = pltpu.roll(z, ROT_DIM // 2, 1)
    r = z * wide(cos_ref) + up * wide(sup_ref) + dn * wide(sdn_ref)
    for o_ref in o_refs:
        if o_ref.shape[0] != r.shape[0]:
            _store_head_rows(o_ref, r)
        elif o_ref.dtype == BF16 and scale != 1.0:
            o_ref[...] = (r * scale).astype(BF16)
        else:
            o_ref[...] = r.astype(o_ref.dtype)


def _proj_forget_kernel(h_ref, w_ref, lbl_ref, lf_ref, kk_ref):
    z = _dot_nt(h_ref[...], w_ref[...].astype(BF16))
    lbl = lbl_ref[...]
    mx = jnp.max(lbl, axis=0, keepdims=True)
    e = jnp.exp(lbl - mx)
    lb = e[0:1, :] / jnp.sum(e, axis=0, keepdims=True)
    f = lb + (1.0 - lb) * _sigmoid(z)
    lf_ref[...] = jnp.log(f) * LOG2E
    kk_ref[...] = 1.0 - f


def _proj(kind, h, wspec, tm, tn, extra=(), out_dtypes=(F32,), scale=1.0, head_rows=()):
    w, c0, n = wspec
    m, k = h.shape
    assert c0 % SUBLANES == 0 and n % tn == 0
    in_specs = [pl.BlockSpec((tm, k), lambda i, j: (i, 0)),
                pl.BlockSpec((pl.Element(tn), pl.Element(k)),
                             lambda i, j: (pl.multiple_of(c0 + j * tn, SUBLANES), 0))]
    if kind == "rope":
        kern = functools.partial(_proj_rope_kernel, scale=scale)
        in_specs += [pl.BlockSpec((tm, LANES), lambda i, j: (i, 0))] * 3
    elif kind == "forget":
        kern = _proj_forget_kernel
        in_specs += [pl.BlockSpec((extra[0].shape[0], tn), lambda i, j: (0, j))]
    else:
        kern = functools.partial(_proj_plain_kernel, scale=scale)
    hpt = tn // LANES
    outs, out_specs = [], []
    for o, dt in enumerate(out_dtypes):
        if o in head_rows:
            assert n == tn
            outs.append(jax.ShapeDtypeStruct((m * hpt, LANES), dt))
            out_specs.append(pl.BlockSpec((tm * hpt, LANES), lambda i, j: (i, 0)))
        else:
            outs.append(jax.ShapeDtypeStruct((m, n), dt))
            out_specs.append(pl.BlockSpec((tm, tn), lambda i, j: (i, j)))
    res = pl.pallas_call(
        kern,
        out_shape=tuple(outs),
        grid=(m // tm, n // tn),
        in_specs=in_specs,
        out_specs=tuple(out_specs),
        compiler_params=_cparams(("arbitrary", "arbitrary")),
        name="proj_" + kind,
    )(h, w, *extra)
    return res


def _proj_v_kernel(h_ref, w_ref, vf_ref, vt_ref, *, wn):
    z = _dot_nt(h_ref[...], w_ref[...].astype(BF16))
    _store_head_rows(vf_ref, z)
    zt = z.T
    ones = jnp.ones((BF16_ROWS, wn), BF16)
    for u in range(z.shape[0] // wn):
        blk = zt[:, u * wn:(u + 1) * wn].astype(BF16)
        for g in range(N_KV_HEADS):
            vt_ref[u, g * VT_ROWS:g * VT_ROWS + HEAD_DIM, :] = blk[g * HEAD_DIM:(g + 1) * HEAD_DIM]
            vt_ref[u, g * VT_ROWS + HEAD_DIM:(g + 1) * VT_ROWS, :] = ones


def _proj_v(h, wspec, tm, wn):
    w, c0, n = wspec
    m, k = h.shape
    assert c0 % SUBLANES == 0
    return pl.pallas_call(
        functools.partial(_proj_v_kernel, wn=wn),
        out_shape=(jax.ShapeDtypeStruct((m * N_KV_HEADS, HEAD_DIM), F32),
                   jax.ShapeDtypeStruct((m // wn, N_KV_HEADS * VT_ROWS, wn), BF16)),
        grid=(m // tm,),
        in_specs=[pl.BlockSpec((tm, k), lambda i: (i, 0)),
                  pl.BlockSpec((pl.Element(n), pl.Element(k)), lambda i: (c0, 0))],
        out_specs=(pl.BlockSpec((tm * N_KV_HEADS, HEAD_DIM), lambda i: (i, 0)),
                   pl.BlockSpec((tm // wn, N_KV_HEADS * VT_ROWS, wn), lambda i: (i, 0, 0))),
        compiler_params=_cparams(("arbitrary",)),
        name="proj_v",
    )(h, w)


def _dsa_kernel(*refs, tq, tk, t_new, l_cache, pos0, topk, has_cache):
    if has_cache:
        (qi_ref, wi_ref, q_ref, kin_ref, kn_ref, vtn_ref, cki_ref, ck_ref, cv_ref, o_ref,
         sc_ref, qit_ref, wt_ref, qt_ref, acc_ref, s_ref, p_ref, tau_ref, sct_ref) = refs
    else:
        (qi_ref, wi_ref, q_ref, kin_ref, kn_ref, vtn_ref, o_ref,
         sc_ref, qit_ref, wt_ref, qt_ref, acc_ref, s_ref, p_ref, tau_ref, sct_ref) = refs
        cki_ref = ck_ref = cv_ref = None
    i = pl.program_id(1)
    ntc = l_cache // tk
    wn = min(tk, t_new)
    nq = GROUP * tq
    q0 = pos0 + i * tq
    lane_q = lax.broadcasted_iota(jnp.int32, (1, tq), 1)
    qend = (((q0 + lane_q) >> 6) + 1) << 6
    last_end = (((q0 + tq - 1) >> 6) + 1) << 6
    nvis_new = jnp.minimum(last_end - pos0, t_new)
    ntn = (nvis_new + wn - 1) // wn
    ntiles = ntc + ntn

    def tpose(x):
        if tq == LANES:
            return x.T
        xf = x.astype(F32)
        if tq < LANES:
            xf = jnp.concatenate([xf, jnp.zeros((LANES - tq, LANES), F32)], axis=0)
        xt = xf.T
        return xt[:, :tq] if tq < LANES else xt

    qi_blk = qi_ref[0]
    for h in range(IDX_HEADS):
        qit_ref[:, h * tq:(h + 1) * tq] = tpose(qi_blk[:, h * IDX_DIM:(h + 1) * IDX_DIM]).astype(BF16)
    wt_ref[...] = tpose(wi_ref[0])
    q_blk = q_ref[0]
    for g in range(N_KV_HEADS):
        for a in range(GROUP):
            hh = GROUP * g + a
            qt_ref[g, :, a * tq:(a + 1) * tq] = tpose(q_blk[:, hh * HEAD_DIM:(hh + 1) * HEAD_DIM]).astype(BF16)

    def score_tile(ki_tile, kpos0, w):
        acc = jnp.zeros((w, tq), F32)
        hpd = 2 if 2 * tq % LANES == 0 else 1
        for hp in range(IDX_HEADS // hpd):
            lg = _dot(ki_tile, qit_ref[:, hpd * hp * tq:(hpd * hp + hpd) * tq])
            for e in range(hpd):
                h = hpd * hp + e
                acc = acc + wt_ref[h:h + 1, :] * jnp.maximum(lg[:, e * tq:(e + 1) * tq], 0.0)
        kpos = kpos0 + lax.broadcasted_iota(jnp.int32, (w, 1), 0)
        vis = kpos < qend
        s = jnp.where(vis, acc, NEG_INF)
        smin = _col_reduce(jnp.min, jnp.where(vis, acc, POS_INF))
        smax = _col_reduce(jnp.max, s)
        return s, smax, smin

    by_rows = tq < LANES and t_new <= tk

    def square_t(x, fill):
        r, c_ = x.shape
        if c_ < LANES:
            x = jnp.concatenate([x, jnp.full((r, LANES - c_), fill, F32)], axis=1)
        if r < LANES:
            x = jnp.concatenate([x, jnp.full((LANES - r, LANES), fill, F32)], axis=0)
        return x.T

    def stash_rows(t, s):
        st = square_t(s, NEG_INF)[:tq, :]
        if st.shape[1] < tk:
            sct_ref[t] = jnp.full((tq, tk), NEG_INF, F32)
            sct_ref[t, :, 0:st.shape[1]] = st
        else:
            sct_ref[t] = st

    def p1_cache(t, carry):
        mx, mn = carry
        r0 = pl.multiple_of(t * tk, tk)
        s, smax, smin = score_tile(cki_ref[0, pl.ds(r0, tk), :].astype(BF16), r0, tk)
        sc_ref[t] = s
        if by_rows:
            stash_rows(t, s)
        return jnp.maximum(mx, smax), jnp.minimum(mn, smin)

    def p1_new(j, carry):
        mx, mn = carry
        r0 = pl.multiple_of(j * wn, wn)
        s, smax, smin = score_tile(kin_ref[0, pl.ds(r0, wn), :], pos0 + r0, wn)
        if wn == tk:
            sc_ref[ntc + j] = s
        else:
            sc_ref[ntc + j] = jnp.full((tk, tq), NEG_INF, F32)
            sc_ref[ntc + j, 0:wn, :] = s
        if by_rows:
            stash_rows(ntc + j, s)
        return jnp.maximum(mx, smax), jnp.minimum(mn, smin)

    carry = (jnp.full((1, tq), NEG_INF, F32), jnp.full((1, tq), POS_INF, F32))
    if has_cache:
        carry = lax.fori_loop(0, ntc, p1_cache, carry)
    mx, mn = lax.fori_loop(0, ntn, p1_new, carry)

    nvis = jnp.clip(qend - pos0, 0, t_new)
    if has_cache:
        nvis = nvis + jnp.minimum(qend, l_cache)
    act0 = jnp.where(nvis > topk, 1.0, 0.0)
    kf = float(topk)
    cu = min(tk, COUNT_UNIT)
    upt = tk // cu
    n_units = ntc * upt + (nvis_new + cu - 1) // cu

    def bis_cond(carry):
        return jnp.logical_and(carry[0] < 400, carry[1] > 0)

    def unit(u):
        return sc_ref.at[u // upt, pl.ds(pl.multiple_of((u % upt) * cu, cu), cu), :]

    def count_where(pred):
        def body(u, cnt):
            ind = jnp.where(pred(unit(u)[...]), 1.0, 0.0)
            return cnt + jnp.sum(ind.reshape(cu // CNT_ROWS, CNT_ROWS, tq), axis=0)

        cnt = lax.fori_loop(0, n_units, body, jnp.zeros((CNT_ROWS, tq), F32))
        return jnp.sum(cnt, axis=0, keepdims=True)

    def count_ge_rows(mid):
        ind = jnp.where(sct_ref[...] >= mid[None], 1.0, 0.0)
        return jnp.sum(jnp.sum(ind, axis=0), axis=1, keepdims=True)

    def to_col(r):
        return square_t(jnp.broadcast_to(r, (LANES, tq)), 0.0)[:tq, 0:1]

    def to_row(col):
        return square_t(jnp.broadcast_to(col, (tq, LANES)), 0.0)[0:1, :tq]

    count_ge = count_ge_rows if by_rows else (lambda mid: count_where(lambda s: s >= mid))

    def bis_body(carry):
        it, _, lo, hi, tau, act, tied = carry
        mid = lo * 0.5 + hi * 0.5
        c = count_ge(mid)
        found = jnp.where(c == kf, act, 0.0)
        stuck = jnp.where(jnp.logical_or(mid <= lo, mid >= hi), act, 0.0) * (1.0 - found)
        above = jnp.where(c > kf, act, 0.0) * (1.0 - stuck)
        below = act * (1.0 - above) * (1.0 - found) * (1.0 - stuck)
        tau = jnp.where(found > 0, mid, tau)
        tau = jnp.where(stuck > 0, lo, tau)
        lo = jnp.where(above > 0, mid, lo)
        hi = jnp.where(below > 0, mid, hi)
        nact = act * (1.0 - found) * (1.0 - stuck)
        return it + 1, (jnp.max(nact) > 0).astype(jnp.int32), lo, hi, tau, nact, jnp.maximum(tied, stuck)

    flag0 = (jnp.max(act0) > 0).astype(jnp.int32)
    if by_rows:
        mn_c = to_col(mn)
        init = (jnp.int32(0), flag0, mn_c, to_col(mx), mn_c, to_col(act0), jnp.zeros((tq, 1), F32))
    else:
        init = (jnp.int32(0), flag0, mn, mx, mn, act0, jnp.zeros((1, tq), F32))
    res = lax.while_loop(bis_cond, lambda carry: bis_body(bis_body(carry)), init)
    tau_ref[...] = to_row(res[4]) if by_rows else res[4]
    tied = to_row(res[6]) if by_rows else res[6]

    @pl.when(jnp.max(tied) > 0)
    def _():
        lo_t = tau_ref[...]

        def tmin_body(u, acc):
            s = unit(u)[...]
            return jnp.minimum(acc, _col_reduce(jnp.min, jnp.where(s >= lo_t, s, POS_INF)))

        tstar = lax.fori_loop(0, n_units, tmin_body, jnp.full((1, tq), POS_INF, F32))
        tstar = jnp.where(tied > 0, tstar, lo_t)
        need = kf - count_where(lambda s: s > tstar)
        ri = lax.broadcasted_iota(jnp.int32, (cu, cu), 0)
        ci = lax.broadcasted_iota(jnp.int32, (cu, cu), 1)
        tri = jnp.where(ri >= ci, 1.0, 0.0).astype(BF16)

        def strike_body(u, run):
            s = unit(u)[...]
            eq = jnp.where(jnp.logical_and(s == tstar, tied > 0), 1.0, 0.0)
            rank = run + _dot(tri, eq.astype(BF16))
            unit(u)[...] = jnp.where(jnp.logical_and(eq > 0, rank > need), NEG_INF, s)
            return rank[cu - 1:cu, :]

        lax.fori_loop(0, n_units, strike_body, jnp.zeros((1, tq), F32))
        tau_ref[...] = tstar

    tau = tau_ref[...]

    acc_ref[...] = jnp.zeros(acc_ref.shape, F32)

    def attend(t, k_of_g, vt_of_g, w, ms):
        s_idx = sc_ref[t]
        if w < tk:
            s_idx = s_idx[0:w]
        bias = jnp.where(s_idx >= tau, 0.0, NEG_INF)
        bias4 = jnp.concatenate([bias] * GROUP, axis=1)
        def qk(g):
            s = _dot(k_of_g(g), qt_ref[g]) + bias4
            s_ref[g % STAGE_SLOTS, 0:w] = s
            return _col_reduce(jnp.max, s)

        out = []
        maxes = [qk(g) for g in range(STAGE_SLOTS - 1)]
        for g in range(N_KV_HEADS):
            slot = g % STAGE_SLOTS
            if g + STAGE_SLOTS - 1 < N_KV_HEADS:
                maxes.append(qk(g + STAGE_SLOTS - 1))
            m_old = ms[g]
            m_new = jnp.maximum(m_old, maxes[g])
            m_safe = jnp.where(m_new == NEG_INF, 0.0, m_new)
            alpha = jnp.exp2(m_old - m_safe)
            p_ref[slot, 0:w] = jnp.exp2(s_ref[slot, 0:w] - m_safe).astype(BF16)
            acc_ref[g] = acc_ref[g] * alpha + _dot(vt_of_g(g), p_ref[slot, 0:w])
            out.append(m_new)
        return tuple(out)

    def p3_cache(t, ms):
        r0 = pl.multiple_of(t * tk, tk)

        def vt(g):
            vg = cv_ref[0, pl.ds(r0 * N_KV_HEADS + g, tk, stride=N_KV_HEADS), :]
            return jnp.concatenate([vg.T, jnp.ones((BF16_ROWS, tk), F32)], axis=0).astype(BF16)

        return attend(
            t, lambda g: ck_ref[0, pl.ds(r0 * N_KV_HEADS + g, tk, stride=N_KV_HEADS), :].astype(BF16), vt, tk, ms)

    def p3_new(j, ms):
        r0 = pl.multiple_of(j * wn, wn)
        return attend(ntc + j,
                      lambda g: kn_ref[0, pl.ds(r0, wn), g * HEAD_DIM:(g + 1) * HEAD_DIM],
                      lambda g: vtn_ref[0, j, g * VT_ROWS:(g + 1) * VT_ROWS, :], wn, ms)

    ms = tuple(jnp.full((1, nq), NEG_INF, F32) for _ in range(N_KV_HEADS))
    if has_cache:
        ms = lax.fori_loop(0, ntc, p3_cache, ms)
    lax.fori_loop(0, ntn, p3_new, ms)

    for g in range(N_KV_HEADS):
        acc = acc_ref[g]
        og = (acc[0:HEAD_DIM] / acc[HEAD_DIM:HEAD_DIM + 1]).T
        for a in range(GROUP):
            hh = GROUP * g + a
            o_ref[0, :, hh * HEAD_DIM:(hh + 1) * HEAD_DIM] = og[a * tq:(a + 1) * tq, :].astype(o_ref.dtype)


def _dsa(qi, wi, q, kin, kn, vtn, cache, pos0, tq, tk):
    b, t, _ = q.shape
    has_cache = cache is not None
    l_cache = cache[0].shape[1] if has_cache else 0
    topk = min(TOPK_MAX, (l_cache + t) // 4)
    wn = min(tk, t)
    ntn = t // wn
    nt = l_cache // tk + ntn
    kvd = N_KV_HEADS * HEAD_DIM
    nq = GROUP * tq
    res = pl.Buffered(1) if b == 1 else None
    in_specs = [pl.BlockSpec((1, tq, IDX_HEADS * IDX_DIM), lambda bb, i: (bb, i, 0)),
                pl.BlockSpec((1, tq, LANES), lambda bb, i: (bb, i, 0)),
                pl.BlockSpec((1, tq, N_HEADS * HEAD_DIM), lambda bb, i: (bb, i, 0)),
                pl.BlockSpec((1, t, IDX_DIM), lambda bb, i: (bb, 0, 0), pipeline_mode=res),
                pl.BlockSpec((1, t, kvd), lambda bb, i: (bb, 0, 0), pipeline_mode=res),
                pl.BlockSpec((1, ntn, N_KV_HEADS * VT_ROWS, wn), lambda bb, i: (bb, 0, 0, 0), pipeline_mode=res)]
    args = [qi, wi, q, kin, kn, vtn]
    if has_cache:
        in_specs += [pl.BlockSpec((1, l_cache, IDX_DIM), lambda bb, i: (bb, 0, 0)),
                     pl.BlockSpec((1, l_cache * N_KV_HEADS, HEAD_DIM), lambda bb, i: (bb, 0, 0)),
                     pl.BlockSpec((1, l_cache * N_KV_HEADS, HEAD_DIM), lambda bb, i: (bb, 0, 0))]
        args += list(cache)
    kern = functools.partial(_dsa_kernel, tq=tq, tk=tk, t_new=t, l_cache=l_cache, pos0=pos0, topk=topk,
                             has_cache=has_cache)
    return pl.pallas_call(
        kern,
        out_shape=jax.ShapeDtypeStruct((b, t, N_HEADS * HEAD_DIM), BF16),
        grid=(b, t // tq),
        in_specs=in_specs,
        out_specs=pl.BlockSpec((1, tq, N_HEADS * HEAD_DIM), lambda bb, i: (bb, i, 0)),
        scratch_shapes=[pltpu.VMEM((nt, tk, tq), F32),
                        pltpu.VMEM((IDX_DIM, IDX_HEADS * tq), BF16),
                        pltpu.VMEM((LANES, tq), F32),
                        pltpu.VMEM((N_KV_HEADS, HEAD_DIM, nq), BF16),
                        pltpu.VMEM((N_KV_HEADS, VT_ROWS, nq), F32),
                        pltpu.VMEM((STAGE_SLOTS, tk, nq), F32),
                        pltpu.VMEM((STAGE_SLOTS, tk, nq), BF16),
                        pltpu.VMEM((1, tq), F32),
                        pltpu.VMEM((nt if (tq < LANES and t <= tk) else 1, tq, tk), F32)],
        compiler_params=_cparams(("arbitrary", "arbitrary")),
        name="dsa",
    )(*args)


def _hgrn_kernel(*refs, c, nchunk, nh, has_state):
    if has_state:
        q_ref, lf_ref, kk_ref, v_ref, hg_ref, nw_ref, s0_ref, o_ref, sout_ref, st_ref, g_ref = refs
    else:
        q_ref, lf_ref, kk_ref, v_ref, hg_ref, nw_ref, o_ref, sout_ref, st_ref, g_ref = refs
        s0_ref = None
    r = pl.program_id(2)

    @pl.when(r == 0)
    def _():
        for hh in range(nh):
            if has_state:
                st_ref[hh] = s0_ref[0, hh].T
            else:
                st_ref[hh] = jnp.zeros((HG_DV, HG_DK), F32)

    ri = lax.broadcasted_iota(jnp.int32, (c, c), 0)
    ci = lax.broadcasted_iota(jnp.int32, (c, c), 1)
    tri = jnp.where(ri >= ci, 1.0, 0.0).astype(BF16)
    nsub = c // SUB_BLOCK
    pair = lax.broadcasted_iota(jnp.int32, (SUB_BLOCK * SUB_BLOCK, HG_DK), 0)
    cap = jnp.where((pair % SUB_BLOCK) >= (pair // SUB_BLOCK), 0.0, NEG_INF)
    ones_w = jnp.ones((HG_DK, LANES), BF16)

    def rep_rows(row):
        return jnp.concatenate(
            [jnp.broadcast_to(row(sg), (SUB_BLOCK, HG_DK)) for sg in range(SUB_BLOCK)], axis=0)

    def cumulative(base):
        lf = lf_ref[pl.ds(base, c), :]
        l1 = lf.astype(BF16)
        r1 = lf - l1.astype(F32)
        l2 = r1.astype(BF16)
        l3 = (r1 - l2.astype(F32)).astype(BF16)
        return _dot(tri, l1) + _dot(tri, l2) + _dot(tri, l3)

    def stage_a(base, hh, g_all):
        cs = slice(hh * HG_DK, (hh + 1) * HG_DK)
        G = g_all[:, cs]
        kk = kk_ref[pl.ds(base, c), cs]
        v = v_ref[pl.ds(base, c), cs].astype(F32)
        g_ref[0, hh] = G
        g_ref[1, hh] = kk
        g_ref[2, hh] = v
        return dict(cs=cs, G=G, hh=hh, q=q_ref[pl.ds(base, c), cs].astype(F32), kk=kk, v=v)

    def stage_b(hh, d):
        q, kk, v, G = d["q"], d["kk"], d["v"], d["G"]
        st = st_ref[hh]
        vb = v.astype(BF16)
        d["vb"] = vb
        d["o_inter"] = _dot_nt((q * jnp.exp2(G)).astype(BF16), st.astype(BF16))
        d["A"], diag = [], []
        for i in range(nsub):
            lo_, hi_ = i * SUB_BLOCK, (i + 1) * SUB_BLOCK
            qi_ = q[lo_:hi_]
            Gi = G[lo_:hi_]
            if i > 0:
                Gb = G[lo_ - 1:lo_]
                qt = qi_ * jnp.exp2(Gi - Gb)
                kt = kk[:lo_] * jnp.exp2(Gb - G[:lo_])
                a_blk = _dot_nt(qt.astype(BF16), kt.astype(BF16))
                d["A"].append(jnp.concatenate([a_blk, jnp.zeros((SUB_BLOCK, c - lo_), F32)], axis=1))
            qrep = jnp.concatenate([qi_] * SUB_BLOCK, axis=0)
            grep = jnp.concatenate([Gi] * SUB_BLOCK, axis=0)
            kk_rep = rep_rows(lambda sg: g_ref[1, hh, lo_ + sg:lo_ + sg + 1, :])
            g_rep = rep_rows(lambda sg: g_ref[0, hh, lo_ + sg:lo_ + sg + 1, :])
            diag.append((qrep * kk_rep * jnp.exp2(jnp.minimum(grep - g_rep, cap))).astype(BF16))
        rs = _dot(jnp.concatenate(diag, axis=0) if nsub > 1 else diag[0], ones_w)
        pairs = SUB_BLOCK * SUB_BLOCK
        d["rs"] = [rs[i * pairs:(i + 1) * pairs] for i in range(nsub)]
        Gl = G[c - 1:c]
        kdec = kk * jnp.exp2(Gl - G)
        st_ref[hh] = st * jnp.exp2(Gl) + _dot_tn(vb, kdec.astype(BF16))

    def stage_c(base, d):
        parts = []
        o_base = d["o_inter"]
        if nsub > 1:
            a_low = jnp.concatenate([jnp.zeros((SUB_BLOCK, c), F32)] + d["A"], axis=0)
            o_base = o_base + _dot(a_low.astype(BF16), d["vb"])
        for i in range(nsub):
            lo_, hi_ = i * SUB_BLOCK, (i + 1) * SUB_BLOCK
            oi = o_base[lo_:hi_]
            v_rep = rep_rows(lambda sg: g_ref[2, d["hh"], lo_ + sg:lo_ + sg + 1, :])
            contrib = (d["rs"][i] * v_rep).reshape(SUB_BLOCK, SUB_BLOCK, HG_DV)
            parts.append(oi + jnp.sum(contrib, axis=0))
        o = jnp.concatenate(parts, axis=0) if nsub > 1 else parts[0]
        on = o * lax.rsqrt(jnp.mean(o * o, axis=1, keepdims=True) + EPS) * nw_ref[...]
        hg = hg_ref[pl.ds(base, c), d["cs"]].astype(F32)
        o_ref[pl.ds(base, c), d["cs"]] = (on * (hg * _sigmoid(hg))).astype(o_ref.dtype)

    def chunk(n, carry):
        base = pl.multiple_of(n * c, c)
        g_all = cumulative(base)
        heads = [stage_a(base, hh, g_all) for hh in range(nh)]
        for hh, d in enumerate(heads):
            stage_b(hh, d)
        for d in heads:
            stage_c(base, d)
        return carry

    lax.fori_loop(0, nchunk, chunk, 0)

    @pl.when(r == pl.num_programs(2) - 1)
    def _():
        for hh in range(nh):
            sout_ref[0, hh] = st_ref[hh].T


def _hgrn(hq, lf, kk, hv, hg, nw, s0, b, t, rb, nh):
    c = min(CHUNK, t)
    nchunk = rb // c
    nr = t // rb
    has_state = s0 is not None
    blk = pl.BlockSpec((rb, nh * HG_DK), lambda bb, h, r: (bb * nr + r, h))
    sblk = pl.BlockSpec((1, nh, HG_DK, HG_DV), lambda bb, h, r: (bb, h, 0, 0))
    in_specs = [blk, blk, blk, blk, blk, pl.BlockSpec((1, HG_DV), lambda bb, h, r: (0, 0))]
    args = [hq, lf, kk, hv, hg, nw]
    if has_state:
        in_specs.append(sblk)
        args.append(s0)
    kern = functools.partial(_hgrn_kernel, c=c, nchunk=nchunk, nh=nh, has_state=has_state)
    return pl.pallas_call(
        kern,
        out_shape=(jax.ShapeDtypeStruct((b * t, HG_HEADS * HG_DV), BF16),
                   jax.ShapeDtypeStruct((b, HG_HEADS, HG_DK, HG_DV), F32)),
        grid=(b, HG_HEADS // nh, nr),
        in_specs=in_specs,
        out_specs=(blk, sblk),
        scratch_shapes=[pltpu.VMEM((nh, HG_DV, HG_DK), F32), pltpu.VMEM((3, nh, c, HG_DK), F32)],
        compiler_params=_cparams(("arbitrary", "arbitrary", "arbitrary")),
        name="hgrn",
    )(*args)


def _merge_out_kernel(x_ref, oa_ref, oh_ref, ga_ref, gb_ref, w_ref, g1_ref, sc_ref, sh_ref, nw_ref,
                      x1_ref, h2_ref):
    bb, tb, d = x_ref.shape
    merged = (_sigmoid(ga_ref[...].astype(F32)) * oa_ref[...].astype(F32)
              + _sigmoid(gb_ref[...].astype(F32)) * oh_ref[...].astype(F32))
    y = _dot(merged.astype(BF16), w_ref[...]).reshape(bb, tb, d)
    x1 = x_ref[...] + g1_ref[...] * y
    x1_ref[...] = x1
    ms = jnp.mean(x1 * x1, axis=-1, keepdims=True)
    xn = x1 * lax.rsqrt(ms + EPS) * nw_ref[...]
    h2 = xn * (1.0 + sc_ref[...]) + sh_ref[...]
    h2_ref[...] = h2.reshape(bb * tb, d).astype(BF16)


def _merge_out(x, oa, oh, ga, gb, w_out, g1, sc2, sh2, nw2, bb, tb):
    b, t, d = x.shape
    tm = bb * tb
    nt = t // tb

    def row(i, j):
        return (i * nt + j, 0)

    def mod(i, j):
        return (i, 0, 0)

    x3 = pl.BlockSpec((bb, tb, d), lambda i, j: (i, j, 0))
    r2 = pl.BlockSpec((tm, d), row)
    return pl.pallas_call(
        _merge_out_kernel,
        out_shape=(jax.ShapeDtypeStruct((b, t, d), F32), jax.ShapeDtypeStruct((b * t, d), BF16)),
        grid=(b // bb, nt),
        in_specs=[x3, r2, r2, r2, r2,
                  pl.BlockSpec((d, d), lambda i, j: (0, 0)),
                  pl.BlockSpec((bb, 1, d), mod), pl.BlockSpec((bb, 1, d), mod), pl.BlockSpec((bb, 1, d), mod),
                  pl.BlockSpec((1, 1, d), lambda i, j: (0, 0, 0))],
        out_specs=(x3, r2),
        compiler_params=_cparams(("arbitrary", "arbitrary")),
        name="merge_out",
    )(x, oa, oh, ga, gb, w_out, g1, sc2, sh2, nw2)


def _mlp_kernel(h_ref, wu_ref, wd_ref, x1_ref, g2_ref, fw_ref, y_ref, acc_ref):
    f = pl.program_id(2)

    @pl.when(f == 0)
    def _():
        acc_ref[...] = jnp.zeros(acc_ref.shape, F32)

    u = jnp.maximum(_dot(h_ref[...], wu_ref[...]), 0.0)
    acc_ref[...] += _dot((u * u).astype(BF16), wd_ref[...])

    @pl.when(f == pl.num_programs(2) - 1)
    def _():
        bb, tb, d = x1_ref.shape
        x2 = x1_ref[...] + g2_ref[...] * acc_ref[...].reshape(bb, tb, d)
        ms = jnp.mean(x2 * x2, axis=-1, keepdims=True)
        y_ref[...] = x2 * lax.rsqrt(ms + EPS) * fw_ref[...]


def _mlp(h2, w_up, w_down, x1, g2, fw, bb, tb, tf):
    b, t, d = x1.shape
    dff = w_up.shape[1]
    tm = bb * tb
    nt = t // tb
    x3 = pl.BlockSpec((bb, tb, d), lambda i, j, f: (i, j, 0))
    return pl.pallas_call(
        _mlp_kernel,
        out_shape=jax.ShapeDtypeStruct((b, t, d), F32),
        grid=(b // bb, nt, dff // tf),
        in_specs=[pl.BlockSpec((tm, d), lambda i, j, f: (i * nt + j, 0)),
                  pl.BlockSpec((d, tf), lambda i, j, f: (0, f)),
                  pl.BlockSpec((tf, d), lambda i, j, f: (f, 0)),
                  x3,
                  pl.BlockSpec((bb, 1, d), lambda i, j, f: (i, 0, 0)),
                  pl.BlockSpec((1, 1, d), lambda i, j, f: (0, 0, 0))],
        out_specs=x3,
        scratch_shapes=[pltpu.VMEM((tm, d), F32)],
        compiler_params=_cparams(("arbitrary", "arbitrary", "arbitrary")),
        name="mlp",
    )(h2, w_up, w_down, x1, g2, fw)


def _rope_tables(pos):
    half = ROT_DIM // 2
    inv_freq = ROPE_THETA ** (-(jnp.arange(half, dtype=F32) * (2.0 / ROT_DIM)))
    ang = pos.astype(F32)[:, None] * inv_freq[None, :]
    cos, sin = jnp.cos(ang), jnp.sin(ang)
    n = pos.shape[0]
    ones = jnp.ones((n, LANES - ROT_DIM), F32)
    zeros = jnp.zeros((n, LANES - ROT_DIM), F32)
    zh = jnp.zeros((n, half), F32)
    c_t = jnp.concatenate([cos, cos, ones], axis=1)
    s_up = jnp.concatenate([-sin, zh, zeros], axis=1)
    s_dn = jnp.concatenate([zh, sin, zeros], axis=1)
    return c_t, s_up, s_dn


def _trunk(x, mod, pos0, past, wts, blocks):
    (norm1_w, w_parts, lb_logits, hg_norm_w, w_out, norm2_w, w_up, w_down, final_w) = wts
    b, t, d = x.shape
    bb, tb, tm_proj, tq, tk, rb, mlp_bb, mlp_tb = blocks
    m = [mod[:, i:i + 1, :] for i in range(6)]
    sh1, sc1, g1, sh2, sc2, g2 = m
    h = _normmod(x, sc1, sh1, norm1_w.reshape(1, 1, d), bb, min(t, 2 * tb)).reshape(b * t, d)

    pos = pos0 + jnp.arange(t, dtype=jnp.int32)
    tabs = tuple(jnp.tile(tb_, (b, 1)) for tb_ in _rope_tables(pos))
    wq, wk, wv, wqi, wki, wwi, whq, whf, whi, whg, wga, wgb = w_parts
    tm = min(tm_proj, b * t)
    wn = min(tk, t)
    (q_bf,) = _proj("rope", h, wq, tm, PROJ_TN, tabs, (BF16,), scale=HEAD_DIM ** -0.5 * LOG2E)
    k_f, k_bf = _proj("rope", h, wk, tm, PROJ_TN, tabs, (F32, BF16), head_rows=(0,))
    v_f, vt_bf = _proj_v(h, wv, tm, wn)
    (qi_bf,) = _proj("rope", h, wqi, tm, PROJ_TN, tabs, (BF16,))
    ki_f, ki_bf = _proj("rope", h, wki, tm, LANES, tabs, (F32, BF16))
    (wi_f,) = _proj("plain", h, wwi, tm, LANES, (), (F32,), scale=IDX_HEADS ** -0.5 * IDX_DIM ** -0.5)
    (hq,) = _proj("plain", h, whq, tm, PROJ_TN, (), (BF16,))
    lf, kk = _proj("forget", h, whf, tm, PROJ_TN, (lb_logits,), (F32, F32))
    (hi,) = _proj("plain", h, whi, tm, PROJ_TN, (), (BF16,))
    (hg,) = _proj("plain", h, whg, tm, PROJ_TN, (), (BF16,))
    (ga,) = _proj("plain", h, wga, tm, PROJ_TN, (), (BF16,))
    (gb,) = _proj("plain", h, wgb, tm, PROJ_TN, (), (BF16,))

    r3 = lambda a: a.reshape(b, t, a.shape[-1])
    if past is None:
        cache, s0 = None, None
    else:
        ck, cv, cki, s0 = past
        lc = ck.shape[1]
        cache = (cki, ck.reshape(b, lc * N_KV_HEADS, HEAD_DIM), cv.reshape(b, lc * N_KV_HEADS, HEAD_DIM))
    vtn = vt_bf.reshape(b, t // wn, N_KV_HEADS * VT_ROWS, wn)
    o_attn = _dsa(r3(qi_bf), r3(wi_f), r3(q_bf), r3(ki_bf), r3(k_bf), vtn, cache, pos0, tq, tk)
    o_hg, s_new = _hgrn(hq, lf, kk, hi, hg, hg_norm_w.reshape(1, HG_DV), s0, b, t, rb, 16)

    x1, h2 = _merge_out(x, o_attn.reshape(b * t, d), o_hg, ga, gb, w_out, g1, sc2, sh2,
                        norm2_w.reshape(1, 1, d), bb, tb)
    y = _mlp(h2, w_up, w_down, x1, g2, final_w.reshape(1, 1, d), mlp_bb, mlp_tb, 1024)
    return (y, k_f.reshape(b, t, N_KV_HEADS, HEAD_DIM), v_f.reshape(b, t, N_KV_HEADS, HEAD_DIM),
            ki_f.reshape(b, t, IDX_DIM), s_new)


def kernel(x_prompt, x_sample, cache_k, cache_v, cache_ki, state_hgrn, c_prompt, c_sample, w_ada, b_ada, norm1_w,
           w_in, hg_lb_logits, hg_norm_w, w_out, norm2_w, w_up, w_down, final_norm_w):
    depth = w_in.shape[0]
    assert depth == 1
    d = x_prompt.shape[-1]
    bp, tp, _ = x_prompt.shape
    bs, ts, _ = x_sample.shape
    past_len = cache_k.shape[2]

    c_all = jnp.concatenate([c_prompt, c_sample], axis=0)
    nrow = c_all.shape[0]
    pad = (-nrow) % SUBLANES
    c_all = jnp.pad(c_all, ((0, pad), (0, 0)))
    mod = _ada(c_all, w_ada[0], b_ada[0].reshape(1, -1)).reshape(nrow + pad, 6, d)

    sizes = (N_HEADS * HEAD_DIM, N_KV_HEADS * HEAD_DIM, N_KV_HEADS * HEAD_DIM, IDX_HEADS * IDX_DIM, IDX_DIM,
             IDX_HEADS, HG_HEADS * HG_DK, HG_HEADS * HG_DK, HG_HEADS * HG_DV, HG_HEADS * HG_DV, d, d)
    offs = np.concatenate([[0], np.cumsum(sizes)])
    w_t = jnp.transpose(w_in[0])
    w_parts = [(w_t, int(offs[i]), sz + (-sz) % LANES) for i, sz in enumerate(sizes)]
    wts = (norm1_w[0], tuple(w_parts), hg_lb_logits.astype(F32), hg_norm_w[0], w_out[0].astype(BF16), norm2_w[0],
           w_up[0].astype(BF16), w_down[0].astype(BF16), final_norm_w)

    yp, kp, vp, kip, sp = _trunk(x_prompt, mod[:bp], 0, None, wts, (1, 256, 2048, 128, 1024, 512, 1, 512))
    past = (cache_k[0], cache_v[0], cache_ki[0], state_hgrn[0])
    ys, ks, vs, kis, ss = _trunk(x_sample, mod[bp:bp + bs], past_len, past, wts,
                                 (bs // 2, ts, bs * ts, ts, 512, ts, bs, ts))
    return (yp, ys, kp[None], vp[None], kip[None], sp[None], ks[None], vs[None], kis[None], ss[None])
```
